```python
import jax, jax.numpy as jnp
from jax import lax
import numpy as np

D_MODEL = 1024
BATCH = 2
SEQ = 16384
DEPTH = 2

CONV_WIDTH = 3
HEAD_DIM = 64
N_HEADS = D_MODEL // HEAD_DIM
N_KV_GROUPS = 4
HEADS_PER_GROUP = N_HEADS // N_KV_GROUPS
N_BRANCH = 3
N_KV_SLOTS = 2 * N_BRANCH
CMP_BLOCK = 32
CMP_STRIDE = 16
CMP_HIDDEN = 256
SLC_BLOCK = 64
N_SELECT = 16
WINDOW = 512
Q_BLOCK = 128
FORCE_BONUS = 1000.0
ROPE_THETA = 10000.0
N_GROUPS = 4
EXPERTS_PER_GROUP = 8
N_EXPERTS = N_GROUPS * EXPERTS_PER_GROUP
TOP_K_INNER = 2
D_EXPERT = D_MODEL // 2
DISPATCH_ROWS = 256
RMS_EPS = 1e-6

kernel_name = 'hybrid_shortconv_nsa_hmoe'


def rmsnorm(x, g):
    xf = x.astype(jnp.float32)
    y = xf * lax.rsqrt(jnp.mean(xf * xf, axis=-1, keepdims=True) + RMS_EPS)
    return (y * g.astype(jnp.float32)).astype(x.dtype)


def rope(t, positions):
    half = HEAD_DIM // 2
    inv = ROPE_THETA ** (-jnp.arange(half, dtype=jnp.float32) / half)
    ang = positions.astype(jnp.float32)[..., None] * inv
    cos = jnp.cos(ang)[:, :, None, :]
    sin = jnp.sin(ang)[:, :, None, :]
    t1 = t[..., :half].astype(jnp.float32)
    t2 = t[..., half:].astype(jnp.float32)
    return jnp.concatenate([t1 * cos - t2 * sin, t2 * cos + t1 * sin], axis=-1).astype(t.dtype)


def masked_softmax(s, mask):
    s = jnp.where(mask, s.astype(jnp.float32), -jnp.inf)
    m = jnp.max(s, axis=-1, keepdims=True)
    m = jnp.where(jnp.isfinite(m), m, 0.0)
    e = jnp.exp(s - m)
    d = jnp.sum(e, axis=-1, keepdims=True)
    return e / jnp.maximum(d, jnp.float32(1e-30))


def short_conv_mixer(h, w_in, conv_w, w_out):
    d = h.shape[-1]
    proj = h @ w_in
    b_gate, c_gate, v = jnp.split(proj, 3, axis=-1)
    u = c_gate * v
    y = lax.conv_general_dilated(
        u, conv_w.astype(u.dtype)[:, None, :], window_strides=(1,),
        padding=[(CONV_WIDTH - 1, 0)], dimension_numbers=('NWC', 'WIO', 'NWC'),
        feature_group_count=d)
    return (b_gate * y) @ w_out


def nsa_shared_kv(h, positions, w_kv, cmp_pos_k, cmp_w1_k, cmp_w2_k, cmp_pos_v, cmp_w1_v, cmp_w2_v):
    b, s, _ = h.shape
    kv = (h @ w_kv).reshape(b, s, N_KV_SLOTS, N_KV_GROUPS, HEAD_DIM)
    k_c, v_c, k_s, v_s, k_w, v_w = [kv[:, :, i] for i in range(N_KV_SLOTS)]
    k_s = rope(k_s, positions)
    k_w = rope(k_w, positions)
    n_cmp = (s - CMP_BLOCK) // CMP_STRIDE + 1
    idx = jnp.arange(n_cmp)[:, None] * CMP_STRIDE + jnp.arange(CMP_BLOCK)[None, :]

    def compress(t, pos, w1, w2):
        blk = t[:, idx] + pos[None, None, :, None, :]
        blk = blk.transpose(0, 1, 3, 2, 4).reshape(b, n_cmp, N_KV_GROUPS, CMP_BLOCK * HEAD_DIM)
        return jax.nn.gelu(blk @ w1) @ w2

    k_cmp = compress(k_c, cmp_pos_k, cmp_w1_k, cmp_w2_k)
    v_cmp = compress(v_c, cmp_pos_v, cmp_w1_v, cmp_w2_v)
    n_slc = s // SLC_BLOCK
    k_slc = k_s.reshape(b, n_slc, SLC_BLOCK, N_KV_GROUPS, HEAD_DIM).transpose(0, 3, 1, 2, 4)
    v_slc = v_s.reshape(b, n_slc, SLC_BLOCK, N_KV_GROUPS, HEAD_DIM).transpose(0, 3, 1, 2, 4)
    pad = ((0, 0), (WINDOW, 0), (0, 0), (0, 0))
    k_win = jnp.pad(k_w, pad)
    v_win = jnp.pad(v_w, pad)
    return k_cmp, v_cmp, k_slc, v_slc, k_win, v_win


def selection_map(n_cmp, n_slc):
    c0 = jnp.arange(n_cmp) * CMP_STRIDE
    j0 = jnp.arange(n_slc) * SLC_BLOCK
    ov = jnp.minimum(c0[:, None] + CMP_BLOCK, j0[None, :] + SLC_BLOCK) - jnp.maximum(c0[:, None], j0[None, :])
    return jnp.clip(ov, 0).astype(jnp.float32) / CMP_BLOCK


def nsa_attention(h, positions, w_in, w_out, k_cmp, v_cmp, k_slc, v_slc, k_win, v_win):
    b, s, _ = h.shape
    dtype = h.dtype
    qd = N_HEADS * HEAD_DIM
    proj = h @ w_in
    q = proj[..., :qd].reshape(b, s, N_HEADS, HEAD_DIM)
    gates = jax.nn.sigmoid(proj[..., qd:].astype(jnp.float32)).reshape(
        b, s, N_KV_GROUPS, HEADS_PER_GROUP, N_BRANCH)
    scale = HEAD_DIM ** -0.5
    q_rot = (rope(q, positions) * scale).reshape(b, s, N_KV_GROUPS, HEADS_PER_GROUP, HEAD_DIM)
    q_cmp = (q * scale).reshape(b, s, N_KV_GROUPS, HEADS_PER_GROUP, HEAD_DIM)
    n_cmp = k_cmp.shape[1]
    n_slc = k_slc.shape[2]
    n_sel = min(N_SELECT, n_slc)
    cmp_end = jnp.arange(n_cmp) * CMP_STRIDE + CMP_BLOCK - 1
    slc_map = selection_map(n_cmp, n_slc)
    blk_ids = jnp.arange(n_slc)
    bi = jnp.arange(b)[:, None, None, None]
    gi = jnp.arange(N_KV_GROUPS)[None, None, :, None]

    def one_block(i):
        qs = i * Q_BLOCK
        t = qs + jnp.arange(Q_BLOCK)
        qc = lax.dynamic_slice_in_dim(q_cmp, qs, Q_BLOCK, axis=1)
        qr = lax.dynamic_slice_in_dim(q_rot, qs, Q_BLOCK, axis=1)
        gt = lax.dynamic_slice_in_dim(gates, qs, Q_BLOCK, axis=1)
        s_c = jnp.einsum('bqghd,bcgd->bqghc', qc, k_cmp)
        m_c = (cmp_end[None, :] <= t[:, None])[None, :, None, None, :]
        p_c = masked_softmax(s_c, m_c)
        o_c = jnp.einsum('bqghc,bcgd->bqghd', p_c.astype(dtype), v_cmp)
        imp = jnp.einsum('bqghc,cj->bqgj', p_c, slc_map)
        cur = t // SLC_BLOCK
        valid = blk_ids[None, :] * SLC_BLOCK <= t[:, None]
        forced = (blk_ids[None, :] == 0) | (blk_ids[None, :] == cur[:, None]) | (blk_ids[None, :] == cur[:, None] - 1)
        score = jnp.where(valid[None, :, None, :],
                          imp + jnp.where(forced, FORCE_BONUS, 0.0)[None, :, None, :], -1.0)
        _, sel = lax.top_k(score, n_sel)
        k_g = k_slc[bi, gi, sel]
        v_g = v_slc[bi, gi, sel]
        kpos = sel[..., None] * SLC_BLOCK + jnp.arange(SLC_BLOCK)
        s_s = jnp.einsum('bqghd,bqgnkd->bqghnk', qr, k_g).reshape(
            b, Q_BLOCK, N_KV_GROUPS, HEADS_PER_GROUP, n_sel * SLC_BLOCK)
        m_s = (kpos <= t[None, :, None, None, None]).reshape(b, Q_BLOCK, N_KV_GROUPS, 1, n_sel * SLC_BLOCK)
        p_s = masked_softmax(s_s, m_s).reshape(b, Q_BLOCK, N_KV_GROUPS, HEADS_PER_GROUP, n_sel, SLC_BLOCK)
        o_s = jnp.einsum('bqghnk,bqgnkd->bqghd', p_s.astype(dtype), v_g)
        kw = lax.dynamic_slice_in_dim(k_win, qs, Q_BLOCK + WINDOW, axis=1)
        vw = lax.dynamic_slice_in_dim(v_win, qs, Q_BLOCK + WINDOW, axis=1)
        wpos = qs - WINDOW + jnp.arange(Q_BLOCK + WINDOW)
        diff = t[:, None] - wpos[None, :]
        m_w = ((diff >= 0) & (diff < WINDOW) & (wpos[None, :] >= 0))[None, :, None, None, :]
        s_w = jnp.einsum('bqghd,bkgd->bqghk', qr, kw)
        p_w = masked_softmax(s_w, m_w)
        o_w = jnp.einsum('bqghk,bkgd->bqghd', p_w.astype(dtype), vw)
        o = (gt[..., 0:1] * o_c.astype(jnp.float32) + gt[..., 1:2] * o_s.astype(jnp.float32)
             + gt[..., 2:3] * o_w.astype(jnp.float32))
        return o.astype(dtype).reshape(b, Q_BLOCK, qd)

    out = lax.map(one_block, jnp.arange(s // Q_BLOCK))
    out = out.transpose(1, 0, 2, 3).reshape(b, s, qd)
    return out @ w_out


def hier_moe(h, w_rg, b_rg, w_re, b_re, w_gate, w_up, w_down):
    b, s, d = h.shape
    n_tok = b * s
    xf = h.reshape(n_tok, d)
    g_logits = (xf @ w_rg).astype(jnp.float32) + b_rg.astype(jnp.float32)
    g_prob = jax.nn.softmax(g_logits, axis=-1)
    g_top = jnp.argmax(g_logits, axis=-1)
    p_g = jnp.take_along_axis(g_prob, g_top[:, None], axis=1)[:, 0]
    e_logits = ((xf @ w_re).astype(jnp.float32) + b_re.astype(jnp.float32)).reshape(
        n_tok, N_GROUPS, EXPERTS_PER_GROUP)
    chosen = jnp.take_along_axis(e_logits, g_top[:, None, None], axis=1)[:, 0]
    top_vals, top_idx = lax.top_k(chosen, TOP_K_INNER)
    weights = p_g[:, None] * jax.nn.softmax(top_vals, axis=-1)
    expert = g_top[:, None] * EXPERTS_PER_GROUP + top_idx
    a_exp = expert.reshape(-1)
    a_tok = jnp.repeat(jnp.arange(n_tok), TOP_K_INNER)
    a_w = weights.reshape(-1)
    n_asg = n_tok * TOP_K_INNER
    onehot = jax.nn.one_hot(a_exp, N_EXPERTS, dtype=jnp.int32)
    rank = jnp.take_along_axis(jnp.cumsum(onehot, axis=0), a_exp[:, None], axis=1)[:, 0] - 1
    counts = jnp.sum(onehot, axis=0)
    padded = (counts + DISPATCH_ROWS - 1) // DISPATCH_ROWS * DISPATCH_ROWS
    pend = jnp.cumsum(padded)
    pstart = pend - padded
    dest = pstart[a_exp] + rank
    n_chunks = -(-n_asg // DISPATCH_ROWS) + N_EXPERTS
    n_rows = n_chunks * DISPATCH_ROWS
    row_tok = jnp.zeros((n_rows,), jnp.int32).at[dest].set(a_tok)
    row_w = jnp.zeros((n_rows,), jnp.float32).at[dest].set(a_w)
    chunk_exp = jnp.clip(jnp.searchsorted(pend, jnp.arange(n_chunks) * DISPATCH_ROWS, side='right'),
                         0, N_EXPERTS - 1)
    xs = xf[row_tok].reshape(n_chunks, DISPATCH_ROWS, d)

    def run_chunk(args):
        xc, e = args
        return (jax.nn.silu(xc @ w_gate[e]) * (xc @ w_up[e])) @ w_down[e]

    ys = lax.map(run_chunk, (xs, chunk_exp)).reshape(n_rows, d)
    out = jax.ops.segment_sum(ys * row_w[:, None].astype(ys.dtype), row_tok, num_segments=n_tok)
    return out.reshape(b, s, d).astype(h.dtype)


def setup_inputs(seed: int = 0) -> dict:
    key = jax.random.key(seed)
    ks = iter(jax.random.split(key, 32))
    f32 = jnp.float32

    def nrm(shape, scale):
        return jax.random.normal(next(ks), shape, f32) * scale

    d = D_MODEL
    n_a = DEPTH // 2
    n_b = DEPTH - n_a
    gd = N_KV_GROUPS * HEAD_DIM
    qd = N_HEADS * HEAD_DIM
    cin = CMP_BLOCK * HEAD_DIM
    return {
        'x': nrm((BATCH, SEQ, d), 1.0),
        'positions': jnp.tile(jnp.arange(SEQ, dtype=jnp.int32)[None, :], (BATCH, 1)),
        'norm_mix': 1.0 + nrm((DEPTH, d), 0.02),
        'norm_ffn': 1.0 + nrm((DEPTH, d), 0.02),
        'conv_w_in': nrm((n_a, d, 3 * d), d ** -0.5),
        'conv_w': nrm((n_a, CONV_WIDTH, d), CONV_WIDTH ** -0.5),
        'conv_w_out': nrm((n_a, d, d), d ** -0.5),
        'norm_kv': 1.0 + nrm((d,), 0.02),
        'w_kv': nrm((d, N_KV_SLOTS * gd), d ** -0.5),
        'cmp_pos_k': nrm((CMP_BLOCK, HEAD_DIM), 0.1),
        'cmp_w1_k': nrm((cin, CMP_HIDDEN), cin ** -0.5),
        'cmp_w2_k': nrm((CMP_HIDDEN, HEAD_DIM), CMP_HIDDEN ** -0.5),
        'cmp_pos_v': nrm((CMP_BLOCK, HEAD_DIM), 0.1),
        'cmp_w1_v': nrm((cin, CMP_HIDDEN), cin ** -0.5),
        'cmp_w2_v': nrm((CMP_HIDDEN, HEAD_DIM), CMP_HIDDEN ** -0.5),
        'attn_w_in': nrm((n_b, d, qd + N_BRANCH * N_HEADS), d ** -0.5),
        'attn_w_out': nrm((n_b, qd, d), qd ** -0.5),
        'router_group_w': nrm((DEPTH, d, N_GROUPS), d ** -0.5),
        'router_group_b': nrm((DEPTH, N_GROUPS), 0.01),
        'router_expert_w': nrm((DEPTH, d, N_EXPERTS), d ** -0.5),
        'router_expert_b': nrm((DEPTH, N_EXPERTS), 0.01),
        'expert_w_gate': nrm((DEPTH, N_EXPERTS, d, D_EXPERT), d ** -0.5),
        'expert_w_up': nrm((DEPTH, N_EXPERTS, d, D_EXPERT), d ** -0.5),
        'expert_w_down': nrm((DEPTH, N_EXPERTS, D_EXPERT, d), D_EXPERT ** -0.5),
        'norm_final': 1.0 + nrm((d,), 0.02),
    }


def reference(x, positions, norm_mix, norm_ffn, conv_w_in, conv_w, conv_w_out, norm_kv, w_kv,
              cmp_pos_k, cmp_w1_k, cmp_w2_k, cmp_pos_v, cmp_w1_v, cmp_w2_v, attn_w_in, attn_w_out,
              router_group_w, router_group_b, router_expert_w, router_expert_b,
              expert_w_gate, expert_w_up, expert_w_down, norm_final):
    n_a = DEPTH // 2
    shared = None
    for l in range(DEPTH):
        if l < n_a:
            h = rmsnorm(x, norm_mix[l])
            x = x + short_conv_mixer(h, conv_w_in[l], conv_w[l], conv_w_out[l])
        else:
            if shared is None:
                shared = nsa_shared_kv(rmsnorm(x, norm_kv), positions, w_kv, cmp_pos_k, cmp_w1_k, cmp_w2_k,
                                       cmp_pos_v, cmp_w1_v, cmp_w2_v)
            h = rmsnorm(x, norm_mix[l])
            x = x + nsa_attention(h, positions, attn_w_in[l - n_a], attn_w_out[l - n_a], *shared)
        x = x + hier_moe(rmsnorm(x, norm_ffn[l]), router_group_w[l], router_group_b[l],
                         router_expert_w[l], router_expert_b[l],
                         expert_w_gate[l], expert_w_up[l], expert_w_down[l])
    return rmsnorm(x, norm_final)
```

```python
import functools

import jax
import jax.numpy as jnp
from jax import lax
from jax.experimental import pallas as pl
from jax.experimental.pallas import tpu as pltpu

F32 = jnp.float32
BF16 = jnp.bfloat16
I32 = jnp.int32

RMS_EPS = 1e-6
ROPE_THETA = 10000.0
HEAD_DIM = 64
HALF = HEAD_DIM // 2
N_HEADS = 16
N_KV_GROUPS = 4
HEADS_PER_GROUP = N_HEADS // N_KV_GROUPS
N_BRANCH = 3
CMP_BLOCK = 32
CMP_STRIDE = 16
SLC_BLOCK = 64
N_SELECT = 16
WINDOW = 512
Q_BLOCK = 128
FORCE_BONUS = 1000.0
N_GROUPS = 4
EXPERTS_PER_GROUP = 8
N_EXPERTS = N_GROUPS * EXPERTS_PER_GROUP
CHUNK_ROWS = 256
NEG = -1e30
LANES = 128
KEY_TILE = 512
BIAS_ROWS = 16
VMEM_LIMIT = 56 * 1024 * 1024


def _cparams(*sem):
    return pltpu.CompilerParams(dimension_semantics=sem, vmem_limit_bytes=VMEM_LIMIT)


def _rms_scale(x):
    return x * lax.rsqrt(jnp.mean(x * x, axis=-1, keepdims=True) + RMS_EPS)


def _dot(a, b):
    return jnp.dot(a, b, preferred_element_type=F32)


def _dot_nt(a, b):
    return lax.dot_general(a, b, (((1,), (1,)), ((), ())), preferred_element_type=F32)


def _dot_tn(a, b):
    return lax.dot_general(a, b, (((0,), (0,)), ((), ())), preferred_element_type=F32)


def _mixer_kernel(x_ref, g_ref, win_ref, cw_ref, wout_ref, o_ref, tail_ref):
    d = x_ref.shape[-1]
    tm = x_ref.shape[1]

    @pl.when(pl.program_id(1) == 0)
    def _():
        tail_ref[...] = jnp.zeros_like(tail_ref)

    x = x_ref[0]
    h = (_rms_scale(x) * g_ref[...]).astype(BF16)
    proj = _dot(h, win_ref[...])
    b_gate = proj[:, :d]
    u = proj[:, d:2 * d] * proj[:, 2 * d:]
    row = lax.broadcasted_iota(I32, (tm, 1), 0)
    prev1 = tail_ref[7:8, :]
    prev2 = tail_ref[6:7, :]
    u1 = jnp.where(row >= 1, pltpu.roll(u, 1, 0), prev1)
    u2 = jnp.where(row >= 2, pltpu.roll(u, 2, 0), jnp.where(row == 1, prev1, prev2))
    y = cw_ref[2:3, :] * u + cw_ref[1:2, :] * u1 + cw_ref[0:1, :] * u2
    tail_ref[...] = u[tm - 8:, :]
    z = (b_gate * y).astype(BF16)
    o_ref[0] = x + _dot(z, wout_ref[...])


def _mixer(x, g, w_in, conv_w, w_out, tm=512):
    b, s, d = x.shape
    return pl.pallas_call(
        _mixer_kernel,
        grid=(b, s // tm),
        in_specs=[
            pl.BlockSpec((1, tm, d), lambda i, j: (i, j, 0)),
            pl.BlockSpec((1, d), lambda i, j: (0, 0)),
            pl.BlockSpec((d, 3 * d), lambda i, j: (0, 0)),
            pl.BlockSpec((3, d), lambda i, j: (0, 0)),
            pl.BlockSpec((d, d), lambda i, j: (0, 0)),
        ],
        out_specs=pl.BlockSpec((1, tm, d), lambda i, j: (i, j, 0)),
        out_shape=jax.ShapeDtypeStruct((b, s, d), F32),
        scratch_shapes=[pltpu.VMEM((8, d), F32)],
        compiler_params=_cparams("arbitrary", "arbitrary"),
        name="conv_mixer",
    )(x, g.reshape(1, d), w_in.astype(BF16), conv_w, w_out.astype(BF16))


def _router_kernel(x_ref, g_ref, wt_ref, b_ref, hb_ref, eid_ref, wts_ref):
    tm = x_ref.shape[0]
    h = _rms_scale(x_ref[...]) * g_ref[...]
    hb_ref[...] = h.astype(BF16)
    lt = lax.dot_general(wt_ref[...], h, (((1,), (1,)), ((), ())),
                         precision=lax.Precision.HIGHEST, preferred_element_type=F32) + b_ref[...]
    r8 = lax.broadcasted_iota(I32, (8, 1), 0)
    gl = jnp.where(r8 < N_GROUPS, lt[0:8], -jnp.inf)
    gmax = jnp.max(gl, axis=0, keepdims=True)
    gtop = jnp.min(jnp.where(gl == gmax, r8, 8), axis=0, keepdims=True)
    p_g = 1.0 / jnp.sum(jnp.exp(gl - gmax), axis=0, keepdims=True)
    chosen = lt[8:16]
    for k in range(1, N_GROUPS):
        chosen = jnp.where(gtop == k, lt[8 + 8 * k:16 + 8 * k], chosen)
    v0 = jnp.max(chosen, axis=0, keepdims=True)
    i0 = jnp.min(jnp.where(chosen == v0, r8, 8), axis=0, keepdims=True)
    rest = jnp.where(r8 == i0, -jnp.inf, chosen)
    v1 = jnp.max(rest, axis=0, keepdims=True)
    i1 = jnp.min(jnp.where(rest == v1, r8, 8), axis=0, keepdims=True)
    e = jnp.exp(v1 - v0)
    w0 = p_g / (1.0 + e)
    w1 = p_g * e / (1.0 + e)
    base = gtop * EXPERTS_PER_GROUP
    eid_ref[...] = jnp.where(r8 == 0, base + i0, jnp.where(r8 == 1, base + i1, 0))
    wts_ref[...] = jnp.where(r8 == 0, w0, jnp.where(r8 == 1, w1, 0.0))


def _router(xf, g, w_rg, b_rg, w_re, b_re, tm=512):
    n, d = xf.shape
    rows = 8 + N_EXPERTS
    wt = jnp.zeros((rows, d), F32).at[:N_GROUPS].set(w_rg.T).at[8:].set(w_re.T)
    bt = jnp.zeros((rows, 1), F32).at[:N_GROUPS, 0].set(b_rg).at[8:, 0].set(b_re)
    return pl.pallas_call(
        _router_kernel,
        grid=(n // tm,),
        in_specs=[
            pl.BlockSpec((tm, d), lambda i: (i, 0)),
            pl.BlockSpec((1, d), lambda i: (0, 0)),
            pl.BlockSpec((rows, d), lambda i: (0, 0)),
            pl.BlockSpec((rows, 1), lambda i: (0, 0)),
        ],
        out_specs=[
            pl.BlockSpec((tm, d), lambda i: (i, 0)),
            pl.BlockSpec((8, tm), lambda i: (0, i)),
            pl.BlockSpec((8, tm), lambda i: (0, i)),
        ],
        out_shape=[
            jax.ShapeDtypeStruct((n, d), BF16),
            jax.ShapeDtypeStruct((8, n), I32),
            jax.ShapeDtypeStruct((8, n), F32),
        ],
        compiler_params=_cparams("arbitrary"),
        name="moe_router",
    )(xf, g.reshape(1, d), wt, bt)


def _dispatch_plan(eid, n_tok):
    a_exp = eid.reshape(-1)
    n_asg = a_exp.shape[0]
    a_tok = jnp.arange(n_asg, dtype=I32) % n_tok
    onehot = (a_exp[:, None] == jnp.arange(N_EXPERTS, dtype=I32)[None, :]).astype(I32)
    csum = jnp.cumsum(onehot, axis=0)
    rank = jnp.sum(csum * onehot, axis=1) - 1
    counts = csum[-1]
    padded = (counts + CHUNK_ROWS - 1) // CHUNK_ROWS * CHUNK_ROWS
    pend = jnp.cumsum(padded)
    pstart = pend - padded
    dest = (jnp.sum(pstart[None, :] * onehot, axis=1) + rank).astype(I32)
    n_chunks = -(-n_asg // CHUNK_ROWS) + N_EXPERTS
    row_tok = jnp.zeros((n_chunks * CHUNK_ROWS,), I32).at[dest].set(a_tok)
    chunk_exp = jnp.clip(jnp.searchsorted(pend, jnp.arange(n_chunks, dtype=I32) * CHUNK_ROWS, side='right'),
                         0, N_EXPERTS - 1).astype(I32)
    n_used = (pend[-1] // CHUNK_ROWS).astype(I32).reshape(1)
    return row_tok, dest.reshape(2, n_tok), chunk_exp, n_used, n_chunks


def _expert_kernel(ce_ref, nu_ref, xs_ref, wg_ref, wu_ref, wd_ref, o_ref):
    c = pl.program_id(0)

    @pl.when(c < nu_ref[0])
    def _():
        xc = xs_ref[...]
        gate = _dot(xc, wg_ref[0])
        up = _dot(xc, wu_ref[0])
        act = (gate * jax.nn.sigmoid(gate) * up).astype(BF16)
        o_ref[...] = _dot(act, wd_ref[0])

    @pl.when(c >= nu_ref[0])
    def _():
        o_ref[...] = jnp.zeros_like(o_ref)


def _experts(xs, chunk_exp, n_used, w_gate, w_up, w_down, n_chunks):
    d = xs.shape[1]
    de = w_gate.shape[-1]
    grid_spec = pltpu.PrefetchScalarGridSpec(
        num_scalar_prefetch=2,
        grid=(n_chunks,),
        in_specs=[
            pl.BlockSpec((CHUNK_ROWS, d), lambda c, ce, nu: (c, 0)),
            pl.BlockSpec((1, d, de), lambda c, ce, nu: (ce[c], 0, 0)),
            pl.BlockSpec((1, d, de), lambda c, ce, nu: (ce[c], 0, 0)),
            pl.BlockSpec((1, de, d), lambda c, ce, nu: (ce[c], 0, 0)),
        ],
        out_specs=pl.BlockSpec((CHUNK_ROWS, d), lambda c, ce, nu: (c, 0)),
    )
    return pl.pallas_call(
        _expert_kernel,
        grid_spec=grid_spec,
        out_shape=jax.ShapeDtypeStruct((n_chunks * CHUNK_ROWS, d), F32),
        compiler_params=_cparams("arbitrary"),
        name="moe_experts",
    )(chunk_exp, n_used, xs, w_gate, w_up, w_down)


def _combine_kernel(x_ref, y0_ref, y1_ref, w_ref, g_ref, o_ref, *, final_norm):
    w = w_ref[...]
    out = x_ref[...] + w[:, 0:1] * y0_ref[...] + w[:, 1:2] * y1_ref[...]
    if final_norm:
        out = _rms_scale(out) * g_ref[...]
    o_ref[...] = out


def _combine(xf, y0, y1, wcol, g, final_norm, tm=512):
    n, d = xf.shape
    row = pl.BlockSpec((tm, d), lambda i: (i, 0))
    return pl.pallas_call(
        functools.partial(_combine_kernel, final_norm=final_norm),
        grid=(n // tm,),
        in_specs=[row, row, row, pl.BlockSpec((tm, 8), lambda i: (i, 0)), pl.BlockSpec((1, d), lambda i: (0, 0))],
        out_specs=row,
        out_shape=jax.ShapeDtypeStruct((n, d), F32),
        compiler_params=_cparams("arbitrary"),
        name="moe_combine",
    )(xf, y0, y1, wcol, g.reshape(1, d))


def _hier_moe(xf, g_ffn, w_rg, b_rg, w_re, b_re, w_gate, w_up, w_down, g_final, final_norm):
    n = xf.shape[0]
    hb, eid, wts = _router(xf, g_ffn, w_rg, b_rg, w_re, b_re)
    row_tok, dest, chunk_exp, n_used, n_chunks = _dispatch_plan(eid[:2], n)
    xs = jnp.take(hb, row_tok, axis=0)
    ys = _experts(xs, chunk_exp, n_used, w_gate.astype(BF16), w_up.astype(BF16), w_down.astype(BF16), n_chunks)
    y0 = jnp.take(ys, dest[0], axis=0)
    y1 = jnp.take(ys, dest[1], axis=0)
    return _combine(xf, y0, y1, wts.T, g_final, final_norm)


def _proj_kernel(x_ref, prow_ref, pcol_ref, gkv_ref, gq_ref, wnat_ref, wvt_ref, wqt_ref,
                 kcv_ref, ks_ref, kw_ref, vst_ref, vwt_ref, qc_ref, qr_ref, gt_ref):
    tm = x_ref.shape[1]
    gd = N_KV_GROUPS * HEAD_DIM
    qd = N_HEADS * HEAD_DIM
    y = _rms_scale(x_ref[0])
    hkv = (y * gkv_ref[...]).astype(BF16)
    hq = (y * gq_ref[...]).astype(BF16)

    nat = _dot(hkv, wnat_ref[...])
    kcv_ref[0] = nat[:, :2 * gd]
    lane = lax.broadcasted_iota(I32, (1, LANES), 1)
    inv = jnp.exp((lane % HALF).astype(F32) * (-jnp.log(ROPE_THETA) / HALF))
    ang = pcol_ref[0].astype(F32) * inv
    ctab = jnp.where(lane < HEAD_DIM, jnp.cos(ang), 0.0)
    stab = jnp.where(lane < HEAD_DIM, jnp.sin(ang), 0.0)
    key = pl.program_id(1) * tm + lax.broadcasted_iota(I32, (tm, 1), 0)
    blk = (key // SLC_BLOCK) % BIAS_ROWS
    onehot = jnp.where(lane == HEAD_DIM + blk, 1.0, 0.0)
    for g in range(N_KV_GROUPS):
        for slot, ref in ((0, ks_ref), (1, kw_ref)):
            lo = 2 * gd + (slot * N_KV_GROUPS + g) * LANES
            t = nat[:, lo:lo + LANES]
            roped = t * ctab + pltpu.roll(t, HEAD_DIM, 1) * stab
            if slot == 0:
                roped = roped + onehot
            ref[0, g] = roped.astype(BF16)

    vt = _dot_nt(wvt_ref[...], hkv)
    vst_ref[0, :, 0] = vt[:gd].reshape(N_KV_GROUPS, HEAD_DIM, tm).astype(BF16)
    vw3 = vt[gd:].reshape(N_KV_GROUPS, HEAD_DIM, tm).astype(BF16)
    for w in range(tm // Q_BLOCK):
        vwt_ref[0, :, w] = vw3[:, :, w * Q_BLOCK:(w + 1) * Q_BLOCK]

    qt = _dot_nt(wqt_ref[...], hq)
    scale = HEAD_DIM ** -0.5
    q3 = qt[:qd].reshape(N_HEADS, HEAD_DIM, tm)
    qc_ref[0] = (qt[:qd] * scale).astype(BF16)
    frq = lax.broadcasted_iota(I32, (HALF, 1), 0).astype(F32)
    ang_t = jnp.exp(frq * (-jnp.log(ROPE_THETA) / HALF)) * prow_ref[0].astype(F32)
    cos_t = jnp.cos(ang_t)[None]
    sin_t = jnp.sin(ang_t)[None]
    t1 = q3[:, :HALF]
    t2 = q3[:, HALF:]
    qr = jnp.concatenate([t1 * cos_t - t2 * sin_t, t2 * cos_t + t1 * sin_t], axis=1)
    qr_ref[0] = (qr * scale).reshape(qd, tm).astype(BF16)
    gt_ref[0] = jax.nn.sigmoid(qt[qd:])


def _rot_half_cols(w):
    return jnp.concatenate([-w[:, HALF:], w[:, :HALF]], axis=1)


def _projections(x, positions, g_kv, g_q, w_kv, w_q, tm=KEY_TILE):
    b, s, d = x.shape
    gd = N_KV_GROUPS * HEAD_DIM
    qd = N_HEADS * HEAD_DIM
    slot = lambda i: w_kv[:, i * gd:(i + 1) * gd]
    aug = []
    for sl in (2, 4):
        for g in range(N_KV_GROUPS):
            wk = slot(sl)[:, g * HEAD_DIM:(g + 1) * HEAD_DIM]
            aug += [wk, _rot_half_cols(wk)]
    w_nat = jnp.concatenate([slot(0), slot(1)] + aug, axis=1).astype(BF16)
    w_vt = jnp.concatenate([slot(3), slot(5)], axis=1).T.astype(BF16)
    gate_cols = w_q[:, qd:].reshape(d, N_KV_GROUPS, HEADS_PER_GROUP * N_BRANCH)
    gate_cols = jnp.pad(gate_cols, ((0, 0), (0, 0), (0, 16 - HEADS_PER_GROUP * N_BRANCH))).reshape(d, 64)
    w_qt = jnp.concatenate([w_q[:, :qd], gate_cols], axis=1).T.astype(BF16)
    nt = s // tm
    n_nat = w_nat.shape[1]
    const = lambda shape: pl.BlockSpec(shape, lambda i, j: (0,) * len(shape))
    return pl.pallas_call(
        _proj_kernel,
        grid=(b, nt),
        in_specs=[
            pl.BlockSpec((1, tm, d), lambda i, j: (i, j, 0)),
            pl.BlockSpec((1, 1, tm), lambda i, j: (i, 0, j)),
            pl.BlockSpec((1, tm, 1), lambda i, j: (i, j, 0)),
            const((1, d)), const((1, d)), const((d, n_nat)), const((2 * gd, d)), const((qd + 64, d)),
        ],
        out_specs=[
            pl.BlockSpec((1, tm, 2 * gd), lambda i, j: (i, j, 0)),
            pl.BlockSpec((1, N_KV_GROUPS, tm, LANES), lambda i, j: (i, 0, j, 0)),
            pl.BlockSpec((1, N_KV_GROUPS, tm, LANES), lambda i, j: (i, 0, j, 0)),
            pl.BlockSpec((1, N_KV_GROUPS, 1, HEAD_DIM, tm), lambda i, j: (i, 0, j, 0, 0)),
            pl.BlockSpec((1, N_KV_GROUPS, tm // Q_BLOCK, HEAD_DIM, Q_BLOCK), lambda i, j: (i, 0, j, 0, 0)),
            pl.BlockSpec((1, qd, tm), lambda i, j: (i, 0, j)),
            pl.BlockSpec((1, qd, tm), lambda i, j: (i, 0, j)),
            pl.BlockSpec((1, 64, tm), lambda i, j: (i, 0, j)),
        ],
        out_shape=[
            jax.ShapeDtypeStruct((b, s, 2 * gd), F32),
            jax.ShapeDtypeStruct((b, N_KV_GROUPS, s, LANES), BF16),
            jax.ShapeDtypeStruct((b, N_KV_GROUPS, s, LANES), BF16),
            jax.ShapeDtypeStruct((b, N_KV_GROUPS, nt, HEAD_DIM, tm), BF16),
            jax.ShapeDtypeStruct((b, N_KV_GROUPS, s // Q_BLOCK, HEAD_DIM, Q_BLOCK), BF16),
            jax.ShapeDtypeStruct((b, qd, s), BF16),
            jax.ShapeDtypeStruct((b, qd, s), BF16),
            jax.ShapeDtypeStruct((b, 64, s), F32),
        ],
        compiler_params=_cparams("arbitrary", "arbitrary"),
        name="nsa_projections",
    )(x, positions.reshape(b, 1, s), positions.reshape(b, s, 1), g_kv.reshape(1, d), g_q.reshape(1, d),
      w_nat, w_vt, w_qt)


def _compress_kernel(a_ref, pos_ref, w1_ref, w2_ref, o_ref, *, transposed_out):
    half = w1_ref.shape[0] // 2
    a = a_ref[0, 0]
    top = (a + pos_ref[0:1, :]).astype(BF16)
    bot = (a + pos_ref[1:2, :]).astype(BF16)
    p = _dot(top, w1_ref[:half, :])
    q = _dot(bot, w1_ref[half:, :])
    n = a.shape[0]
    hidden = p + pltpu.roll(q, n - 1, 0)
    act = jax.nn.gelu(hidden, approximate=True).astype(BF16)
    if transposed_out:
        o_ref[0, 0] = _dot_nt(w2_ref[...], act).astype(BF16)
    else:
        o_ref[0, 0] = _dot(act, w2_ref[...]).astype(BF16)


def _compress(a, pos, w1, w2, transposed_out):
    b, g, n, width = a.shape
    hidden = w1.shape[1]
    pos2 = pos.reshape(2, width)
    if transposed_out:
        w2p = w2.T.astype(BF16)
        out_block, out_shape = (1, 1, HEAD_DIM, n), (b, g, HEAD_DIM, n)
    else:
        w2p = jnp.pad(w2, ((0, 0), (0, LANES - HEAD_DIM))).astype(BF16)
        out_block, out_shape = (1, 1, n, LANES), (b, g, n, LANES)
    return pl.pallas_call(
        functools.partial(_compress_kernel, transposed_out=transposed_out),
        grid=(b, g),
        in_specs=[
            pl.BlockSpec((1, 1, n, width), lambda i, j: (i, j, 0, 0)),
            pl.BlockSpec((2, width), lambda i, j: (0, 0)),
            pl.BlockSpec((2 * width, hidden), lambda i, j: (0, 0)),
            pl.BlockSpec(w2p.shape, lambda i, j: (0, 0)),
        ],
        out_specs=pl.BlockSpec(out_block, lambda i, j: (i, j, 0, 0)),
        out_shape=jax.ShapeDtypeStruct(out_shape, BF16),
        compiler_params=_cparams("arbitrary", "arbitrary"),
        name="nsa_compress_v" if transposed_out else "nsa_compress_k",
    )(a, pos2, w1.astype(BF16), w2p)


def _heads_on_lanes(ref):
    return jnp.concatenate([ref[0, h * HEAD_DIM:(h + 1) * HEAD_DIM, :] for h in range(HEADS_PER_GROUP)], axis=1)


def _nsa_kernel(qc_ref, qr_ref, g_ref, kc_ref, vct_ref, ks_ref, vst_ref, kw_ref, vwt_ref, mapt_ref,
                o_ref, bias_ref, score_ref, sel_ref):
    i = pl.program_id(2)
    qb = Q_BLOCK
    width = HEADS_PER_GROUP * qb
    n_cmp = kc_ref.shape[2]
    n_slc = mapt_ref.shape[0]
    t_q = i * qb + lax.broadcasted_iota(I32, (1, qb), 1)
    t_row = jnp.concatenate([t_q] * HEADS_PER_GROUP, axis=1)
    zeros_pad = jnp.zeros((LANES - HEAD_DIM, width), BF16)

    qc_t = jnp.concatenate([_heads_on_lanes(qc_ref), zeros_pad], axis=0)
    s_c = _dot(kc_ref[0, 0], qc_t)
    cmp_end = lax.broadcasted_iota(I32, (n_cmp, 1), 0) * CMP_STRIDE + (CMP_BLOCK - 1)
    m_c = cmp_end <= t_row
    s_c = jnp.where(m_c, s_c, NEG)
    mx = jnp.max(s_c, axis=0, keepdims=True)
    e_c = jnp.where(m_c, jnp.exp(s_c - mx), 0.0)
    rinv = 1.0 / jnp.maximum(jnp.sum(e_c, axis=0, keepdims=True), 1e-30)
    o_cmp = _dot(vct_ref[0, 0], e_c.astype(BF16)) * rinv
    p_c = e_c * rinv
    p_sum = p_c[:, 0:qb]
    for h in range(1, HEADS_PER_GROUP):
        p_sum = p_sum + p_c[:, h * qb:(h + 1) * qb]
    p_hi = p_sum.astype(BF16)
    p_lo = (p_sum - p_hi.astype(F32)).astype(BF16)
    imp = _dot(mapt_ref[...], p_hi) + _dot(mapt_ref[...], p_lo)

    j_idx = lax.broadcasted_iota(I32, (n_slc, 1), 0)
    cur = t_q // SLC_BLOCK
    valid = j_idx * SLC_BLOCK <= t_q
    forced = (j_idx == 0) | (j_idx == cur) | (j_idx == cur - 1)
    score_ref[...] = jnp.where(valid, imp + jnp.where(forced, FORCE_BONUS, 0.0), -1.0)
    sel_ref[...] = jnp.zeros_like(sel_ref)

    def pick(_, carry):
        sc = score_ref[...]
        best = jnp.max(sc, axis=0, keepdims=True)
        idx = jnp.min(jnp.where(sc == best, j_idx, n_slc), axis=0, keepdims=True)
        hit = j_idx == idx
        sel_ref[...] = jnp.where(hit, 1.0, sel_ref[...])
        score_ref[...] = jnp.where(hit, -2.0, sc)
        return carry

    lax.fori_loop(0, N_SELECT, pick, 0)
    bias = jnp.where(sel_ref[...] > 0.5, 0.0, NEG).astype(BF16)
    bias_ref[...] = jnp.concatenate([bias] * HEADS_PER_GROUP, axis=1)

    qr_heads = _heads_on_lanes(qr_ref)
    zeros_sel = jnp.zeros((LANES - HEAD_DIM - BIAS_ROWS, width), BF16)
    tk = KEY_TILE
    blocks_per_tile = tk // SLC_BLOCK

    def sel_step(kt, carry):
        m, l, acc = carry
        b0 = pl.multiple_of((kt * blocks_per_tile) // BIAS_ROWS * BIAS_ROWS, BIAS_ROWS)
        q_aug = jnp.concatenate([qr_heads, bias_ref[pl.ds(b0, BIAS_ROWS), :], zeros_sel], axis=0)
        k0 = pl.multiple_of(kt * tk, tk)
        s = _dot(ks_ref[0, 0, pl.ds(k0, tk), :], q_aug)
        kpos = k0 + lax.broadcasted_iota(I32, (tk, 1), 0)
        s = jnp.where(kpos <= t_row, s, NEG)
        m_new = jnp.maximum(m, jnp.max(s, axis=0, keepdims=True))
        alpha = jnp.exp(m - m_new)
        p = jnp.exp(s - m_new)
        l = alpha * l + jnp.sum(p, axis=0, keepdims=True)
        acc = alpha * acc + _dot(vst_ref[0, 0, kt], p.astype(BF16))
        return m_new, l, acc

    n_tiles = (i * qb + qb + tk - 1) // tk
    init = (jnp.full((1, width), NEG, F32), jnp.zeros((1, width), F32), jnp.zeros((HEAD_DIM, width), F32))
    _, l_s, acc_s = lax.fori_loop(0, n_tiles, sel_step, init)
    o_slc = acc_s / l_s

    qr_t = jnp.concatenate([qr_heads, zeros_pad], axis=0)
    n_win = WINDOW // qb + 1
    s_w = []
    for r in range(n_win):
        j = i - (n_win - 1) + r
        jc = jnp.maximum(j, 0)
        k0 = pl.multiple_of(jc * qb, qb)
        s = _dot(kw_ref[0, 0, pl.ds(k0, qb), :], qr_t)
        kpos = j * qb + lax.broadcasted_iota(I32, (qb, 1), 0)
        diff = t_row - kpos
        s_w.append(jnp.where((diff >= 0) & (diff < WINDOW) & (kpos >= 0), s, NEG))
    m_w = s_w[0].max(axis=0, keepdims=True)
    for r in range(1, n_win):
        m_w = jnp.maximum(m_w, s_w[r].max(axis=0, keepdims=True))
    l_w = jnp.zeros((1, width), F32)
    acc_w = jnp.zeros((HEAD_DIM, width), F32)
    for r in range(n_win):
        jc = jnp.maximum(i - (n_win - 1) + r, 0)
        p = jnp.exp(s_w[r] - m_w)
        l_w = l_w + jnp.sum(p, axis=0, keepdims=True)
        acc_w = acc_w + _dot(vwt_ref[0, 0, jc], p.astype(BF16))
    o_win = acc_w / l_w

    gts = g_ref[0]
    for h in range(HEADS_PER_GROUP):
        sl = slice(h * qb, (h + 1) * qb)
        r0 = h * N_BRANCH
        o = (gts[r0:r0 + 1] * o_cmp[:, sl] + gts[r0 + 1:r0 + 2] * o_slc[:, sl] + gts[r0 + 2:r0 + 3] * o_win[:, sl])
        o_ref[0, h * HEAD_DIM:(h + 1) * HEAD_DIM, :] = o.astype(BF16)


def _nsa_attention(qc_t, qr_t, gates_t, k_cmp, v_cmp_t, k_slc, v_slc_t, k_win, v_win_t, map_t):
    b, qd, s = qc_t.shape
    g = N_KV_GROUPS
    n_cmp = k_cmp.shape[2]
    n_slc = map_t.shape[0]
    nt, tk = v_slc_t.shape[2], v_slc_t.shape[4]
    rows = HEADS_PER_GROUP * HEAD_DIM
    qspec = pl.BlockSpec((1, rows, Q_BLOCK), lambda bi, gi, i: (bi, gi, i))
    per_bg = lambda shape: pl.BlockSpec((1, 1) + shape, lambda bi, gi, i: (bi, gi) + (0,) * len(shape))
    return pl.pallas_call(
        _nsa_kernel,
        grid=(b, g, s // Q_BLOCK),
        in_specs=[
            qspec, qspec,
            pl.BlockSpec((1, 16, Q_BLOCK), lambda bi, gi, i: (bi, gi, i)),
            per_bg((n_cmp, LANES)), per_bg((HEAD_DIM, n_cmp)),
            per_bg((s, LANES)), per_bg((nt, HEAD_DIM, tk)),
            per_bg((s, LANES)), per_bg((s // Q_BLOCK, HEAD_DIM, Q_BLOCK)),
            pl.BlockSpec((n_slc, n_cmp), lambda bi, gi, i: (0, 0)),
        ],
        out_specs=qspec,
        out_shape=jax.ShapeDtypeStruct((b, qd, s), BF16),
        scratch_shapes=[
            pltpu.VMEM((n_slc, HEADS_PER_GROUP * Q_BLOCK), BF16),
            pltpu.VMEM((n_slc, Q_BLOCK), F32),
            pltpu.VMEM((n_slc, Q_BLOCK), F32),
        ],
        compiler_params=_cparams("arbitrary", "arbitrary", "arbitrary"),
        name="nsa_attention",
    )(qc_t, qr_t, gates_t, k_cmp, v_cmp_t, k_slc, v_slc_t, k_win, v_win_t, map_t)


def _selection_map_t(n_cmp_pad, n_slc):
    c0 = jnp.arange(n_cmp_pad) * CMP_STRIDE
    j0 = jnp.arange(n_slc) * SLC_BLOCK
    ov = jnp.minimum(c0[None, :] + CMP_BLOCK, j0[:, None] + SLC_BLOCK) - jnp.maximum(c0[None, :], j0[:, None])
    return (jnp.clip(ov, 0).astype(F32) / CMP_BLOCK).astype(BF16)


def _out_proj_kernel(x_ref, ot_ref, w_ref, o_ref):
    o_ref[0] = x_ref[0] + _dot_tn(ot_ref[0], w_ref[...])


def _out_proj(x, o_t, w_out, tm=512):
    b, s, d = x.shape
    qd = o_t.shape[1]
    return pl.pallas_call(
        _out_proj_kernel,
        grid=(b, s // tm),
        in_specs=[
            pl.BlockSpec((1, tm, d), lambda i, j: (i, j, 0)),
            pl.BlockSpec((1, qd, tm), lambda i, j: (i, 0, j)),
            pl.BlockSpec((qd, d), lambda i, j: (0, 0)),
        ],
        out_specs=pl.BlockSpec((1, tm, d), lambda i, j: (i, j, 0)),
        out_shape=jax.ShapeDtypeStruct((b, s, d), F32),
        compiler_params=_cparams("arbitrary", "arbitrary"),
        name="nsa_out_proj",
    )(x, o_t, w_out.astype(BF16))


def _nsa_layer(x, positions, g_kv, g_q, w_kv, cmp_pos_k, cmp_w1_k, cmp_w2_k, cmp_pos_v, cmp_w1_v, cmp_w2_v,
               w_in, w_out):
    b, s, d = x.shape
    g = N_KV_GROUPS
    kcv, k_slc, k_win, v_slc_t, v_win_t, qc_t, qr_t, gates_t = _projections(x, positions, g_kv, g_q, w_kv, w_in)
    n_half = s // CMP_STRIDE
    halves = kcv.reshape(b, n_half, CMP_STRIDE, 2, g, HEAD_DIM).transpose(0, 3, 4, 1, 2, 5)
    halves = halves.reshape(b, 2, g, n_half, CMP_STRIDE * HEAD_DIM)
    k_cmp = _compress(halves[:, 0], cmp_pos_k, cmp_w1_k, cmp_w2_k, transposed_out=False)
    v_cmp_t = _compress(halves[:, 1], cmp_pos_v, cmp_w1_v, cmp_w2_v, transposed_out=True)
    map_t = _selection_map_t(n_half, s // SLC_BLOCK)
    o_t = _nsa_attention(qc_t, qr_t, gates_t, k_cmp, v_cmp_t, k_slc, v_slc_t, k_win, v_win_t, map_t)
    return _out_proj(x, o_t, w_out)


def kernel(x, positions, norm_mix, norm_ffn, conv_w_in, conv_w, conv_w_out, norm_kv, w_kv, cmp_pos_k, cmp_w1_k, cmp_w2_k, cmp_pos_v, cmp_w1_v, cmp_w2_v, attn_w_in, attn_w_out, router_group_w, router_group_b, router_expert_w, router_expert_b, expert_w_gate, expert_w_up, expert_w_down, norm_final):
    b, s, d = x.shape
    assert s % KEY_TILE == 0 and (s // SLC_BLOCK) % BIAS_ROWS == 0 and WINDOW % Q_BLOCK == 0

    def moe(xs, l, final_norm):
        return _hier_moe(xs.reshape(b * s, d), norm_ffn[l], router_group_w[l], router_group_b[l],
                         router_expert_w[l], router_expert_b[l], expert_w_gate[l], expert_w_up[l],
                         expert_w_down[l], norm_final, final_norm).reshape(b, s, d)

    x = _mixer(x, norm_mix[0], conv_w_in[0], conv_w[0], conv_w_out[0])
    x = moe(x, 0, False)
    x = _nsa_layer(x, positions, norm_kv, norm_mix[1], w_kv, cmp_pos_k, cmp_w1_k, cmp_w2_k,
                   cmp_pos_v, cmp_w1_v, cmp_w2_v, attn_w_in[0], attn_w_out[0])
    return moe(x, 1, True)
```

```python
import functools

import jax
import jax.numpy as jnp
from jax import lax
from jax.experimental import pallas as pl
from jax.experimental.pallas import tpu as pltpu

F32 = jnp.float32
BF16 = jnp.bfloat16
I32 = jnp.int32

RMS_EPS = 1e-6
ROPE_THETA = 10000.0
HEAD_DIM = 64
HALF = HEAD_DIM // 2
N_HEADS = 16
N_KV_GROUPS = 4
HEADS_PER_GROUP = N_HEADS // N_KV_GROUPS
N_BRANCH = 3
CMP_BLOCK = 32
CMP_STRIDE = 16
SLC_BLOCK = 64
N_SELECT = 16
WINDOW = 512
Q_BLOCK = 128
FORCE_BONUS = 1000.0
N_GROUPS = 4
EXPERTS_PER_GROUP = 8
N_EXPERTS = N_GROUPS * EXPERTS_PER_GROUP
CHUNK_ROWS = 256
NEG = -1e30
LANES = 128
KEY_TILE = 512
BIAS_ROWS = 16
MAX_ROWS = 64
CMP_CHUNK = 256
N_FORCED = 3
PICKED = -2.0
V_ROWS = 80
LOG2E = 1.4426950408889634
VMEM_LIMIT = 56 * 1024 * 1024


def _cparams(*sem):
    return pltpu.CompilerParams(dimension_semantics=sem, vmem_limit_bytes=VMEM_LIMIT)


def _rms_scale(x):
    return x * lax.rsqrt(jnp.mean(x * x, axis=-1, keepdims=True) + RMS_EPS)


def _dot(a, b):
    return jnp.dot(a, b, preferred_element_type=F32)


def _dot_nt(a, b):
    return lax.dot_general(a, b, (((1,), (1,)), ((), ())), preferred_element_type=F32)


def _dot_tn(a, b):
    return lax.dot_general(a, b, (((0,), (0,)), ((), ())), preferred_element_type=F32)


def _ones_row_block(shape):
    row = lax.broadcasted_iota(I32, shape, len(shape) - 2)
    return jnp.where(row == 0, 1.0, 0.0).astype(F32)


def _mixer_kernel(x_ref, g_ref, win_ref, cw_ref, wout_ref, o_ref, tail_ref):
    d = x_ref.shape[-1]
    tm = x_ref.shape[1]

    @pl.when(pl.program_id(1) == 0)
    def _():
        tail_ref[...] = jnp.zeros_like(tail_ref)

    x = x_ref[0]
    h = (_rms_scale(x) * g_ref[...]).astype(BF16)
    proj = _dot(h, win_ref[...])
    b_gate = proj[:, :d]
    u = proj[:, d:2 * d] * proj[:, 2 * d:]
    row = lax.broadcasted_iota(I32, (tm, 1), 0)
    prev1 = tail_ref[7:8, :]
    prev2 = tail_ref[6:7, :]
    u1 = jnp.where(row >= 1, pltpu.roll(u, 1, 0), prev1)
    u2 = jnp.where(row >= 2, pltpu.roll(u, 2, 0), jnp.where(row == 1, prev1, prev2))
    y = cw_ref[2:3, :] * u + cw_ref[1:2, :] * u1 + cw_ref[0:1, :] * u2
    tail_ref[...] = u[tm - 8:, :]
    z = (b_gate * y).astype(BF16)
    o_ref[0] = x + _dot(z, wout_ref[...])


def _mixer(x, g, w_in, conv_w, w_out, tm=512):
    b, s, d = x.shape
    return pl.pallas_call(
        _mixer_kernel,
        grid=(b, s // tm),
        in_specs=[
            pl.BlockSpec((1, tm, d), lambda i, j: (i, j, 0)),
            pl.BlockSpec((1, d), lambda i, j: (0, 0)),
            pl.BlockSpec((d, 3 * d), lambda i, j: (0, 0)),
            pl.BlockSpec((3, d), lambda i, j: (0, 0)),
            pl.BlockSpec((d, d), lambda i, j: (0, 0)),
        ],
        out_specs=pl.BlockSpec((1, tm, d), lambda i, j: (i, j, 0)),
        out_shape=jax.ShapeDtypeStruct((b, s, d), F32),
        scratch_shapes=[pltpu.VMEM((8, d), F32)],
        compiler_params=_cparams("arbitrary", "arbitrary"),
        name="conv_mixer",
    )(x, g.reshape(1, d), w_in.astype(BF16), conv_w, w_out.astype(BF16))


def _router_kernel(x_ref, g_ref, wt_ref, b_ref, hb_ref, eid_ref, wts_ref):
    tm = x_ref.shape[0]
    h = _rms_scale(x_ref[...]) * g_ref[...]
    hb_ref[...] = h.astype(BF16)
    lt = lax.dot_general(wt_ref[...], h, (((1,), (1,)), ((), ())),
                         precision=lax.Precision.HIGHEST, preferred_element_type=F32) + b_ref[...]
    r8 = lax.broadcasted_iota(I32, (8, 1), 0)
    gl = jnp.where(r8 < N_GROUPS, lt[0:8], -jnp.inf)
    gmax = jnp.max(gl, axis=0, keepdims=True)
    gtop = jnp.min(jnp.where(gl == gmax, r8, 8), axis=0, keepdims=True)
    p_g = 1.0 / jnp.sum(jnp.exp(gl - gmax), axis=0, keepdims=True)
    chosen = lt[8:16]
    for k in range(1, N_GROUPS):
        chosen = jnp.where(gtop == k, lt[8 + 8 * k:16 + 8 * k], chosen)
    v0 = jnp.max(chosen, axis=0, keepdims=True)
    i0 = jnp.min(jnp.where(chosen == v0, r8, 8), axis=0, keepdims=True)
    rest = jnp.where(r8 == i0, -jnp.inf, chosen)
    v1 = jnp.max(rest, axis=0, keepdims=True)
    i1 = jnp.min(jnp.where(rest == v1, r8, 8), axis=0, keepdims=True)
    e = jnp.exp(v1 - v0)
    w0 = p_g / (1.0 + e)
    w1 = p_g * e / (1.0 + e)
    base = gtop * EXPERTS_PER_GROUP
    eid_ref[...] = jnp.where(r8 == 0, base + i0, jnp.where(r8 == 1, base + i1, 0))
    wts_ref[...] = jnp.where(r8 == 0, w0, jnp.where(r8 == 1, w1, 0.0))


def _router(xf, g, w_rg, b_rg, w_re, b_re, tm=512):
    n, d = xf.shape
    rows = 8 + N_EXPERTS
    wt = jnp.zeros((rows, d), F32).at[:N_GROUPS].set(w_rg.T).at[8:].set(w_re.T)
    bt = jnp.zeros((rows, 1), F32).at[:N_GROUPS, 0].set(b_rg).at[8:, 0].set(b_re)
    return pl.pallas_call(
        _router_kernel,
        grid=(n // tm,),
        in_specs=[
            pl.BlockSpec((tm, d), lambda i: (i, 0)),
            pl.BlockSpec((1, d), lambda i: (0, 0)),
            pl.BlockSpec((rows, d), lambda i: (0, 0)),
            pl.BlockSpec((rows, 1), lambda i: (0, 0)),
        ],
        out_specs=[
            pl.BlockSpec((tm, d), lambda i: (i, 0)),
            pl.BlockSpec((8, tm), lambda i: (0, i)),
            pl.BlockSpec((8, tm), lambda i: (0, i)),
        ],
        out_shape=[
            jax.ShapeDtypeStruct((n, d), BF16),
            jax.ShapeDtypeStruct((8, n), I32),
            jax.ShapeDtypeStruct((8, n), F32),
        ],
        compiler_params=_cparams("arbitrary"),
        name="moe_router",
    )(xf, g.reshape(1, d), wt, bt)


def _dispatch_plan(eid, n_tok):
    a_exp = eid.reshape(-1)
    n_asg = a_exp.shape[0]
    a_tok = jnp.arange(n_asg, dtype=I32) % n_tok
    onehot = (a_exp[:, None] == jnp.arange(N_EXPERTS, dtype=I32)[None, :]).astype(I32)
    csum = jnp.cumsum(onehot, axis=0)
    rank = jnp.sum(csum * onehot, axis=1) - 1
    counts = csum[-1]
    padded = (counts + CHUNK_ROWS - 1) // CHUNK_ROWS * CHUNK_ROWS
    pend = jnp.cumsum(padded)
    pstart = pend - padded
    dest = (jnp.sum(pstart[None, :] * onehot, axis=1) + rank).astype(I32)
    n_chunks = -(-n_asg // CHUNK_ROWS) + N_EXPERTS
    row_tok = jnp.zeros((n_chunks * CHUNK_ROWS,), I32).at[dest].set(a_tok)
    chunk_exp = jnp.clip(jnp.searchsorted(pend, jnp.arange(n_chunks, dtype=I32) * CHUNK_ROWS, side='right'),
                         0, N_EXPERTS - 1).astype(I32)
    n_used = (pend[-1] // CHUNK_ROWS).astype(I32).reshape(1)
    return row_tok, dest.reshape(2, n_tok), chunk_exp, n_used, n_chunks


def _expert_kernel(ce_ref, nu_ref, xs_ref, wg_ref, wu_ref, wd_ref, o_ref):
    c = pl.program_id(0)

    @pl.when(c < nu_ref[0])
    def _():
        xc = xs_ref[...]
        gate = _dot(xc, wg_ref[0])
        up = _dot(xc, wu_ref[0])
        act = (gate * jax.nn.sigmoid(gate) * up).astype(BF16)
        o_ref[...] = _dot(act, wd_ref[0])

    @pl.when(c >= nu_ref[0])
    def _():
        o_ref[...] = jnp.zeros_like(o_ref)


def _experts(xs, chunk_exp, n_used, w_gate, w_up, w_down, n_chunks):
    d = xs.shape[1]
    de = w_gate.shape[-1]
    grid_spec = pltpu.PrefetchScalarGridSpec(
        num_scalar_prefetch=2,
        grid=(n_chunks,),
        in_specs=[
            pl.BlockSpec((CHUNK_ROWS, d), lambda c, ce, nu: (c, 0)),
            pl.BlockSpec((1, d, de), lambda c, ce, nu: (ce[c], 0, 0)),
            pl.BlockSpec((1, d, de), lambda c, ce, nu: (ce[c], 0, 0)),
            pl.BlockSpec((1, de, d), lambda c, ce, nu: (ce[c], 0, 0)),
        ],
        out_specs=pl.BlockSpec((CHUNK_ROWS, d), lambda c, ce, nu: (c, 0)),
    )
    return pl.pallas_call(
        _expert_kernel,
        grid_spec=grid_spec,
        out_shape=jax.ShapeDtypeStruct((n_chunks * CHUNK_ROWS, d), F32),
        compiler_params=_cparams("arbitrary"),
        name="moe_experts",
    )(chunk_exp, n_used, xs, w_gate, w_up, w_down)


def _combine_kernel(x_ref, y0_ref, y1_ref, w_ref, g_ref, o_ref, *, final_norm):
    w = w_ref[...]
    out = x_ref[...] + w[:, 0:1] * y0_ref[...] + w[:, 1:2] * y1_ref[...]
    if final_norm:
        out = _rms_scale(out) * g_ref[...]
    o_ref[...] = out


def _combine(xf, y0, y1, wcol, g, final_norm, tm=512):
    n, d = xf.shape
    row = pl.BlockSpec((tm, d), lambda i: (i, 0))
    return pl.pallas_call(
        functools.partial(_combine_kernel, final_norm=final_norm),
        grid=(n // tm,),
        in_specs=[row, row, row, pl.BlockSpec((tm, 8), lambda i: (i, 0)), pl.BlockSpec((1, d), lambda i: (0, 0))],
        out_specs=row,
        out_shape=jax.ShapeDtypeStruct((n, d), F32),
        compiler_params=_cparams("arbitrary"),
        name="moe_combine",
    )(xf, y0, y1, wcol, g.reshape(1, d))


def _hier_moe(xf, g_ffn, w_rg, b_rg, w_re, b_re, w_gate, w_up, w_down, g_final, final_norm):
    n = xf.shape[0]
    hb, eid, wts = _router(xf, g_ffn, w_rg, b_rg, w_re, b_re)
    row_tok, dest, chunk_exp, n_used, n_chunks = _dispatch_plan(eid[:2], n)
    xs = jnp.take(hb, row_tok, axis=0)
    ys = _experts(xs, chunk_exp, n_used, w_gate.astype(BF16), w_up.astype(BF16), w_down.astype(BF16), n_chunks)
    y0 = jnp.take(ys, dest[0], axis=0)
    y1 = jnp.take(ys, dest[1], axis=0)
    return _combine(xf, y0, y1, wts.T, g_final, final_norm)


def _proj_kernel(x_ref, prow_ref, pcol_ref, gkv_ref, gq_ref, wnat_ref, wvt_ref, wqt_ref,
                 kcv_ref, ks_ref, kw_ref, vst_ref, vwt_ref, qc_ref, qr_ref, gt_ref):
    tm = x_ref.shape[1]
    gd = N_KV_GROUPS * HEAD_DIM
    qd = N_HEADS * HEAD_DIM
    y = _rms_scale(x_ref[0])
    hkv = (y * gkv_ref[...]).astype(BF16)
    hq = (y * gq_ref[...]).astype(BF16)

    nat = _dot(hkv, wnat_ref[...])
    kcv_ref[0] = nat[:, :2 * gd]
    lane = lax.broadcasted_iota(I32, (1, LANES), 1)
    inv = jnp.exp((lane % HALF).astype(F32) * (-jnp.log(ROPE_THETA) / HALF))
    ang = pcol_ref[0].astype(F32) * inv
    ctab = jnp.where(lane < HEAD_DIM, jnp.cos(ang), 0.0)
    stab = jnp.where(lane < HEAD_DIM, jnp.sin(ang), 0.0)
    key = pl.program_id(1) * tm + lax.broadcasted_iota(I32, (tm, 1), 0)
    blk = (key // SLC_BLOCK) % BIAS_ROWS
    onehot = jnp.where(lane == HEAD_DIM + blk, 1.0, 0.0)
    for g in range(N_KV_GROUPS):
        for slot, ref in ((0, ks_ref), (1, kw_ref)):
            lo = 2 * gd + (slot * N_KV_GROUPS + g) * LANES
            t = nat[:, lo:lo + LANES]
            roped = t * ctab + pltpu.roll(t, HEAD_DIM, 1) * stab
            if slot == 0:
                roped = roped + onehot
            ref[0, g] = roped.astype(BF16)

    vt = _dot_nt(wvt_ref[...], hkv)
    ones_rows = _ones_row_block((N_KV_GROUPS, V_ROWS - HEAD_DIM, tm))
    vs3 = jnp.concatenate([vt[:gd].reshape(N_KV_GROUPS, HEAD_DIM, tm), ones_rows], axis=1).astype(BF16)
    vw3 = jnp.concatenate([vt[gd:].reshape(N_KV_GROUPS, HEAD_DIM, tm), ones_rows], axis=1).astype(BF16)
    vst_ref[0, :, 0] = vs3
    for w in range(tm // Q_BLOCK):
        vwt_ref[0, :, w] = vw3[:, :, w * Q_BLOCK:(w + 1) * Q_BLOCK]

    qt = _dot_nt(wqt_ref[...], hq)
    scale = HEAD_DIM ** -0.5 * LOG2E
    q3 = qt[:qd].reshape(N_HEADS, HEAD_DIM, tm)
    qc_ref[0] = (qt[:qd] * scale).astype(BF16)
    frq = lax.broadcasted_iota(I32, (HALF, 1), 0).astype(F32)
    ang_t = jnp.exp(frq * (-jnp.log(ROPE_THETA) / HALF)) * prow_ref[0].astype(F32)
    cos_t = jnp.cos(ang_t)[None]
    sin_t = jnp.sin(ang_t)[None]
    t1 = q3[:, :HALF]
    t2 = q3[:, HALF:]
    qr = jnp.concatenate([t1 * cos_t - t2 * sin_t, t2 * cos_t + t1 * sin_t], axis=1)
    qr_ref[0] = (qr * scale).reshape(qd, tm).astype(BF16)
    gt_ref[0] = jax.nn.sigmoid(qt[qd:])


def _rot_half_cols(w):
    return jnp.concatenate([-w[:, HALF:], w[:, :HALF]], axis=1)


def _projections(x, positions, g_kv, g_q, w_kv, w_q, tm=KEY_TILE):
    b, s, d = x.shape
    gd = N_KV_GROUPS * HEAD_DIM
    qd = N_HEADS * HEAD_DIM
    slot = lambda i: w_kv[:, i * gd:(i + 1) * gd]
    aug = []
    for sl in (2, 4):
        for g in range(N_KV_GROUPS):
            wk = slot(sl)[:, g * HEAD_DIM:(g + 1) * HEAD_DIM]
            aug += [wk, _rot_half_cols(wk)]
    w_nat = jnp.concatenate([slot(0), slot(1)] + aug, axis=1).astype(BF16)
    w_vt = jnp.concatenate([slot(3), slot(5)], axis=1).T.astype(BF16)
    gate_cols = w_q[:, qd:].reshape(d, N_KV_GROUPS, HEADS_PER_GROUP * N_BRANCH)
    gate_cols = jnp.pad(gate_cols, ((0, 0), (0, 0), (0, 16 - HEADS_PER_GROUP * N_BRANCH))).reshape(d, 64)
    w_qt = jnp.concatenate([w_q[:, :qd], gate_cols], axis=1).T.astype(BF16)
    nt = s // tm
    n_nat = w_nat.shape[1]
    const = lambda shape: pl.BlockSpec(shape, lambda i, j: (0,) * len(shape))
    return pl.pallas_call(
        _proj_kernel,
        grid=(b, nt),
        in_specs=[
            pl.BlockSpec((1, tm, d), lambda i, j: (i, j, 0)),
            pl.BlockSpec((1, 1, tm), lambda i, j: (i, 0, j)),
            pl.BlockSpec((1, tm, 1), lambda i, j: (i, j, 0)),
            const((1, d)), const((1, d)), const((d, n_nat)), const((2 * gd, d)), const((qd + 64, d)),
        ],
        out_specs=[
            pl.BlockSpec((1, tm, 2 * gd), lambda i, j: (i, j, 0)),
            pl.BlockSpec((1, N_KV_GROUPS, tm, LANES), lambda i, j: (i, 0, j, 0)),
            pl.BlockSpec((1, N_KV_GROUPS, tm, LANES), lambda i, j: (i, 0, j, 0)),
            pl.BlockSpec((1, N_KV_GROUPS, 1, V_ROWS, tm), lambda i, j: (i, 0, j, 0, 0)),
            pl.BlockSpec((1, N_KV_GROUPS, tm // Q_BLOCK, V_ROWS, Q_BLOCK), lambda i, j: (i, 0, j, 0, 0)),
            pl.BlockSpec((1, qd, tm), lambda i, j: (i, 0, j)),
            pl.BlockSpec((1, qd, tm), lambda i, j: (i, 0, j)),
            pl.BlockSpec((1, 64, tm), lambda i, j: (i, 0, j)),
        ],
        out_shape=[
            jax.ShapeDtypeStruct((b, s, 2 * gd), F32),
            jax.ShapeDtypeStruct((b, N_KV_GROUPS, s, LANES), BF16),
            jax.ShapeDtypeStruct((b, N_KV_GROUPS, s, LANES), BF16),
            jax.ShapeDtypeStruct((b, N_KV_GROUPS, nt, V_ROWS, tm), BF16),
            jax.ShapeDtypeStruct((b, N_KV_GROUPS, s // Q_BLOCK, V_ROWS, Q_BLOCK), BF16),
            jax.ShapeDtypeStruct((b, qd, s), BF16),
            jax.ShapeDtypeStruct((b, qd, s), BF16),
            jax.ShapeDtypeStruct((b, 64, s), F32),
        ],
        compiler_params=_cparams("arbitrary", "arbitrary"),
        name="nsa_projections",
    )(x, positions.reshape(b, 1, s), positions.reshape(b, s, 1), g_kv.reshape(1, d), g_q.reshape(1, d),
      w_nat, w_vt, w_qt)


def _compress_kernel(a_ref, pos_ref, w1_ref, w2_ref, o_ref, *, transposed_out):
    half = w1_ref.shape[0] // 2
    a = a_ref[0, 0]
    top = (a + pos_ref[0:1, :]).astype(BF16)
    bot = (a + pos_ref[1:2, :]).astype(BF16)
    p = _dot(top, w1_ref[:half, :])
    q = _dot(bot, w1_ref[half:, :])
    n = a.shape[0]
    hidden = p + pltpu.roll(q, n - 1, 0)
    act = jax.nn.gelu(hidden, approximate=True).astype(BF16)
    if transposed_out:
        v_t = _dot_nt(w2_ref[...], act)
        o_ref[0, 0] = jnp.concatenate([v_t, _ones_row_block((V_ROWS - HEAD_DIM, n))], axis=0).astype(BF16)
    else:
        o_ref[0, 0] = _dot(act, w2_ref[...]).astype(BF16)


def _compress(a, pos, w1, w2, transposed_out):
    b, g, n, width = a.shape
    hidden = w1.shape[1]
    pos2 = pos.reshape(2, width)
    if transposed_out:
        w2p = w2.T.astype(BF16)
        out_block, out_shape = (1, 1, V_ROWS, n), (b, g, V_ROWS, n)
    else:
        w2p = jnp.pad(w2, ((0, 0), (0, LANES - HEAD_DIM))).astype(BF16)
        out_block, out_shape = (1, 1, n, LANES), (b, g, n, LANES)
    return pl.pallas_call(
        functools.partial(_compress_kernel, transposed_out=transposed_out),
        grid=(b, g),
        in_specs=[
            pl.BlockSpec((1, 1, n, width), lambda i, j: (i, j, 0, 0)),
            pl.BlockSpec((2, width), lambda i, j: (0, 0)),
            pl.BlockSpec((2 * width, hidden), lambda i, j: (0, 0)),
            pl.BlockSpec(w2p.shape, lambda i, j: (0, 0)),
        ],
        out_specs=pl.BlockSpec(out_block, lambda i, j: (i, j, 0, 0)),
        out_shape=jax.ShapeDtypeStruct(out_shape, BF16),
        compiler_params=_cparams("arbitrary", "arbitrary"),
        name="nsa_compress_v" if transposed_out else "nsa_compress_k",
    )(a, pos2, w1.astype(BF16), w2p)


def _heads_on_lanes(ref):
    return jnp.concatenate([ref[0, h * HEAD_DIM:(h + 1) * HEAD_DIM, :] for h in range(HEADS_PER_GROUP)], axis=1)


def _nsa_kernel(qc_ref, qr_ref, g_ref, kc_ref, lc_ref, ks_ref, vst_ref, kw_ref, vwt_ref,
                o_ref, bias_ref, score_ref, sc_ref, ec_ref, sa_ref, sb_ref, pa_ref, pb_ref):
    i = pl.program_id(2)
    qb = Q_BLOCK
    width = HEADS_PER_GROUP * qb
    n_cmp = kc_ref.shape[2]
    n_slc = score_ref.shape[0]
    t_q = i * qb + lax.broadcasted_iota(I32, (1, qb), 1)
    t_row = jnp.concatenate([t_q] * HEADS_PER_GROUP, axis=1)
    zeros_pad = jnp.zeros((LANES - HEAD_DIM, width), BF16)

    cc = min(CMP_CHUNK, n_cmp)
    n_cc = jnp.minimum((i * (qb // CMP_STRIDE) + qb // CMP_STRIDE + cc - 1) // cc, n_cmp // cc)
    qc_t = jnp.concatenate([_heads_on_lanes(qc_ref), zeros_pad], axis=0)

    @pl.when(i == 0)
    def _():
        ec_ref[...] = jnp.zeros_like(ec_ref)

    def cmp_scores(ch, m):
        r0 = pl.multiple_of(ch * cc, cc)
        s = _dot(kc_ref[0, 0, pl.ds(r0, cc), :], qc_t)
        cmp_end = (r0 + lax.broadcasted_iota(I32, (cc, 1), 0)) * CMP_STRIDE + (CMP_BLOCK - 1)
        s = jnp.where(cmp_end <= t_row, s, NEG)
        sc_ref[pl.ds(r0, cc), :] = s
        return jnp.maximum(m, jnp.max(s, axis=0, keepdims=True))

    m_c = lax.fori_loop(0, n_cc, cmp_scores, jnp.full((1, width), NEG, F32))
    m_c = jnp.maximum(m_c, NEG * 1e-10)

    def cmp_exp(ch, carry):
        r0 = pl.multiple_of(ch * cc, cc)
        ec_ref[pl.ds(r0, cc), :] = jnp.exp2(sc_ref[pl.ds(r0, cc), :] - m_c).astype(BF16)
        return carry

    lax.fori_loop(0, n_cc, cmp_exp, 0)
    res = _dot(lc_ref[0, 0], ec_ref[...])
    rinv = 1.0 / jnp.maximum(res[HEAD_DIM:HEAD_DIM + 1], 1e-30)
    o_cmp = res[:HEAD_DIM] * rinv
    u = res[V_ROWS:] * rinv
    imp = u[:, 0:qb]
    for h in range(1, HEADS_PER_GROUP):
        imp = imp + u[:, h * qb:(h + 1) * qb]

    j_idx = lax.broadcasted_iota(I32, (n_slc, 1), 0)
    cur = t_q // SLC_BLOCK
    valid = j_idx * SLC_BLOCK <= t_q
    forced = (j_idx == 0) | (j_idx == cur) | (j_idx == cur - 1)
    score_ref[...] = jnp.where(valid & jnp.logical_not(forced), imp, -1.0)

    def pick(_, carry):
        sc = score_ref[...]
        best = jnp.max(sc, axis=0, keepdims=True)
        idx = jnp.min(jnp.where(sc == best, j_idx, n_slc), axis=0, keepdims=True)
        score_ref[...] = jnp.where(j_idx == idx, PICKED, sc)
        return carry

    lax.fori_loop(0, N_SELECT - N_FORCED, pick, 0)
    bias = jnp.where(forced | (score_ref[...] == PICKED), 0.0, NEG).astype(BF16)
    bias_ref[...] = jnp.concatenate([bias] * HEADS_PER_GROUP, axis=1)

    qr_heads = _heads_on_lanes(qr_ref)
    zeros_sel = jnp.zeros((LANES - HEAD_DIM - BIAS_ROWS, width), BF16)
    tk = KEY_TILE
    blocks_per_tile = tk // SLC_BLOCK

    def scores(kt, s_ref, masked):
        b0 = pl.multiple_of((kt * blocks_per_tile) // BIAS_ROWS * BIAS_ROWS, BIAS_ROWS)
        q_aug = jnp.concatenate([qr_heads, bias_ref[pl.ds(b0, BIAS_ROWS), :], zeros_sel], axis=0)
        k0 = pl.multiple_of(kt * tk, tk)
        s = _dot(ks_ref[0, 0, pl.ds(k0, tk), :], q_aug)
        if masked:
            s = jnp.where(k0 + lax.broadcasted_iota(I32, (tk, 1), 0) <= t_row, s, NEG)
        s_ref[...] = s
        part = jnp.max(s.reshape(tk // MAX_ROWS, MAX_ROWS, width), axis=0)
        return jnp.max(part, axis=0, keepdims=True)

    def values(kt, p_ref):
        return _dot(vst_ref[0, 0, jnp.maximum(kt, 0)], p_ref[...])

    def trip(kk, carry, next_scores):
        m, acc, beta, cmax_a, cmax_b = carry
        kt = 2 * kk
        live = jnp.where(kk > 0, 1.0, 0.0)
        acc = acc + live * (beta * values(kt - 2, pa_ref) + values(kt - 1, pb_ref))
        m1 = jnp.maximum(m, cmax_a)
        pa_ref[...] = jnp.exp2((sa_ref[...] - m1).astype(BF16))
        m2 = jnp.maximum(m1, cmax_b)
        pb_ref[...] = jnp.exp2((sb_ref[...] - m2).astype(BF16))
        beta = jnp.exp2(m1 - m2)
        if next_scores is not None:
            cmax_a = scores(kt + 2, sa_ref, next_scores)
            cmax_b = scores(kt + 3, sb_ref, next_scores)
        return m2, (jnp.exp2(m - m1) * beta) * acc, beta, cmax_a, cmax_b

    @pl.when((pl.program_id(0) == 0) & (pl.program_id(1) == 0) & (i == 0))
    def _():
        pa_ref[...] = jnp.zeros_like(pa_ref)
        pb_ref[...] = jnp.zeros_like(pb_ref)

    n_full = (i * qb) // tk
    n_unmasked = jnp.maximum(n_full // 2 - 1, 0)
    carry = (jnp.full((1, width), NEG, F32), jnp.zeros((V_ROWS, width), F32), jnp.ones((1, width), F32),
             scores(0, sa_ref, True), scores(1, sb_ref, True))
    carry = lax.fori_loop(0, n_unmasked, functools.partial(trip, next_scores=False), carry)
    carry = lax.fori_loop(n_unmasked, n_full // 2, functools.partial(trip, next_scores=True), carry)
    last = n_full // 2
    _, acc_s, beta, _, _ = trip(last, carry, None)
    acc_s = acc_s + beta * values(2 * last, pa_ref) + values(2 * last + 1, pb_ref)
    o_slc = acc_s[:HEAD_DIM] / acc_s[HEAD_DIM:HEAD_DIM + 1]

    qr_t = jnp.concatenate([qr_heads, zeros_pad], axis=0)
    n_win = WINDOW // qb + 1
    s_w = []
    for r in range(n_win):
        j = i - (n_win - 1) + r
        k0 = pl.multiple_of(jnp.maximum(j, 0) * qb, qb)
        s = _dot(kw_ref[0, 0, pl.ds(k0, qb), :], qr_t)
        if r in (0, n_win - 1):
            kpos = j * qb + lax.broadcasted_iota(I32, (qb, 1), 0)
            diff = t_row - kpos
            s = jnp.where((diff >= 0) & (diff < WINDOW) & (kpos >= 0), s, NEG)
        else:
            s = s + jnp.where(j >= 0, 0.0, NEG)
        s_w.append(s)
    m_w = s_w[0].max(axis=0, keepdims=True)
    for r in range(1, n_win):
        m_w = jnp.maximum(m_w, s_w[r].max(axis=0, keepdims=True))
    acc_w = jnp.zeros((V_ROWS, width), F32)
    for r in range(n_win):
        jc = jnp.maximum(i - (n_win - 1) + r, 0)
        acc_w = acc_w + _dot(vwt_ref[0, 0, jc], jnp.exp2((s_w[r] - m_w).astype(BF16)))
    o_win = acc_w[:HEAD_DIM] / acc_w[HEAD_DIM:HEAD_DIM + 1]

    gts = g_ref[0]
    for h in range(HEADS_PER_GROUP):
        sl = slice(h * qb, (h + 1) * qb)
        r0 = h * N_BRANCH
        o = (gts[r0:r0 + 1] * o_cmp[:, sl] + gts[r0 + 1:r0 + 2] * o_slc[:, sl] + gts[r0 + 2:r0 + 3] * o_win[:, sl])
        o_ref[0, h * HEAD_DIM:(h + 1) * HEAD_DIM, :] = o.astype(BF16)


def _nsa_attention(qc_t, qr_t, gates_t, k_cmp, lhs_cmp, k_slc, v_slc_t, k_win, v_win_t):
    b, qd, s = qc_t.shape
    g = N_KV_GROUPS
    n_cmp = k_cmp.shape[2]
    n_slc = lhs_cmp.shape[2] - V_ROWS
    nt, tk = v_slc_t.shape[2], v_slc_t.shape[4]
    rows = HEADS_PER_GROUP * HEAD_DIM
    width = HEADS_PER_GROUP * Q_BLOCK
    qspec = pl.BlockSpec((1, rows, Q_BLOCK), lambda bi, gi, i: (bi, gi, i))
    per_bg = lambda shape: pl.BlockSpec((1, 1) + shape, lambda bi, gi, i: (bi, gi) + (0,) * len(shape))
    return pl.pallas_call(
        _nsa_kernel,
        grid=(b, g, s // Q_BLOCK),
        in_specs=[
            qspec, qspec,
            pl.BlockSpec((1, 16, Q_BLOCK), lambda bi, gi, i: (bi, gi, i)),
            per_bg((n_cmp, LANES)), per_bg((V_ROWS + n_slc, n_cmp)),
            per_bg((s, LANES)), per_bg((nt, V_ROWS, tk)),
            per_bg((s, LANES)), per_bg((s // Q_BLOCK, V_ROWS, Q_BLOCK)),
        ],
        out_specs=qspec,
        out_shape=jax.ShapeDtypeStruct((b, qd, s), BF16),
        scratch_shapes=[
            pltpu.VMEM((n_slc, width), BF16),
            pltpu.VMEM((n_slc, Q_BLOCK), F32),
            pltpu.VMEM((n_cmp, width), F32),
            pltpu.VMEM((n_cmp, width), BF16),
            pltpu.VMEM((tk, width), F32), pltpu.VMEM((tk, width), F32),
            pltpu.VMEM((tk, width), BF16), pltpu.VMEM((tk, width), BF16),
        ],
        compiler_params=_cparams("arbitrary", "arbitrary", "arbitrary"),
        name="nsa_attention",
    )(qc_t, qr_t, gates_t, k_cmp, lhs_cmp, k_slc, v_slc_t, k_win, v_win_t)


def _selection_map_t(n_cmp_pad, n_slc):
    c0 = jnp.arange(n_cmp_pad) * CMP_STRIDE
    j0 = jnp.arange(n_slc) * SLC_BLOCK
    ov = jnp.minimum(c0[None, :] + CMP_BLOCK, j0[:, None] + SLC_BLOCK) - jnp.maximum(c0[None, :], j0[:, None])
    return (jnp.clip(ov, 0).astype(F32) / CMP_BLOCK).astype(BF16)


def _out_proj_kernel(x_ref, ot_ref, w_ref, o_ref):
    o_ref[0] = x_ref[0] + _dot_tn(ot_ref[0], w_ref[...])


def _out_proj(x, o_t, w_out, tm=512):
    b, s, d = x.shape
    qd = o_t.shape[1]
    return pl.pallas_call(
        _out_proj_kernel,
        grid=(b, s // tm),
        in_specs=[
            pl.BlockSpec((1, tm, d), lambda i, j: (i, j, 0)),
            pl.BlockSpec((1, qd, tm), lambda i, j: (i, 0, j)),
            pl.BlockSpec((qd, d), lambda i, j: (0, 0)),
        ],
        out_specs=pl.BlockSpec((1, tm, d), lambda i, j: (i, j, 0)),
        out_shape=jax.ShapeDtypeStruct((b, s, d), F32),
        compiler_params=_cparams("arbitrary", "arbitrary"),
        name="nsa_out_proj",
    )(x, o_t, w_out.astype(BF16))


def _nsa_layer(x, positions, g_kv, g_q, w_kv, cmp_pos_k, cmp_w1_k, cmp_w2_k, cmp_pos_v, cmp_w1_v, cmp_w2_v,
               w_in, w_out):
    b, s, d = x.shape
    g = N_KV_GROUPS
    kcv, k_slc, k_win, v_slc_t, v_win_t, qc_t, qr_t, gates_t = _projections(x, positions, g_kv, g_q, w_kv, w_in)
    n_half = s // CMP_STRIDE
    halves = kcv.reshape(b, n_half, CMP_STRIDE, 2, g, HEAD_DIM).transpose(0, 3, 4, 1, 2, 5)
    halves = halves.reshape(b, 2, g, n_half, CMP_STRIDE * HEAD_DIM)
    k_cmp = _compress(halves[:, 0], cmp_pos_k, cmp_w1_k, cmp_w2_k, transposed_out=False)
    v_cmp_t = _compress(halves[:, 1], cmp_pos_v, cmp_w1_v, cmp_w2_v, transposed_out=True)
    map_t = _selection_map_t(n_half, s // SLC_BLOCK)
    lhs_cmp = jnp.concatenate([v_cmp_t, jnp.broadcast_to(map_t, (b, g) + map_t.shape)], axis=2)
    o_t = _nsa_attention(qc_t, qr_t, gates_t, k_cmp, lhs_cmp, k_slc, v_slc_t, k_win, v_win_t)
    return _out_proj(x, o_t, w_out)


def kernel(x, positions, norm_mix, norm_ffn, conv_w_in, conv_w, conv_w_out, norm_kv, w_kv, cmp_pos_k, cmp_w1_k, cmp_w2_k, cmp_pos_v, cmp_w1_v, cmp_w2_v, attn_w_in, attn_w_out, router_group_w, router_group_b, router_expert_w, router_expert_b, expert_w_gate, expert_w_up, expert_w_down, norm_final):
    b, s, d = x.shape
    assert s % (2 * KEY_TILE) == 0 and (s // SLC_BLOCK) % BIAS_ROWS == 0 and WINDOW % Q_BLOCK == 0

    def moe(xs, l, final_norm):
        return _hier_moe(xs.reshape(b * s, d), norm_ffn[l], router_group_w[l], router_group_b[l],
                         router_expert_w[l], router_expert_b[l], expert_w_gate[l], expert_w_up[l],
                         expert_w_down[l], norm_final, final_norm).reshape(b, s, d)

    x = _mixer(x, norm_mix[0], conv_w_in[0], conv_w[0], conv_w_out[0])
    x = moe(x, 0, False)
    x = _nsa_layer(x, positions, norm_kv, norm_mix[1], w_kv, cmp_pos_k, cmp_w1_k, cmp_w2_k,
                   cmp_pos_v, cmp_w1_v, cmp_w2_v, attn_w_in[0], attn_w_out[0])
    return moe(x, 1, True)
```

```python
import functools

import jax
import jax.numpy as jnp
from jax import lax
from jax.experimental import pallas as pl
from jax.experimental.pallas import tpu as pltpu

F32 = jnp.float32
BF16 = jnp.bfloat16
I32 = jnp.int32

RMS_EPS = 1e-6
ROPE_THETA = 10000.0
HEAD_DIM = 64
HALF = HEAD_DIM // 2
N_HEADS = 16
N_KV_GROUPS = 4
HEADS_PER_GROUP = N_HEADS // N_KV_GROUPS
N_BRANCH = 3
CMP_BLOCK = 32
CMP_STRIDE = 16
SLC_BLOCK = 64
N_SELECT = 16
WINDOW = 512
Q_BLOCK = 128
FORCE_BONUS = 1000.0
N_GROUPS = 4
EXPERTS_PER_GROUP = 8
N_EXPERTS = N_GROUPS * EXPERTS_PER_GROUP
CHUNK_ROWS = 256
NEG = -1e30
LANES = 128
KEY_TILE = 512
BIAS_ROWS = 16
MAX_ROWS = 64
CMP_CHUNK = 256
N_FORCED = 3
PICK_ROWS = 64
PICKED = -2.0
V_ROWS = 80
LOG2E = 1.4426950408889634
VMEM_LIMIT = 56 * 1024 * 1024


def _cparams(*sem):
    return pltpu.CompilerParams(dimension_semantics=sem, vmem_limit_bytes=VMEM_LIMIT)


def _rms_scale(x):
    return x * lax.rsqrt(jnp.mean(x * x, axis=-1, keepdims=True) + RMS_EPS)


def _dot(a, b):
    return jnp.dot(a, b, preferred_element_type=F32)


def _dot_nt(a, b):
    return lax.dot_general(a, b, (((1,), (1,)), ((), ())), preferred_element_type=F32)


def _dot_tn(a, b):
    return lax.dot_general(a, b, (((0,), (0,)), ((), ())), preferred_element_type=F32)


def _ones_row_block(shape):
    row = lax.broadcasted_iota(I32, shape, len(shape) - 2)
    return jnp.where(row == 0, 1.0, 0.0).astype(F32)


def _mixer_kernel(x_ref, g_ref, win_ref, cw_ref, wout_ref, o_ref, tail_ref):
    d = x_ref.shape[-1]
    tm = x_ref.shape[1]

    @pl.when(pl.program_id(1) == 0)
    def _():
        tail_ref[...] = jnp.zeros_like(tail_ref)

    x = x_ref[0]
    h = (_rms_scale(x) * g_ref[...]).astype(BF16)
    proj = _dot(h, win_ref[...])
    b_gate = proj[:, :d]
    u = proj[:, d:2 * d] * proj[:, 2 * d:]
    row = lax.broadcasted_iota(I32, (tm, 1), 0)
    prev1 = tail_ref[7:8, :]
    prev2 = tail_ref[6:7, :]
    u1 = jnp.where(row >= 1, pltpu.roll(u, 1, 0), prev1)
    u2 = jnp.where(row >= 2, pltpu.roll(u, 2, 0), jnp.where(row == 1, prev1, prev2))
    y = cw_ref[2:3, :] * u + cw_ref[1:2, :] * u1 + cw_ref[0:1, :] * u2
    tail_ref[...] = u[tm - 8:, :]
    z = (b_gate * y).astype(BF16)
    o_ref[0] = x + _dot(z, wout_ref[...])


def _mixer(x, g, w_in, conv_w, w_out, tm=512):
    b, s, d = x.shape
    return pl.pallas_call(
        _mixer_kernel,
        grid=(b, s // tm),
        in_specs=[
            pl.BlockSpec((1, tm, d), lambda i, j: (i, j, 0)),
            pl.BlockSpec((1, d), lambda i, j: (0, 0)),
            pl.BlockSpec((d, 3 * d), lambda i, j: (0, 0)),
            pl.BlockSpec((3, d), lambda i, j: (0, 0)),
            pl.BlockSpec((d, d), lambda i, j: (0, 0)),
        ],
        out_specs=pl.BlockSpec((1, tm, d), lambda i, j: (i, j, 0)),
        out_shape=jax.ShapeDtypeStruct((b, s, d), F32),
        scratch_shapes=[pltpu.VMEM((8, d), F32)],
        compiler_params=_cparams("arbitrary", "arbitrary"),
        name="conv_mixer",
    )(x, g.reshape(1, d), w_in.astype(BF16), conv_w, w_out.astype(BF16))


def _router_kernel(x_ref, g_ref, wt_ref, b_ref, hb_ref, eid_ref, wts_ref, counts_ref, cnt_ref):
    tm = x_ref.shape[0]
    h = _rms_scale(x_ref[...]) * g_ref[...]
    hb_ref[...] = h.astype(BF16)
    lt = lax.dot_general(wt_ref[...], h, (((1,), (1,)), ((), ())),
                         precision=lax.Precision.HIGHEST, preferred_element_type=F32) + b_ref[...]
    r8 = lax.broadcasted_iota(I32, (8, 1), 0)
    gl = jnp.where(r8 < N_GROUPS, lt[0:8], -jnp.inf)
    gmax = jnp.max(gl, axis=0, keepdims=True)
    gtop = jnp.min(jnp.where(gl == gmax, r8, 8), axis=0, keepdims=True)
    p_g = 1.0 / jnp.sum(jnp.exp(gl - gmax), axis=0, keepdims=True)
    chosen = lt[8:16]
    for k in range(1, N_GROUPS):
        chosen = jnp.where(gtop == k, lt[8 + 8 * k:16 + 8 * k], chosen)
    v0 = jnp.max(chosen, axis=0, keepdims=True)
    i0 = jnp.min(jnp.where(chosen == v0, r8, 8), axis=0, keepdims=True)
    rest = jnp.where(r8 == i0, -jnp.inf, chosen)
    v1 = jnp.max(rest, axis=0, keepdims=True)
    i1 = jnp.min(jnp.where(rest == v1, r8, 8), axis=0, keepdims=True)
    e = jnp.exp(v1 - v0)
    w0 = p_g / (1.0 + e)
    w1 = p_g * e / (1.0 + e)
    e0 = gtop * EXPERTS_PER_GROUP + i0
    e1 = gtop * EXPERTS_PER_GROUP + i1
    wts_ref[...] = jnp.where(r8 == 0, w0, jnp.where(r8 == 1, w1, 0.0))

    @pl.when(pl.program_id(0) == 0)
    def _():
        cnt_ref[...] = jnp.zeros_like(cnt_ref)

    rexp = lax.broadcasted_iota(I32, (N_EXPERTS, 1), 0)
    tri = (lax.broadcasted_iota(I32, (tm, tm), 0) <= lax.broadcasted_iota(I32, (tm, tm), 1)).astype(BF16)
    oh0 = rexp == e0
    oh1 = rexp == e1
    cum0 = _dot(oh0.astype(BF16), tri)
    cum1 = _dot(oh1.astype(BF16), tri)
    before0 = cnt_ref[...]
    before1 = before0 + cum0[:, tm - 1:tm]
    rank0 = jnp.sum(jnp.where(oh0, before0 + cum0, 0.0), axis=0, keepdims=True) - 1.0
    rank1 = jnp.sum(jnp.where(oh1, before1 + cum1, 0.0), axis=0, keepdims=True) - 1.0
    cnt_ref[...] = before1 + cum1[:, tm - 1:tm]
    counts_ref[...] = jnp.broadcast_to(cnt_ref[...], counts_ref.shape).astype(I32)
    eid_ref[...] = jnp.where(r8 == 0, e0, jnp.where(r8 == 1, e1, jnp.where(
        r8 == 2, rank0.astype(I32), jnp.where(r8 == 3, rank1.astype(I32), 0))))


def _router(xf, g, w_rg, b_rg, w_re, b_re, tm=512):
    n, d = xf.shape
    rows = 8 + N_EXPERTS
    wt = jnp.zeros((rows, d), F32).at[:N_GROUPS].set(w_rg.T).at[8:].set(w_re.T)
    bt = jnp.zeros((rows, 1), F32).at[:N_GROUPS, 0].set(b_rg).at[8:, 0].set(b_re)
    return pl.pallas_call(
        _router_kernel,
        grid=(n // tm,),
        in_specs=[
            pl.BlockSpec((tm, d), lambda i: (i, 0)),
            pl.BlockSpec((1, d), lambda i: (0, 0)),
            pl.BlockSpec((rows, d), lambda i: (0, 0)),
            pl.BlockSpec((rows, 1), lambda i: (0, 0)),
        ],
        out_specs=[
            pl.BlockSpec((tm, d), lambda i: (i, 0)),
            pl.BlockSpec((8, tm), lambda i: (0, i)),
            pl.BlockSpec((8, tm), lambda i: (0, i)),
            pl.BlockSpec((N_EXPERTS, LANES), lambda i: (0, 0)),
        ],
        out_shape=[
            jax.ShapeDtypeStruct((n, d), BF16),
            jax.ShapeDtypeStruct((8, n), I32),
            jax.ShapeDtypeStruct((8, n), F32),
            jax.ShapeDtypeStruct((N_EXPERTS, LANES), I32),
        ],
        scratch_shapes=[pltpu.VMEM((N_EXPERTS, 1), F32)],
        compiler_params=_cparams("arbitrary"),
        name="moe_router",
    )(xf, g.reshape(1, d), wt, bt)


def _dispatch_plan(eid, counts, n_tok):
    a_exp = eid[0:2].reshape(-1)
    rank = eid[2:4].reshape(-1)
    n_asg = a_exp.shape[0]
    a_tok = jnp.arange(n_asg, dtype=I32) % n_tok
    padded = (counts + CHUNK_ROWS - 1) // CHUNK_ROWS * CHUNK_ROWS
    pend = jnp.cumsum(padded)
    pstart = pend - padded
    dest = (jnp.take(pstart, a_exp) + rank).astype(I32)
    n_chunks = -(-n_asg // CHUNK_ROWS) + N_EXPERTS
    row_tok = jnp.zeros((n_chunks * CHUNK_ROWS,), I32).at[dest].set(a_tok)
    chunk_exp = jnp.clip(jnp.searchsorted(pend, jnp.arange(n_chunks, dtype=I32) * CHUNK_ROWS, side='right'),
                         0, N_EXPERTS - 1).astype(I32)
    n_used = (pend[-1] // CHUNK_ROWS).astype(I32).reshape(1)
    return row_tok, dest.reshape(2, n_tok), chunk_exp, n_used, n_chunks


def _expert_kernel(ce_ref, nu_ref, xs_ref, wg_ref, wu_ref, wd_ref, o_ref, wg_bf, wu_bf, wd_bf):
    c = pl.program_id(0)

    @pl.when((c == 0) | (ce_ref[c] != ce_ref[jnp.maximum(c - 1, 0)]))
    def _():
        wg_bf[...] = wg_ref[0].astype(BF16)
        wu_bf[...] = wu_ref[0].astype(BF16)
        wd_bf[...] = wd_ref[0].astype(BF16)

    @pl.when(c < nu_ref[0])
    def _():
        xc = xs_ref[...]
        gate = _dot(xc, wg_bf[...])
        up = _dot(xc, wu_bf[...])
        act = (gate * jax.nn.sigmoid(gate) * up).astype(BF16)
        o_ref[...] = _dot(act, wd_bf[...])

    @pl.when(c >= nu_ref[0])
    def _():
        o_ref[...] = jnp.zeros_like(o_ref)


def _experts(xs, chunk_exp, n_used, w_gate, w_up, w_down, n_chunks):
    d = xs.shape[1]
    de = w_gate.shape[-1]
    grid_spec = pltpu.PrefetchScalarGridSpec(
        num_scalar_prefetch=2,
        grid=(n_chunks,),
        in_specs=[
            pl.BlockSpec((CHUNK_ROWS, d), lambda c, ce, nu: (c, 0)),
            pl.BlockSpec((1, d, de), lambda c, ce, nu: (ce[c], 0, 0)),
            pl.BlockSpec((1, d, de), lambda c, ce, nu: (ce[c], 0, 0)),
            pl.BlockSpec((1, de, d), lambda c, ce, nu: (ce[c], 0, 0)),
        ],
        out_specs=pl.BlockSpec((CHUNK_ROWS, d), lambda c, ce, nu: (c, 0)),
        scratch_shapes=[pltpu.VMEM((d, de), BF16), pltpu.VMEM((d, de), BF16), pltpu.VMEM((de, d), BF16)],
    )
    return pl.pallas_call(
        _expert_kernel,
        grid_spec=grid_spec,
        out_shape=jax.ShapeDtypeStruct((n_chunks * CHUNK_ROWS, d), F32),
        compiler_params=_cparams("arbitrary"),
        name="moe_experts",
    )(chunk_exp, n_used, xs, w_gate, w_up, w_down)


def _combine_kernel(x_ref, y0_ref, y1_ref, w_ref, g_ref, o_ref, *, final_norm):
    w = w_ref[...]
    out = x_ref[...] + w[:, 0:1] * y0_ref[...] + w[:, 1:2] * y1_ref[...]
    if final_norm:
        out = _rms_scale(out) * g_ref[...]
    o_ref[...] = out


def _combine(xf, y0, y1, wcol, g, final_norm, tm=512):
    n, d = xf.shape
    row = pl.BlockSpec((tm, d), lambda i: (i, 0))
    return pl.pallas_call(
        functools.partial(_combine_kernel, final_norm=final_norm),
        grid=(n // tm,),
        in_specs=[row, row, row, pl.BlockSpec((tm, 8), lambda i: (i, 0)), pl.BlockSpec((1, d), lambda i: (0, 0))],
        out_specs=row,
        out_shape=jax.ShapeDtypeStruct((n, d), F32),
        compiler_params=_cparams("arbitrary"),
        name="moe_combine",
    )(xf, y0, y1, wcol, g.reshape(1, d))


def _hier_moe(xf, g_ffn, w_rg, b_rg, w_re, b_re, w_gate, w_up, w_down, g_final, final_norm):
    n = xf.shape[0]
    hb, eid, wts, counts = _router(xf, g_ffn, w_rg, b_rg, w_re, b_re)
    row_tok, dest, chunk_exp, n_used, n_chunks = _dispatch_plan(eid, counts[:, 0], n)
    xs = jnp.take(hb, row_tok, axis=0)
    ys = _experts(xs, chunk_exp, n_used, w_gate, w_up, w_down, n_chunks)
    y0 = jnp.take(ys, dest[0], axis=0)
    y1 = jnp.take(ys, dest[1], axis=0)
    return _combine(xf, y0, y1, wts.T, g_final, final_norm)


def _proj_kernel(x_ref, prow_ref, pcol_ref, gkv_ref, gq_ref, wnat_ref, wvt_ref, wqt_ref,
                 kcv_ref, ks_ref, kw_ref, vst_ref, vwt_ref, qc_ref, qr_ref, gt_ref):
    tm = x_ref.shape[1]
    gd = N_KV_GROUPS * HEAD_DIM
    qd = N_HEADS * HEAD_DIM
    y = _rms_scale(x_ref[0])
    hkv = (y * gkv_ref[...]).astype(BF16)
    hq = (y * gq_ref[...]).astype(BF16)

    nat = _dot(hkv, wnat_ref[...])
    kcv_ref[0] = nat[:, :2 * gd]
    lane = lax.broadcasted_iota(I32, (1, LANES), 1)
    inv = jnp.exp((lane % HALF).astype(F32) * (-jnp.log(ROPE_THETA) / HALF))
    ang = pcol_ref[0].astype(F32) * inv
    ctab = jnp.where(lane < HEAD_DIM, jnp.cos(ang), 0.0)
    stab = jnp.where(lane < HEAD_DIM, jnp.sin(ang), 0.0)
    key = pl.program_id(1) * tm + lax.broadcasted_iota(I32, (tm, 1), 0)
    blk = (key // SLC_BLOCK) % BIAS_ROWS
    onehot = jnp.where(lane == HEAD_DIM + blk, 1.0, 0.0)
    for g in range(N_KV_GROUPS):
        for slot, ref in ((0, ks_ref), (1, kw_ref)):
            lo = 2 * gd + (slot * N_KV_GROUPS + g) * LANES
            t = nat[:, lo:lo + LANES]
            roped = t * ctab + pltpu.roll(t, HEAD_DIM, 1) * stab
            if slot == 0:
                roped = roped + onehot
            ref[0, g] = roped.astype(BF16)

    vt = _dot_nt(wvt_ref[...], hkv)
    ones_rows = _ones_row_block((N_KV_GROUPS, V_ROWS - HEAD_DIM, tm))
    vs3 = jnp.concatenate([vt[:gd].reshape(N_KV_GROUPS, HEAD_DIM, tm), ones_rows], axis=1).astype(BF16)
    vw3 = jnp.concatenate([vt[gd:].reshape(N_KV_GROUPS, HEAD_DIM, tm), ones_rows], axis=1).astype(BF16)
    vst_ref[0, :, 0] = vs3
    for w in range(tm // Q_BLOCK):
        vwt_ref[0, :, w] = vw3[:, :, w * Q_BLOCK:(w + 1) * Q_BLOCK]

    qt = _dot_nt(wqt_ref[...], hq)
    scale = HEAD_DIM ** -0.5 * LOG2E
    q3 = qt[:qd].reshape(N_HEADS, HEAD_DIM, tm)
    qc_ref[0] = (qt[:qd] * scale).astype(BF16)
    frq = lax.broadcasted_iota(I32, (HALF, 1), 0).astype(F32)
    ang_t = jnp.exp(frq * (-jnp.log(ROPE_THETA) / HALF)) * prow_ref[0].astype(F32)
    cos_t = jnp.cos(ang_t)[None]
    sin_t = jnp.sin(ang_t)[None]
    t1 = q3[:, :HALF]
    t2 = q3[:, HALF:]
    qr = jnp.concatenate([t1 * cos_t - t2 * sin_t, t2 * cos_t + t1 * sin_t], axis=1)
    qr_ref[0] = (qr * scale).reshape(qd, tm).astype(BF16)
    gt_ref[0] = jax.nn.sigmoid(qt[qd:])


def _rot_half_cols(w):
    return jnp.concatenate([-w[:, HALF:], w[:, :HALF]], axis=1)


def _projections(x, positions, g_kv, g_q, w_kv, w_q, tm=KEY_TILE):
    b, s, d = x.shape
    gd = N_KV_GROUPS * HEAD_DIM
    qd = N_HEADS * HEAD_DIM
    slot = lambda i: w_kv[:, i * gd:(i + 1) * gd]
    aug = []
    for sl in (2, 4):
        for g in range(N_KV_GROUPS):
            wk = slot(sl)[:, g * HEAD_DIM:(g + 1) * HEAD_DIM]
            aug += [wk, _rot_half_cols(wk)]
    w_nat = jnp.concatenate([slot(0), slot(1)] + aug, axis=1).astype(BF16)
    w_vt = jnp.concatenate([slot(3), slot(5)], axis=1).T.astype(BF16)
    gate_cols = w_q[:, qd:].reshape(d, N_KV_GROUPS, HEADS_PER_GROUP * N_BRANCH)
    gate_cols = jnp.pad(gate_cols, ((0, 0), (0, 0), (0, 16 - HEADS_PER_GROUP * N_BRANCH))).reshape(d, 64)
    w_qt = jnp.concatenate([w_q[:, :qd], gate_cols], axis=1).T.astype(BF16)
    nt = s // tm
    n_nat = w_nat.shape[1]
    const = lambda shape: pl.BlockSpec(shape, lambda i, j: (0,) * len(shape))
    return pl.pallas_call(
        _proj_kernel,
        grid=(b, nt),
        in_specs=[
            pl.BlockSpec((1, tm, d), lambda i, j: (i, j, 0)),
            pl.BlockSpec((1, 1, tm), lambda i, j: (i, 0, j)),
            pl.BlockSpec((1, tm, 1), lambda i, j: (i, j, 0)),
            const((1, d)), const((1, d)), const((d, n_nat)), const((2 * gd, d)), const((qd + 64, d)),
        ],
        out_specs=[
            pl.BlockSpec((1, tm, 2 * gd), lambda i, j: (i, j, 0)),
            pl.BlockSpec((1, N_KV_GROUPS, tm, LANES), lambda i, j: (i, 0, j, 0)),
            pl.BlockSpec((1, N_KV_GROUPS, tm, LANES), lambda i, j: (i, 0, j, 0)),
            pl.BlockSpec((1, N_KV_GROUPS, 1, V_ROWS, tm), lambda i, j: (i, 0, j, 0, 0)),
            pl.BlockSpec((1, N_KV_GROUPS, tm // Q_BLOCK, V_ROWS, Q_BLOCK), lambda i, j: (i, 0, j, 0, 0)),
            pl.BlockSpec((1, qd, tm), lambda i, j: (i, 0, j)),
            pl.BlockSpec((1, qd, tm), lambda i, j: (i, 0, j)),
            pl.BlockSpec((1, 64, tm), lambda i, j: (i, 0, j)),
        ],
        out_shape=[
            jax.ShapeDtypeStruct((b, s, 2 * gd), F32),
            jax.ShapeDtypeStruct((b, N_KV_GROUPS, s, LANES), BF16),
            jax.ShapeDtypeStruct((b, N_KV_GROUPS, s, LANES), BF16),
            jax.ShapeDtypeStruct((b, N_KV_GROUPS, nt, V_ROWS, tm), BF16),
            jax.ShapeDtypeStruct((b, N_KV_GROUPS, s // Q_BLOCK, V_ROWS, Q_BLOCK), BF16),
            jax.ShapeDtypeStruct((b, qd, s), BF16),
            jax.ShapeDtypeStruct((b, qd, s), BF16),
            jax.ShapeDtypeStruct((b, 64, s), F32),
        ],
        compiler_params=_cparams("arbitrary", "arbitrary"),
        name="nsa_projections",
    )(x, positions.reshape(b, 1, s), positions.reshape(b, s, 1), g_kv.reshape(1, d), g_q.reshape(1, d),
      w_nat, w_vt, w_qt)


def _compress_kernel(a_ref, pos_ref, w1_ref, w2_ref, o_ref, *, transposed_out):
    half = w1_ref.shape[0] // 2
    a = a_ref[0, 0]
    top = (a + pos_ref[0:1, :]).astype(BF16)
    bot = (a + pos_ref[1:2, :]).astype(BF16)
    p = _dot(top, w1_ref[:half, :])
    q = _dot(bot, w1_ref[half:, :])
    n = a.shape[0]
    hidden = p + pltpu.roll(q, n - 1, 0)
    act = jax.nn.gelu(hidden, approximate=True).astype(BF16)
    if transposed_out:
        v_t = _dot_nt(w2_ref[...], act)
        o_ref[0, 0] = jnp.concatenate([v_t, _ones_row_block((V_ROWS - HEAD_DIM, n))], axis=0).astype(BF16)
    else:
        o_ref[0, 0] = _dot(act, w2_ref[...]).astype(BF16)


def _compress(a, pos, w1, w2, transposed_out):
    b, g, n, width = a.shape
    hidden = w1.shape[1]
    pos2 = pos.reshape(2, width)
    if transposed_out:
        w2p = w2.T.astype(BF16)
        out_block, out_shape = (1, 1, V_ROWS, n), (b, g, V_ROWS, n)
    else:
        w2p = jnp.pad(w2, ((0, 0), (0, LANES - HEAD_DIM))).astype(BF16)
        out_block, out_shape = (1, 1, n, LANES), (b, g, n, LANES)
    return pl.pallas_call(
        functools.partial(_compress_kernel, transposed_out=transposed_out),
        grid=(b, g),
        in_specs=[
            pl.BlockSpec((1, 1, n, width), lambda i, j: (i, j, 0, 0)),
            pl.BlockSpec((2, width), lambda i, j: (0, 0)),
            pl.BlockSpec((2 * width, hidden), lambda i, j: (0, 0)),
            pl.BlockSpec(w2p.shape, lambda i, j: (0, 0)),
        ],
        out_specs=pl.BlockSpec(out_block, lambda i, j: (i, j, 0, 0)),
        out_shape=jax.ShapeDtypeStruct(out_shape, BF16),
        compiler_params=_cparams("arbitrary", "arbitrary"),
        name="nsa_compress_v" if transposed_out else "nsa_compress_k",
    )(a, pos2, w1.astype(BF16), w2p)


def _heads_on_lanes(ref):
    return jnp.concatenate([ref[0, h * HEAD_DIM:(h + 1) * HEAD_DIM, :] for h in range(HEADS_PER_GROUP)], axis=1)


def _nsa_kernel(qc_ref, qr_ref, g_ref, kc_ref, lc_ref, ks_ref, vst_ref, kw_ref, vwt_ref,
                o_ref, bias_ref, score_ref, sc_ref, ec_ref, sa_ref, sb_ref, pa_ref, pb_ref):
    i = pl.program_id(2)
    qb = Q_BLOCK
    width = HEADS_PER_GROUP * qb
    n_cmp = kc_ref.shape[2]
    n_slc = score_ref.shape[0]
    t_q = i * qb + lax.broadcasted_iota(I32, (1, qb), 1)
    t_row = jnp.concatenate([t_q] * HEADS_PER_GROUP, axis=1)
    zeros_pad = jnp.zeros((LANES - HEAD_DIM, width), BF16)

    cc = min(CMP_CHUNK, n_cmp)
    n_cc = jnp.minimum((i * (qb // CMP_STRIDE) + qb // CMP_STRIDE + cc - 1) // cc, n_cmp // cc)
    qc_t = jnp.concatenate([_heads_on_lanes(qc_ref), zeros_pad], axis=0)

    @pl.when(i == 0)
    def _():
        ec_ref[...] = jnp.zeros_like(ec_ref)

    def cmp_scores(ch, m):
        r0 = pl.multiple_of(ch * cc, cc)
        s = _dot(kc_ref[0, 0, pl.ds(r0, cc), :], qc_t)
        cmp_end = (r0 + lax.broadcasted_iota(I32, (cc, 1), 0)) * CMP_STRIDE + (CMP_BLOCK - 1)
        s = jnp.where(cmp_end <= t_row, s, NEG)
        sc_ref[pl.ds(r0, cc), :] = s
        return jnp.maximum(m, jnp.max(s, axis=0, keepdims=True))

    m_c = lax.fori_loop(0, n_cc, cmp_scores, jnp.full((1, width), NEG, F32))
    m_c = jnp.maximum(m_c, NEG * 1e-10)

    def cmp_exp(ch, carry):
        r0 = pl.multiple_of(ch * cc, cc)
        ec_ref[pl.ds(r0, cc), :] = jnp.exp2(sc_ref[pl.ds(r0, cc), :] - m_c).astype(BF16)
        return carry

    lax.fori_loop(0, n_cc, cmp_exp, 0)
    res = _dot(lc_ref[0, 0], ec_ref[...])
    rinv = 1.0 / jnp.maximum(res[HEAD_DIM:HEAD_DIM + 1], 1e-30)
    o_cmp = res[:HEAD_DIM] * rinv
    u = res[V_ROWS:] * rinv
    imp = u[:, 0:qb]
    for h in range(1, HEADS_PER_GROUP):
        imp = imp + u[:, h * qb:(h + 1) * qb]

    j_idx = lax.broadcasted_iota(I32, (n_slc, 1), 0)
    cur = t_q // SLC_BLOCK
    valid = j_idx * SLC_BLOCK <= t_q
    forced = (j_idx == 0) | (j_idx == cur) | (j_idx == cur - 1)
    score_ref[...] = jnp.where(valid & jnp.logical_not(forced), imp, -1.0)

    def pick(_, carry, rows):
        sc = score_ref[0:rows, :]
        best = jnp.max(sc, axis=0, keepdims=True)
        idx = jnp.min(jnp.where(sc == best, j_idx[0:rows], n_slc), axis=0, keepdims=True)
        score_ref[0:rows, :] = jnp.where(j_idx[0:rows] == idx, PICKED, sc)
        return carry

    pick_rows = min(PICK_ROWS, n_slc)
    need = (i * qb + qb - 1) // (SLC_BLOCK * pick_rows)
    for r in range(n_slc // pick_rows):
        @pl.when(need == r)
        def _(rows=(r + 1) * pick_rows):
            lax.fori_loop(0, N_SELECT - N_FORCED, functools.partial(pick, rows=rows), 0)
    bias = jnp.where(forced | (score_ref[...] == PICKED), 0.0, NEG).astype(BF16)
    bias_ref[...] = jnp.concatenate([bias] * HEADS_PER_GROUP, axis=1)

    qr_heads = _heads_on_lanes(qr_ref)
    zeros_sel = jnp.zeros((LANES - HEAD_DIM - BIAS_ROWS, width), BF16)
    tk = KEY_TILE
    blocks_per_tile = tk // SLC_BLOCK

    def scores(kt, s_ref, masked):
        b0 = pl.multiple_of((kt * blocks_per_tile) // BIAS_ROWS * BIAS_ROWS, BIAS_ROWS)
        q_aug = jnp.concatenate([qr_heads, bias_ref[pl.ds(b0, BIAS_ROWS), :], zeros_sel], axis=0)
        k0 = pl.multiple_of(kt * tk, tk)
        s = _dot(ks_ref[0, 0, pl.ds(k0, tk), :], q_aug)
        if masked:
            s = jnp.where(k0 + lax.broadcasted_iota(I32, (tk, 1), 0) <= t_row, s, NEG)
        s_ref[...] = s
        part = jnp.max(s.reshape(tk // MAX_ROWS, MAX_ROWS, width), axis=0)
        return jnp.max(part, axis=0, keepdims=True)

    def values(kt, p_ref):
        return _dot(vst_ref[0, 0, jnp.maximum(kt, 0)], p_ref[...])

    def trip(kk, carry, next_scores):
        m, acc, beta, cmax_a, cmax_b = carry
        kt = 2 * kk
        live = jnp.where(kk > 0, 1.0, 0.0)
        acc = acc + live * (beta * values(kt - 2, pa_ref) + values(kt - 1, pb_ref))
        m1 = jnp.maximum(m, cmax_a)
        pa_ref[...] = jnp.exp2((sa_ref[...] - m1).astype(BF16))
        m2 = jnp.maximum(m1, cmax_b)
        pb_ref[...] = jnp.exp2((sb_ref[...] - m2).astype(BF16))
        beta = jnp.exp2(m1 - m2)
        if next_scores is not None:
            cmax_a = scores(kt + 2, sa_ref, next_scores)
            cmax_b = scores(kt + 3, sb_ref, next_scores)
        return m2, (jnp.exp2(m - m1) * beta) * acc, beta, cmax_a, cmax_b

    @pl.when((pl.program_id(0) == 0) & (pl.program_id(1) == 0) & (i == 0))
    def _():
        pa_ref[...] = jnp.zeros_like(pa_ref)
        pb_ref[...] = jnp.zeros_like(pb_ref)

    n_full = (i * qb) // tk
    n_unmasked = jnp.maximum(n_full // 2 - 1, 0)
    carry = (jnp.full((1, width), NEG, F32), jnp.zeros((V_ROWS, width), F32), jnp.ones((1, width), F32),
             scores(0, sa_ref, True), scores(1, sb_ref, True))
    carry = lax.fori_loop(0, n_unmasked, functools.partial(trip, next_scores=False), carry)
    carry = lax.fori_loop(n_unmasked, n_full // 2, functools.partial(trip, next_scores=True), carry)
    last = n_full // 2
    _, acc_s, beta, _, _ = trip(last, carry, None)
    acc_s = acc_s + beta * values(2 * last, pa_ref) + values(2 * last + 1, pb_ref)
    o_slc = acc_s[:HEAD_DIM] / acc_s[HEAD_DIM:HEAD_DIM + 1]

    qr_t = jnp.concatenate([qr_heads, zeros_pad], axis=0)
    n_win = WINDOW // qb + 1
    s_w = []
    for r in range(n_win):
        j = i - (n_win - 1) + r
        k0 = pl.multiple_of(jnp.maximum(j, 0) * qb, qb)
        s = _dot(kw_ref[0, 0, pl.ds(k0, qb), :], qr_t)
        if r in (0, n_win - 1):
            kpos = j * qb + lax.broadcasted_iota(I32, (qb, 1), 0)
            diff = t_row - kpos
            s = jnp.where((diff >= 0) & (diff < WINDOW) & (kpos >= 0), s, NEG)
        else:
            s = s + jnp.where(j >= 0, 0.0, NEG)
        s_w.append(s)
    m_w = s_w[0].max(axis=0, keepdims=True)
    for r in range(1, n_win):
        m_w = jnp.maximum(m_w, s_w[r].max(axis=0, keepdims=True))
    acc_w = jnp.zeros((V_ROWS, width), F32)
    for r in range(n_win):
        jc = jnp.maximum(i - (n_win - 1) + r, 0)
        acc_w = acc_w + _dot(vwt_ref[0, 0, jc], jnp.exp2((s_w[r] - m_w).astype(BF16)))
    o_win = acc_w[:HEAD_DIM] / acc_w[HEAD_DIM:HEAD_DIM + 1]

    gts = g_ref[0]
    for h in range(HEADS_PER_GROUP):
        sl = slice(h * qb, (h + 1) * qb)
        r0 = h * N_BRANCH
        o = (gts[r0:r0 + 1] * o_cmp[:, sl] + gts[r0 + 1:r0 + 2] * o_slc[:, sl] + gts[r0 + 2:r0 + 3] * o_win[:, sl])
        o_ref[0, h * HEAD_DIM:(h + 1) * HEAD_DIM, :] = o.astype(BF16)


def _nsa_attention(qc_t, qr_t, gates_t, k_cmp, lhs_cmp, k_slc, v_slc_t, k_win, v_win_t):
    b, qd, s = qc_t.shape
    g = N_KV_GROUPS
    n_cmp = k_cmp.shape[2]
    n_slc = lhs_cmp.shape[2] - V_ROWS
    nt, tk = v_slc_t.shape[2], v_slc_t.shape[4]
    rows = HEADS_PER_GROUP * HEAD_DIM
    width = HEADS_PER_GROUP * Q_BLOCK
    qspec = pl.BlockSpec((1, rows, Q_BLOCK), lambda bi, gi, i: (bi, gi, i))
    per_bg = lambda shape: pl.BlockSpec((1, 1) + shape, lambda bi, gi, i: (bi, gi) + (0,) * len(shape))
    return pl.pallas_call(
        _nsa_kernel,
        grid=(b, g, s // Q_BLOCK),
        in_specs=[
            qspec, qspec,
            pl.BlockSpec((1, 16, Q_BLOCK), lambda bi, gi, i: (bi, gi, i)),
            per_bg((n_cmp, LANES)), per_bg((V_ROWS + n_slc, n_cmp)),
            per_bg((s, LANES)), per_bg((nt, V_ROWS, tk)),
            per_bg((s, LANES)), per_bg((s // Q_BLOCK, V_ROWS, Q_BLOCK)),
        ],
        out_specs=qspec,
        out_shape=jax.ShapeDtypeStruct((b, qd, s), BF16),
        scratch_shapes=[
            pltpu.VMEM((n_slc, width), BF16),
            pltpu.VMEM((n_slc, Q_BLOCK), F32),
            pltpu.VMEM((n_cmp, width), F32),
            pltpu.VMEM((n_cmp, width), BF16),
            pltpu.VMEM((tk, width), F32), pltpu.VMEM((tk, width), F32),
            pltpu.VMEM((tk, width), BF16), pltpu.VMEM((tk, width), BF16),
        ],
        compiler_params=_cparams("arbitrary", "arbitrary", "arbitrary"),
        name="nsa_attention",
    )(qc_t, qr_t, gates_t, k_cmp, lhs_cmp, k_slc, v_slc_t, k_win, v_win_t)


def _selection_map_t(n_cmp_pad, n_slc):
    c0 = jnp.arange(n_cmp_pad) * CMP_STRIDE
    j0 = jnp.arange(n_slc) * SLC_BLOCK
    ov = jnp.minimum(c0[None, :] + CMP_BLOCK, j0[:, None] + SLC_BLOCK) - jnp.maximum(c0[None, :], j0[:, None])
    return (jnp.clip(ov, 0).astype(F32) / CMP_BLOCK).astype(BF16)


def _out_proj_kernel(x_ref, ot_ref, w_ref, o_ref):
    o_ref[0] = x_ref[0] + _dot_tn(ot_ref[0], w_ref[...])


def _out_proj(x, o_t, w_out, tm=512):
    b, s, d = x.shape
    qd = o_t.shape[1]
    return pl.pallas_call(
        _out_proj_kernel,
        grid=(b, s // tm),
        in_specs=[
            pl.BlockSpec((1, tm, d), lambda i, j: (i, j, 0)),
            pl.BlockSpec((1, qd, tm), lambda i, j: (i, 0, j)),
            pl.BlockSpec((qd, d), lambda i, j: (0, 0)),
        ],
        out_specs=pl.BlockSpec((1, tm, d), lambda i, j: (i, j, 0)),
        out_shape=jax.ShapeDtypeStruct((b, s, d), F32),
        compiler_params=_cparams("arbitrary", "arbitrary"),
        name="nsa_out_proj",
    )(x, o_t, w_out.astype(BF16))


def _nsa_layer(x, positions, g_kv, g_q, w_kv, cmp_pos_k, cmp_w1_k, cmp_w2_k, cmp_pos_v, cmp_w1_v, cmp_w2_v,
               w_in, w_out):
    b, s, d = x.shape
    g = N_KV_GROUPS
    kcv, k_slc, k_win, v_slc_t, v_win_t, qc_t, qr_t, gates_t = _projections(x, positions, g_kv, g_q, w_kv, w_in)
    n_half = s // CMP_STRIDE
    halves = kcv.reshape(b, n_half, CMP_STRIDE, 2, g, HEAD_DIM).transpose(0, 3, 4, 1, 2, 5)
    halves = halves.reshape(b, 2, g, n_half, CMP_STRIDE * HEAD_DIM)
    k_cmp = _compress(halves[:, 0], cmp_pos_k, cmp_w1_k, cmp_w2_k, transposed_out=False)
    v_cmp_t = _compress(halves[:, 1], cmp_pos_v, cmp_w1_v, cmp_w2_v, transposed_out=True)
    map_t = _selection_map_t(n_half, s // SLC_BLOCK)
    lhs_cmp = jnp.concatenate([v_cmp_t, jnp.broadcast_to(map_t, (b, g) + map_t.shape)], axis=2)
    o_t = _nsa_attention(qc_t, qr_t, gates_t, k_cmp, lhs_cmp, k_slc, v_slc_t, k_win, v_win_t)
    return _out_proj(x, o_t, w_out)


def kernel(x, positions, norm_mix, norm_ffn, conv_w_in, conv_w, conv_w_out, norm_kv, w_kv, cmp_pos_k, cmp_w1_k, cmp_w2_k, cmp_pos_v, cmp_w1_v, cmp_w2_v, attn_w_in, attn_w_out, router_group_w, router_group_b, router_expert_w, router_expert_b, expert_w_gate, expert_w_up, expert_w_down, norm_final):
    b, s, d = x.shape
    assert s % (2 * KEY_TILE) == 0 and (s // SLC_BLOCK) % BIAS_ROWS == 0 and WINDOW % Q_BLOCK == 0

    def moe(xs, l, final_norm):
        return _hier_moe(xs.reshape(b * s, d), norm_ffn[l], router_group_w[l], router_group_b[l],
                         router_expert_w[l], router_expert_b[l], expert_w_gate[l], expert_w_up[l],
                         expert_w_down[l], norm_final, final_norm).reshape(b, s, d)

    x = _mixer(x, norm_mix[0], conv_w_in[0], conv_w[0], conv_w_out[0])
    x = moe(x, 0, False)
    x = _nsa_layer(x, positions, norm_kv, norm_mix[1], w_kv, cmp_pos_k, cmp_w1_k, cmp_w2_k,
                   cmp_pos_v, cmp_w1_v, cmp_w2_v, attn_w_in[0], attn_w_out[0])
    return moe(x, 1, True)
```

```python
import functools

import jax
import jax.numpy as jnp
from jax import lax
from jax.experimental import pallas as pl
from jax.experimental.pallas import tpu as pltpu

F32 = jnp.float32
BF16 = jnp.bfloat16
I32 = jnp.int32

RMS_EPS = 1e-6
ROPE_THETA = 10000.0
HEAD_DIM = 64
HALF = HEAD_DIM // 2
N_HEADS = 16
N_KV_GROUPS = 4
HEADS_PER_GROUP = N_HEADS // N_KV_GROUPS
N_BRANCH = 3
CMP_BLOCK = 32
CMP_STRIDE = 16
SLC_BLOCK = 64
N_SELECT = 16
WINDOW = 512
Q_BLOCK = 256
FORCE_BONUS = 1000.0
N_GROUPS = 4
EXPERTS_PER_GROUP = 8
N_EXPERTS = N_GROUPS * EXPERTS_PER_GROUP
CHUNK_ROWS = 256
NEG = -1e30
LANES = 128
KEY_TILE = 512
BIAS_ROWS = 16
MAX_ROWS = 64
CMP_CHUNK = 256
N_FORCED = 3
PICK_ROWS = 64
PICKED = -2.0
V_ROWS = 80
LOG2E = 1.4426950408889634
VMEM_LIMIT = 56 * 1024 * 1024


def _cparams(*sem):
    return pltpu.CompilerParams(dimension_semantics=sem, vmem_limit_bytes=VMEM_LIMIT)


def _rms_scale(x):
    return x * lax.rsqrt(jnp.mean(x * x, axis=-1, keepdims=True) + RMS_EPS)


def _dot(a, b):
    return jnp.dot(a, b, preferred_element_type=F32)


def _dot_nt(a, b):
    return lax.dot_general(a, b, (((1,), (1,)), ((), ())), preferred_element_type=F32)


def _dot_tn(a, b):
    return lax.dot_general(a, b, (((0,), (0,)), ((), ())), preferred_element_type=F32)


def _ones_row_block(shape):
    row = lax.broadcasted_iota(I32, shape, len(shape) - 2)
    return jnp.where(row == 0, 1.0, 0.0).astype(F32)


def _mixer_kernel(x_ref, g_ref, win_ref, cw_ref, wout_ref, o_ref, tail_ref):
    d = x_ref.shape[-1]
    tm = x_ref.shape[1]

    @pl.when(pl.program_id(1) == 0)
    def _():
        tail_ref[...] = jnp.zeros_like(tail_ref)

    x = x_ref[0]
    h = (_rms_scale(x) * g_ref[...]).astype(BF16)
    proj = _dot(h, win_ref[...])
    b_gate = proj[:, :d]
    u = proj[:, d:2 * d] * proj[:, 2 * d:]
    row = lax.broadcasted_iota(I32, (tm, 1), 0)
    prev1 = tail_ref[7:8, :]
    prev2 = tail_ref[6:7, :]
    u1 = jnp.where(row >= 1, pltpu.roll(u, 1, 0), prev1)
    u2 = jnp.where(row >= 2, pltpu.roll(u, 2, 0), jnp.where(row == 1, prev1, prev2))
    y = cw_ref[2:3, :] * u + cw_ref[1:2, :] * u1 + cw_ref[0:1, :] * u2
    tail_ref[...] = u[tm - 8:, :]
    z = (b_gate * y).astype(BF16)
    o_ref[0] = x + _dot(z, wout_ref[...])


def _mixer(x, g, w_in, conv_w, w_out, tm=512):
    b, s, d = x.shape
    return pl.pallas_call(
        _mixer_kernel,
        grid=(b, s // tm),
        in_specs=[
            pl.BlockSpec((1, tm, d), lambda i, j: (i, j, 0)),
            pl.BlockSpec((1, d), lambda i, j: (0, 0)),
            pl.BlockSpec((d, 3 * d), lambda i, j: (0, 0)),
            pl.BlockSpec((3, d), lambda i, j: (0, 0)),
            pl.BlockSpec((d, d), lambda i, j: (0, 0)),
        ],
        out_specs=pl.BlockSpec((1, tm, d), lambda i, j: (i, j, 0)),
        out_shape=jax.ShapeDtypeStruct((b, s, d), F32),
        scratch_shapes=[pltpu.VMEM((8, d), F32)],
        compiler_params=_cparams("arbitrary", "arbitrary"),
        name="conv_mixer",
    )(x, g.reshape(1, d), w_in.astype(BF16), conv_w, w_out.astype(BF16))


def _router_kernel(x_ref, g_ref, wt_ref, b_ref, hb_ref, eid_ref, wts_ref, counts_ref, cnt_ref):
    tm = x_ref.shape[0]
    h = _rms_scale(x_ref[...]) * g_ref[...]
    hb_ref[...] = h.astype(BF16)
    lt = lax.dot_general(wt_ref[...], h, (((1,), (1,)), ((), ())),
                         precision=lax.Precision.HIGHEST, preferred_element_type=F32) + b_ref[...]
    r8 = lax.broadcasted_iota(I32, (8, 1), 0)
    gl = jnp.where(r8 < N_GROUPS, lt[0:8], -jnp.inf)
    gmax = jnp.max(gl, axis=0, keepdims=True)
    gtop = jnp.min(jnp.where(gl == gmax, r8, 8), axis=0, keepdims=True)
    p_g = 1.0 / jnp.sum(jnp.exp(gl - gmax), axis=0, keepdims=True)
    chosen = lt[8:16]
    for k in range(1, N_GROUPS):
        chosen = jnp.where(gtop == k, lt[8 + 8 * k:16 + 8 * k], chosen)
    v0 = jnp.max(chosen, axis=0, keepdims=True)
    i0 = jnp.min(jnp.where(chosen == v0, r8, 8), axis=0, keepdims=True)
    rest = jnp.where(r8 == i0, -jnp.inf, chosen)
    v1 = jnp.max(rest, axis=0, keepdims=True)
    i1 = jnp.min(jnp.where(rest == v1, r8, 8), axis=0, keepdims=True)
    e = jnp.exp(v1 - v0)
    w0 = p_g / (1.0 + e)
    w1 = p_g * e / (1.0 + e)
    e0 = gtop * EXPERTS_PER_GROUP + i0
    e1 = gtop * EXPERTS_PER_GROUP + i1
    wts_ref[...] = jnp.where(r8 == 0, w0, jnp.where(r8 == 1, w1, 0.0))

    @pl.when(pl.program_id(0) == 0)
    def _():
        cnt_ref[...] = jnp.zeros_like(cnt_ref)

    rexp = lax.broadcasted_iota(I32, (N_EXPERTS, 1), 0)
    tri = (lax.broadcasted_iota(I32, (tm, tm), 0) <= lax.broadcasted_iota(I32, (tm, tm), 1)).astype(BF16)
    oh0 = rexp == e0
    oh1 = rexp == e1
    cum0 = _dot(oh0.astype(BF16), tri)
    cum1 = _dot(oh1.astype(BF16), tri)
    before0 = cnt_ref[...]
    before1 = before0 + cum0[:, tm - 1:tm]
    rank0 = jnp.sum(jnp.where(oh0, before0 + cum0, 0.0), axis=0, keepdims=True) - 1.0
    rank1 = jnp.sum(jnp.where(oh1, before1 + cum1, 0.0), axis=0, keepdims=True) - 1.0
    cnt_ref[...] = before1 + cum1[:, tm - 1:tm]
    counts_ref[...] = jnp.broadcast_to(cnt_ref[...], counts_ref.shape).astype(I32)
    eid_ref[...] = jnp.where(r8 == 0, e0, jnp.where(r8 == 1, e1, jnp.where(
        r8 == 2, rank0.astype(I32), jnp.where(r8 == 3, rank1.astype(I32), 0))))


def _router(xf, g, w_rg, b_rg, w_re, b_re, tm=512):
    n, d = xf.shape
    rows = 8 + N_EXPERTS
    wt = jnp.zeros((rows, d), F32).at[:N_GROUPS].set(w_rg.T).at[8:].set(w_re.T)
    bt = jnp.zeros((rows, 1), F32).at[:N_GROUPS, 0].set(b_rg).at[8:, 0].set(b_re)
    return pl.pallas_call(
        _router_kernel,
        grid=(n // tm,),
        in_specs=[
            pl.BlockSpec((tm, d), lambda i: (i, 0)),
            pl.BlockSpec((1, d), lambda i: (0, 0)),
            pl.BlockSpec((rows, d), lambda i: (0, 0)),
            pl.BlockSpec((rows, 1), lambda i: (0, 0)),
        ],
        out_specs=[
            pl.BlockSpec((tm, d), lambda i: (i, 0)),
            pl.BlockSpec((8, tm), lambda i: (0, i)),
            pl.BlockSpec((8, tm), lambda i: (0, i)),
            pl.BlockSpec((N_EXPERTS, LANES), lambda i: (0, 0)),
        ],
        out_shape=[
            jax.ShapeDtypeStruct((n, d), BF16),
            jax.ShapeDtypeStruct((8, n), I32),
            jax.ShapeDtypeStruct((8, n), F32),
            jax.ShapeDtypeStruct((N_EXPERTS, LANES), I32),
        ],
        scratch_shapes=[pltpu.VMEM((N_EXPERTS, 1), F32)],
        compiler_params=_cparams("arbitrary"),
        name="moe_router",
    )(xf, g.reshape(1, d), wt, bt)


def _take_rows(a, idx):
    return a.at[idx].get(mode="promise_in_bounds")


def _dispatch_plan(eid, counts, n_tok):
    a_exp = eid[0:2].reshape(-1)
    rank = eid[2:4].reshape(-1)
    n_asg = a_exp.shape[0]
    a_tok = jnp.arange(n_asg, dtype=I32) % n_tok
    padded = (counts + CHUNK_ROWS - 1) // CHUNK_ROWS * CHUNK_ROWS
    pend = jnp.cumsum(padded)
    pstart = pend - padded
    dest = (_take_rows(pstart, a_exp) + rank).astype(I32)
    n_chunks = -(-n_asg // CHUNK_ROWS) + N_EXPERTS
    row_tok = jnp.zeros((n_chunks * CHUNK_ROWS,), I32).at[dest].set(
        a_tok, unique_indices=True, mode="promise_in_bounds")
    chunk_row0 = jnp.arange(n_chunks, dtype=I32) * CHUNK_ROWS
    chunk_exp = jnp.minimum(jnp.sum((pend[None, :] <= chunk_row0[:, None]).astype(I32), axis=1), N_EXPERTS - 1)
    n_used = (pend[-1] // CHUNK_ROWS).astype(I32).reshape(1)
    return row_tok, dest.reshape(2, n_tok), chunk_exp, n_used, n_chunks


def _expert_kernel(ce_ref, nu_ref, xs_ref, wg_ref, wu_ref, wd_ref, o_ref, wg_bf, wu_bf, wd_bf):
    c = pl.program_id(0)

    @pl.when((c == 0) | (ce_ref[c] != ce_ref[jnp.maximum(c - 1, 0)]))
    def _():
        wg_bf[...] = wg_ref[0, 0].astype(BF16)
        wu_bf[...] = wu_ref[0, 0].astype(BF16)
        wd_bf[...] = wd_ref[0, 0].astype(BF16)

    @pl.when(c < nu_ref[0])
    def _():
        xc = xs_ref[...]
        gate = _dot(xc, wg_bf[...])
        up = _dot(xc, wu_bf[...])
        act = (gate * jax.nn.sigmoid(gate) * up).astype(BF16)
        o_ref[...] = _dot(act, wd_bf[...])

    @pl.when(c >= nu_ref[0])
    def _():
        o_ref[...] = jnp.zeros_like(o_ref)


def _experts(xs, chunk_exp, n_used, w_gate, w_up, w_down, layer, n_chunks):
    d = xs.shape[1]
    de = w_gate.shape[-1]
    grid_spec = pltpu.PrefetchScalarGridSpec(
        num_scalar_prefetch=2,
        grid=(n_chunks,),
        in_specs=[
            pl.BlockSpec((CHUNK_ROWS, d), lambda c, ce, nu: (c, 0)),
            pl.BlockSpec((1, 1, d, de), lambda c, ce, nu: (layer, ce[c], 0, 0)),
            pl.BlockSpec((1, 1, d, de), lambda c, ce, nu: (layer, ce[c], 0, 0)),
            pl.BlockSpec((1, 1, de, d), lambda c, ce, nu: (layer, ce[c], 0, 0)),
        ],
        out_specs=pl.BlockSpec((CHUNK_ROWS, d), lambda c, ce, nu: (c, 0)),
        scratch_shapes=[pltpu.VMEM((d, de), BF16), pltpu.VMEM((d, de), BF16), pltpu.VMEM((de, d), BF16)],
    )
    return pl.pallas_call(
        _expert_kernel,
        grid_spec=grid_spec,
        out_shape=jax.ShapeDtypeStruct((n_chunks * CHUNK_ROWS, d), F32),
        compiler_params=_cparams("arbitrary"),
        name="moe_experts",
    )(chunk_exp, n_used, xs, w_gate, w_up, w_down)


def _combine_kernel(x_ref, y0_ref, y1_ref, w_ref, g_ref, o_ref, *, final_norm):
    w = w_ref[...]
    out = x_ref[...] + w[:, 0:1] * y0_ref[...] + w[:, 1:2] * y1_ref[...]
    if final_norm:
        out = _rms_scale(out) * g_ref[...]
    o_ref[...] = out


def _combine(xf, y0, y1, wcol, g, final_norm, tm=512):
    n, d = xf.shape
    row = pl.BlockSpec((tm, d), lambda i: (i, 0))
    return pl.pallas_call(
        functools.partial(_combine_kernel, final_norm=final_norm),
        grid=(n // tm,),
        in_specs=[row, row, row, pl.BlockSpec((tm, 8), lambda i: (i, 0)), pl.BlockSpec((1, d), lambda i: (0, 0))],
        out_specs=row,
        out_shape=jax.ShapeDtypeStruct((n, d), F32),
        compiler_params=_cparams("arbitrary"),
        name="moe_combine",
    )(xf, y0, y1, wcol, g.reshape(1, d))


def _hier_moe(xf, g_ffn, w_rg, b_rg, w_re, b_re, w_gate, w_up, w_down, layer, g_final, final_norm):
    n = xf.shape[0]
    hb, eid, wts, counts = _router(xf, g_ffn, w_rg, b_rg, w_re, b_re)
    row_tok, dest, chunk_exp, n_used, n_chunks = _dispatch_plan(eid, counts[:, 0], n)
    xs = _take_rows(hb, row_tok)
    ys = _experts(xs, chunk_exp, n_used, w_gate, w_up, w_down, layer, n_chunks)
    return _combine(xf, _take_rows(ys, dest[0]), _take_rows(ys, dest[1]), wts.T, g_final, final_norm)


def _proj_kernel(x_ref, prow_ref, pcol_ref, gkv_ref, gq_ref, wnat_ref, wvt_ref, wqt_ref,
                 kcv_ref, ks_ref, kw_ref, vst_ref, vwt_ref, qc_ref, qr_ref, gt_ref):
    tm = x_ref.shape[1]
    gd = N_KV_GROUPS * HEAD_DIM
    qd = N_HEADS * HEAD_DIM
    y = _rms_scale(x_ref[0])
    hkv = (y * gkv_ref[...]).astype(BF16)
    hq = (y * gq_ref[...]).astype(BF16)

    nat = _dot(hkv, wnat_ref[...])
    kcv_ref[0] = nat[:, :2 * gd]
    lane = lax.broadcasted_iota(I32, (1, LANES), 1)
    inv = jnp.exp((lane % HALF).astype(F32) * (-jnp.log(ROPE_THETA) / HALF))
    ang = pcol_ref[0].astype(F32) * inv
    ctab = jnp.where(lane < HEAD_DIM, jnp.cos(ang), 0.0)
    stab = jnp.where(lane < HEAD_DIM, jnp.sin(ang), 0.0)
    key = pl.program_id(1) * tm + lax.broadcasted_iota(I32, (tm, 1), 0)
    blk = (key // SLC_BLOCK) % BIAS_ROWS
    onehot = jnp.where(lane == HEAD_DIM + blk, 1.0, 0.0)
    for g in range(N_KV_GROUPS):
        for slot, ref in ((0, ks_ref), (1, kw_ref)):
            lo = 2 * gd + (slot * N_KV_GROUPS + g) * LANES
            t = nat[:, lo:lo + LANES]
            roped = t * ctab + pltpu.roll(t, HEAD_DIM, 1) * stab
            if slot == 0:
                roped = roped + onehot
            ref[0, g] = roped.astype(BF16)

    vt = _dot_nt(wvt_ref[...], hkv)
    ones_rows = _ones_row_block((N_KV_GROUPS, V_ROWS - HEAD_DIM, tm))
    vs3 = jnp.concatenate([vt[:gd].reshape(N_KV_GROUPS, HEAD_DIM, tm), ones_rows], axis=1).astype(BF16)
    vw3 = jnp.concatenate([vt[gd:].reshape(N_KV_GROUPS, HEAD_DIM, tm), ones_rows], axis=1).astype(BF16)
    vst_ref[0, :, 0] = vs3
    for w in range(tm // Q_BLOCK):
        vwt_ref[0, :, w] = vw3[:, :, w * Q_BLOCK:(w + 1) * Q_BLOCK]

    qt = _dot_nt(wqt_ref[...], hq)
    scale = HEAD_DIM ** -0.5 * LOG2E
    q3 = qt[:qd].reshape(N_HEADS, HEAD_DIM, tm)
    qc_ref[0] = (qt[:qd] * scale).astype(BF16)
    frq = lax.broadcasted_iota(I32, (HALF, 1), 0).astype(F32)
    ang_t = jnp.exp(frq * (-jnp.log(ROPE_THETA) / HALF)) * prow_ref[0].astype(F32)
    cos_t = jnp.cos(ang_t)[None]
    sin_t = jnp.sin(ang_t)[None]
    t1 = q3[:, :HALF]
    t2 = q3[:, HALF:]
    qr = jnp.concatenate([t1 * cos_t - t2 * sin_t, t2 * cos_t + t1 * sin_t], axis=1)
    qr_ref[0] = (qr * scale).reshape(qd, tm).astype(BF16)
    gt_ref[0] = jax.nn.sigmoid(qt[qd:])


def _rot_half_cols(w):
    return jnp.concatenate([-w[:, HALF:], w[:, :HALF]], axis=1)


def _projections(x, positions, g_kv, g_q, w_kv, w_q, tm=KEY_TILE):
    b, s, d = x.shape
    gd = N_KV_GROUPS * HEAD_DIM
    qd = N_HEADS * HEAD_DIM
    slot = lambda i: w_kv[:, i * gd:(i + 1) * gd]
    aug = []
    for sl in (2, 4):
        for g in range(N_KV_GROUPS):
            wk = slot(sl)[:, g * HEAD_DIM:(g + 1) * HEAD_DIM]
            aug += [wk, _rot_half_cols(wk)]
    w_nat = jnp.concatenate([slot(0), slot(1)] + aug, axis=1).astype(BF16)
    w_vt = jnp.concatenate([slot(3), slot(5)], axis=1).T.astype(BF16)
    gate_cols = w_q[:, qd:].reshape(d, N_KV_GROUPS, HEADS_PER_GROUP * N_BRANCH)
    gate_cols = jnp.pad(gate_cols, ((0, 0), (0, 0), (0, 16 - HEADS_PER_GROUP * N_BRANCH))).reshape(d, 64)
    w_qt = jnp.concatenate([w_q[:, :qd], gate_cols], axis=1).T.astype(BF16)
    nt = s // tm
    n_nat = w_nat.shape[1]
    const = lambda shape: pl.BlockSpec(shape, lambda i, j: (0,) * len(shape))
    return pl.pallas_call(
        _proj_kernel,
        grid=(b, nt),
        in_specs=[
            pl.BlockSpec((1, tm, d), lambda i, j: (i, j, 0)),
            pl.BlockSpec((1, 1, tm), lambda i, j: (i, 0, j)),
            pl.BlockSpec((1, tm, 1), lambda i, j: (i, j, 0)),
            const((1, d)), const((1, d)), const((d, n_nat)), const((2 * gd, d)), const((qd + 64, d)),
        ],
        out_specs=[
            pl.BlockSpec((1, tm, 2 * gd), lambda i, j: (i, j, 0)),
            pl.BlockSpec((1, N_KV_GROUPS, tm, LANES), lambda i, j: (i, 0, j, 0)),
            pl.BlockSpec((1, N_KV_GROUPS, tm, LANES), lambda i, j: (i, 0, j, 0)),
            pl.BlockSpec((1, N_KV_GROUPS, 1, V_ROWS, tm), lambda i, j: (i, 0, j, 0, 0)),
            pl.BlockSpec((1, N_KV_GROUPS, tm // Q_BLOCK, V_ROWS, Q_BLOCK), lambda i, j: (i, 0, j, 0, 0)),
            pl.BlockSpec((1, qd, tm), lambda i, j: (i, 0, j)),
            pl.BlockSpec((1, qd, tm), lambda i, j: (i, 0, j)),
            pl.BlockSpec((1, 64, tm), lambda i, j: (i, 0, j)),
        ],
        out_shape=[
            jax.ShapeDtypeStruct((b, s, 2 * gd), F32),
            jax.ShapeDtypeStruct((b, N_KV_GROUPS, s, LANES), BF16),
            jax.ShapeDtypeStruct((b, N_KV_GROUPS, s, LANES), BF16),
            jax.ShapeDtypeStruct((b, N_KV_GROUPS, nt, V_ROWS, tm), BF16),
            jax.ShapeDtypeStruct((b, N_KV_GROUPS, s // Q_BLOCK, V_ROWS, Q_BLOCK), BF16),
            jax.ShapeDtypeStruct((b, qd, s), BF16),
            jax.ShapeDtypeStruct((b, qd, s), BF16),
            jax.ShapeDtypeStruct((b, 64, s), F32),
        ],
        compiler_params=_cparams("arbitrary", "arbitrary"),
        name="nsa_projections",
    )(x, positions.reshape(b, 1, s), positions.reshape(b, s, 1), g_kv.reshape(1, d), g_q.reshape(1, d),
      w_nat, w_vt, w_qt)


def _compress_kernel(a_ref, pos_ref, w1_ref, w2_ref, o_ref, *, transposed_out):
    half = w1_ref.shape[0] // 2
    a = a_ref[0, 0]
    top = (a + pos_ref[0:1, :]).astype(BF16)
    bot = (a + pos_ref[1:2, :]).astype(BF16)
    p = _dot(top, w1_ref[:half, :])
    q = _dot(bot, w1_ref[half:, :])
    n = a.shape[0]
    hidden = p + pltpu.roll(q, n - 1, 0)
    act = jax.nn.gelu(hidden, approximate=True).astype(BF16)
    if transposed_out:
        v_t = _dot_nt(w2_ref[...], act)
        o_ref[0, 0] = jnp.concatenate([v_t, _ones_row_block((V_ROWS - HEAD_DIM, n))], axis=0).astype(BF16)
    else:
        o_ref[0, 0] = _dot(act, w2_ref[...]).astype(BF16)


def _compress(a, pos, w1, w2, transposed_out):
    b, g, n, width = a.shape
    hidden = w1.shape[1]
    pos2 = pos.reshape(2, width)
    if transposed_out:
        w2p = w2.T.astype(BF16)
        out_block, out_shape = (1, 1, V_ROWS, n), (b, g, V_ROWS, n)
    else:
        w2p = jnp.pad(w2, ((0, 0), (0, LANES - HEAD_DIM))).astype(BF16)
        out_block, out_shape = (1, 1, n, LANES), (b, g, n, LANES)
    return pl.pallas_call(
        functools.partial(_compress_kernel, transposed_out=transposed_out),
        grid=(b, g),
        in_specs=[
            pl.BlockSpec((1, 1, n, width), lambda i, j: (i, j, 0, 0)),
            pl.BlockSpec((2, width), lambda i, j: (0, 0)),
            pl.BlockSpec((2 * width, hidden), lambda i, j: (0, 0)),
            pl.BlockSpec(w2p.shape, lambda i, j: (0, 0)),
        ],
        out_specs=pl.BlockSpec(out_block, lambda i, j: (i, j, 0, 0)),
        out_shape=jax.ShapeDtypeStruct(out_shape, BF16),
        compiler_params=_cparams("arbitrary", "arbitrary"),
        name="nsa_compress_v" if transposed_out else "nsa_compress_k",
    )(a, pos2, w1.astype(BF16), w2p)


def _heads_on_lanes(ref):
    return jnp.concatenate([ref[0, h * HEAD_DIM:(h + 1) * HEAD_DIM, :] for h in range(HEADS_PER_GROUP)], axis=1)


def _nsa_kernel(qc_ref, qr_ref, g_ref, kc_ref, lc_ref, ks_ref, vst_ref, kw_ref, vwt_ref,
                o_ref, bias_ref, score_ref, sc_ref, ec_ref, sa_ref, sb_ref, pa_ref, pb_ref):
    i = pl.program_id(2)
    qb = Q_BLOCK
    width = HEADS_PER_GROUP * qb
    n_cmp = kc_ref.shape[2]
    n_slc = score_ref.shape[0]
    t_q = i * qb + lax.broadcasted_iota(I32, (1, qb), 1)
    t_row = jnp.concatenate([t_q] * HEADS_PER_GROUP, axis=1)
    zeros_pad = jnp.zeros((LANES - HEAD_DIM, width), BF16)

    cc = min(CMP_CHUNK, n_cmp)
    n_cc = jnp.minimum((i * (qb // CMP_STRIDE) + qb // CMP_STRIDE + cc - 1) // cc, n_cmp // cc)
    qc_t = jnp.concatenate([_heads_on_lanes(qc_ref), zeros_pad], axis=0)

    @pl.when(i == 0)
    def _():
        ec_ref[...] = jnp.zeros_like(ec_ref)

    def cmp_scores(ch, m):
        r0 = pl.multiple_of(ch * cc, cc)
        s = _dot(kc_ref[0, 0, pl.ds(r0, cc), :], qc_t)
        cmp_end = (r0 + lax.broadcasted_iota(I32, (cc, 1), 0)) * CMP_STRIDE + (CMP_BLOCK - 1)
        s = jnp.where(cmp_end <= t_row, s, NEG)
        sc_ref[pl.ds(r0, cc), :] = s
        return jnp.maximum(m, jnp.max(s, axis=0, keepdims=True))

    m_c = lax.fori_loop(0, n_cc, cmp_scores, jnp.full((1, width), NEG, F32))
    m_c = jnp.maximum(m_c, NEG * 1e-10)

    def cmp_exp(ch, carry):
        r0 = pl.multiple_of(ch * cc, cc)
        ec_ref[pl.ds(r0, cc), :] = jnp.exp2(sc_ref[pl.ds(r0, cc), :] - m_c).astype(BF16)
        return carry

    lax.fori_loop(0, n_cc, cmp_exp, 0)
    res = _dot(lc_ref[0, 0], ec_ref[...])
    rinv = 1.0 / jnp.maximum(res[HEAD_DIM:HEAD_DIM + 1], 1e-30)
    o_cmp = res[:HEAD_DIM] * rinv
    u = res[V_ROWS:] * rinv
    imp = u[:, 0:qb]
    for h in range(1, HEADS_PER_GROUP):
        imp = imp + u[:, h * qb:(h + 1) * qb]

    j_idx = lax.broadcasted_iota(I32, (n_slc, 1), 0)
    cur = t_q // SLC_BLOCK
    valid = j_idx * SLC_BLOCK <= t_q
    forced = (j_idx == 0) | (j_idx == cur) | (j_idx == cur - 1)
    score_ref[...] = jnp.where(valid & jnp.logical_not(forced), imp, -1.0)

    def pick(_, carry, rows):
        sc = score_ref[0:rows, :]
        best = jnp.max(sc, axis=0, keepdims=True)
        idx = jnp.min(jnp.where(sc == best, j_idx[0:rows], n_slc), axis=0, keepdims=True)
        score_ref[0:rows, :] = jnp.where(j_idx[0:rows] == idx, PICKED, sc)
        return carry

    pick_rows = min(PICK_ROWS, n_slc)
    need = (i * qb + qb - 1) // (SLC_BLOCK * pick_rows)
    for r in range(n_slc // pick_rows):
        @pl.when(need == r)
        def _(rows=(r + 1) * pick_rows):
            lax.fori_loop(0, N_SELECT - N_FORCED, functools.partial(pick, rows=rows), 0)
    bias = jnp.where(forced | (score_ref[...] == PICKED), 0.0, NEG).astype(BF16)
    bias_ref[...] = jnp.concatenate([bias] * HEADS_PER_GROUP, axis=1)

    qr_heads = _heads_on_lanes(qr_ref)
    zeros_sel = jnp.zeros((LANES - HEAD_DIM - BIAS_ROWS, width), BF16)
    tk = KEY_TILE
    blocks_per_tile = tk // SLC_BLOCK

    def scores(kt, s_ref, masked):
        b0 = pl.multiple_of((kt * blocks_per_tile) // BIAS_ROWS * BIAS_ROWS, BIAS_ROWS)
        q_aug = jnp.concatenate([qr_heads, bias_ref[pl.ds(b0, BIAS_ROWS), :], zeros_sel], axis=0)
        k0 = pl.multiple_of(kt * tk, tk)
        s = _dot(ks_ref[0, 0, pl.ds(k0, tk), :], q_aug)
        if masked:
            s = jnp.where(k0 + lax.broadcasted_iota(I32, (tk, 1), 0) <= t_row, s, NEG)
        s_ref[...] = s
        part = jnp.max(s.reshape(tk // MAX_ROWS, MAX_ROWS, width), axis=0)
        return jnp.max(part, axis=0, keepdims=True)

    def values(kt, p_ref):
        return _dot(vst_ref[0, 0, jnp.maximum(kt, 0)], p_ref[...])

    def trip(kk, carry, next_scores):
        m, acc, beta, cmax_a, cmax_b = carry
        kt = 2 * kk
        live = jnp.where(kk > 0, 1.0, 0.0)
        acc = acc + live * (beta * values(kt - 2, pa_ref) + values(kt - 1, pb_ref))
        m1 = jnp.maximum(m, cmax_a)
        pa_ref[...] = jnp.exp2((sa_ref[...] - m1).astype(BF16))
        m2 = jnp.maximum(m1, cmax_b)
        pb_ref[...] = jnp.exp2((sb_ref[...] - m2).astype(BF16))
        beta = jnp.exp2(m1 - m2)
        if next_scores is not None:
            cmax_a = scores(kt + 2, sa_ref, next_scores)
            cmax_b = scores(kt + 3, sb_ref, next_scores)
        return m2, (jnp.exp2(m - m1) * beta) * acc, beta, cmax_a, cmax_b

    @pl.when((pl.program_id(0) == 0) & (pl.program_id(1) == 0) & (i == 0))
    def _():
        pa_ref[...] = jnp.zeros_like(pa_ref)
        pb_ref[...] = jnp.zeros_like(pb_ref)

    n_full = (i * qb) // tk
    n_unmasked = jnp.maximum(n_full // 2 - 1, 0)
    carry = (jnp.full((1, width), NEG, F32), jnp.zeros((V_ROWS, width), F32), jnp.ones((1, width), F32),
             scores(0, sa_ref, True), scores(1, sb_ref, True))
    carry = lax.fori_loop(0, n_unmasked, functools.partial(trip, next_scores=False), carry)
    carry = lax.fori_loop(n_unmasked, n_full // 2, functools.partial(trip, next_scores=True), carry)
    last = n_full // 2
    _, acc_s, beta, _, _ = trip(last, carry, None)
    acc_s = acc_s + beta * values(2 * last, pa_ref) + values(2 * last + 1, pb_ref)
    o_slc = acc_s[:HEAD_DIM] / acc_s[HEAD_DIM:HEAD_DIM + 1]

    qr_t = jnp.concatenate([qr_heads, zeros_pad], axis=0)
    n_win = WINDOW // qb + 1
    s_w = []
    for r in range(n_win):
        j = i - (n_win - 1) + r
        k0 = pl.multiple_of(jnp.maximum(j, 0) * qb, qb)
        s = _dot(kw_ref[0, 0, pl.ds(k0, qb), :], qr_t)
        if r in (0, n_win - 1):
            kpos = j * qb + lax.broadcasted_iota(I32, (qb, 1), 0)
            diff = t_row - kpos
            s = jnp.where((diff >= 0) & (diff < WINDOW) & (kpos >= 0), s, NEG)
        else:
            s = s + jnp.where(j >= 0, 0.0, NEG)
        s_w.append(s)
    m_w = s_w[0].max(axis=0, keepdims=True)
    for r in range(1, n_win):
        m_w = jnp.maximum(m_w, s_w[r].max(axis=0, keepdims=True))
    acc_w = jnp.zeros((V_ROWS, width), F32)
    for r in range(n_win):
        jc = jnp.maximum(i - (n_win - 1) + r, 0)
        acc_w = acc_w + _dot(vwt_ref[0, 0, jc], jnp.exp2((s_w[r] - m_w).astype(BF16)))
    o_win = acc_w[:HEAD_DIM] / acc_w[HEAD_DIM:HEAD_DIM + 1]

    gts = g_ref[0]
    for h in range(HEADS_PER_GROUP):
        sl = slice(h * qb, (h + 1) * qb)
        r0 = h * N_BRANCH
        o = (gts[r0:r0 + 1] * o_cmp[:, sl] + gts[r0 + 1:r0 + 2] * o_slc[:, sl] + gts[r0 + 2:r0 + 3] * o_win[:, sl])
        o_ref[0, h * HEAD_DIM:(h + 1) * HEAD_DIM, :] = o.astype(BF16)


def _nsa_attention(qc_t, qr_t, gates_t, k_cmp, lhs_cmp, k_slc, v_slc_t, k_win, v_win_t):
    b, qd, s = qc_t.shape
    g = N_KV_GROUPS
    n_cmp = k_cmp.shape[2]
    n_slc = lhs_cmp.shape[2] - V_ROWS
    nt, tk = v_slc_t.shape[2], v_slc_t.shape[4]
    rows = HEADS_PER_GROUP * HEAD_DIM
    width = HEADS_PER_GROUP * Q_BLOCK
    qspec = pl.BlockSpec((1, rows, Q_BLOCK), lambda bi, gi, i: (bi, gi, i))
    per_bg = lambda shape: pl.BlockSpec((1, 1) + shape, lambda bi, gi, i: (bi, gi) + (0,) * len(shape))
    return pl.pallas_call(
        _nsa_kernel,
        grid=(b, g, s // Q_BLOCK),
        in_specs=[
            qspec, qspec,
            pl.BlockSpec((1, 16, Q_BLOCK), lambda bi, gi, i: (bi, gi, i)),
            per_bg((n_cmp, LANES)), per_bg((V_ROWS + n_slc, n_cmp)),
            per_bg((s, LANES)), per_bg((nt, V_ROWS, tk)),
            per_bg((s, LANES)), per_bg((s // Q_BLOCK, V_ROWS, Q_BLOCK)),
        ],
        out_specs=qspec,
        out_shape=jax.ShapeDtypeStruct((b, qd, s), BF16),
        scratch_shapes=[
            pltpu.VMEM((n_slc, width), BF16),
            pltpu.VMEM((n_slc, Q_BLOCK), F32),
            pltpu.VMEM((n_cmp, width), F32),
            pltpu.VMEM((n_cmp, width), BF16),
            pltpu.VMEM((tk, width), F32), pltpu.VMEM((tk, width), F32),
            pltpu.VMEM((tk, width), BF16), pltpu.VMEM((tk, width), BF16),
        ],
        compiler_params=_cparams("arbitrary", "arbitrary", "arbitrary"),
        name="nsa_attention",
    )(qc_t, qr_t, gates_t, k_cmp, lhs_cmp, k_slc, v_slc_t, k_win, v_win_t)


def _selection_map_t(n_cmp_pad, n_slc):
    c0 = jnp.arange(n_cmp_pad) * CMP_STRIDE
    j0 = jnp.arange(n_slc) * SLC_BLOCK
    ov = jnp.minimum(c0[None, :] + CMP_BLOCK, j0[:, None] + SLC_BLOCK) - jnp.maximum(c0[None, :], j0[:, None])
    return (jnp.clip(ov, 0).astype(F32) / CMP_BLOCK).astype(BF16)


def _out_proj_kernel(x_ref, ot_ref, w_ref, o_ref):
    o_ref[0] = x_ref[0] + _dot_tn(ot_ref[0], w_ref[...])


def _out_proj(x, o_t, w_out, tm=512):
    b, s, d = x.shape
    qd = o_t.shape[1]
    return pl.pallas_call(
        _out_proj_kernel,
        grid=(b, s // tm),
        in_specs=[
            pl.BlockSpec((1, tm, d), lambda i, j: (i, j, 0)),
            pl.BlockSpec((1, qd, tm), lambda i, j: (i, 0, j)),
            pl.BlockSpec((qd, d), lambda i, j: (0, 0)),
        ],
        out_specs=pl.BlockSpec((1, tm, d), lambda i, j: (i, j, 0)),
        out_shape=jax.ShapeDtypeStruct((b, s, d), F32),
        compiler_params=_cparams("arbitrary", "arbitrary"),
        name="nsa_out_proj",
    )(x, o_t, w_out.astype(BF16))


def _nsa_layer(x, positions, g_kv, g_q, w_kv, cmp_pos_k, cmp_w1_k, cmp_w2_k, cmp_pos_v, cmp_w1_v, cmp_w2_v,
               w_in, w_out):
    b, s, d = x.shape
    g = N_KV_GROUPS
    kcv, k_slc, k_win, v_slc_t, v_win_t, qc_t, qr_t, gates_t = _projections(x, positions, g_kv, g_q, w_kv, w_in)
    n_half = s // CMP_STRIDE
    halves = kcv.reshape(b, n_half, CMP_STRIDE, 2, g, HEAD_DIM).transpose(0, 3, 4, 1, 2, 5)
    halves = halves.reshape(b, 2, g, n_half, CMP_STRIDE * HEAD_DIM)
    k_cmp = _compress(halves[:, 0], cmp_pos_k, cmp_w1_k, cmp_w2_k, transposed_out=False)
    v_cmp_t = _compress(halves[:, 1], cmp_pos_v, cmp_w1_v, cmp_w2_v, transposed_out=True)
    map_t = _selection_map_t(n_half, s // SLC_BLOCK)
    lhs_cmp = jnp.concatenate([v_cmp_t, jnp.broadcast_to(map_t, (b, g) + map_t.shape)], axis=2)
    o_t = _nsa_attention(qc_t, qr_t, gates_t, k_cmp, lhs_cmp, k_slc, v_slc_t, k_win, v_win_t)
    return _out_proj(x, o_t, w_out)


def kernel(x, positions, norm_mix, norm_ffn, conv_w_in, conv_w, conv_w_out, norm_kv, w_kv, cmp_pos_k, cmp_w1_k, cmp_w2_k, cmp_pos_v, cmp_w1_v, cmp_w2_v, attn_w_in, attn_w_out, router_group_w, router_group_b, router_expert_w, router_expert_b, expert_w_gate, expert_w_up, expert_w_down, norm_final):
    b, s, d = x.shape
    assert s % (2 * KEY_TILE) == 0 and (s // SLC_BLOCK) % BIAS_ROWS == 0 and WINDOW % Q_BLOCK == 0

    def moe(xs, l, final_norm):
        return _hier_moe(xs.reshape(b * s, d), norm_ffn[l], router_group_w[l], router_group_b[l],
                         router_expert_w[l], router_expert_b[l], expert_w_gate, expert_w_up,
                         expert_w_down, l, norm_final, final_norm).reshape(b, s, d)

    x = _mixer(x, norm_mix[0], conv_w_in[0], conv_w[0], conv_w_out[0])
    x = moe(x, 0, False)
    x = _nsa_layer(x, positions, norm_kv, norm_mix[1], w_kv, cmp_pos_k, cmp_w1_k, cmp_w2_k,
                   cmp_pos_v, cmp_w1_v, cmp_w2_v, attn_w_in[0], attn_w_out[0])
    return moe(x, 1, True)
```

```python
import functools

import jax
import jax.numpy as jnp
from jax import lax
from jax.experimental import pallas as pl
from jax.experimental.pallas import tpu as pltpu

F32 = jnp.float32
BF16 = jnp.bfloat16
I32 = jnp.int32

RMS_EPS = 1e-6
ROPE_THETA = 10000.0
HEAD_DIM = 64
HALF = HEAD_DIM // 2
N_HEADS = 16
N_KV_GROUPS = 4
HEADS_PER_GROUP = N_HEADS // N_KV_GROUPS
N_BRANCH = 3
CMP_BLOCK = 32
CMP_STRIDE = 16
SLC_BLOCK = 64
N_SELECT = 16
WINDOW = 512
Q_BLOCK = 256
FORCE_BONUS = 1000.0
N_GROUPS = 4
EXPERTS_PER_GROUP = 8
N_EXPERTS = N_GROUPS * EXPERTS_PER_GROUP
CHUNK_ROWS = 256
NEG = -1e30
LANES = 128
KEY_TILE = 512
BIAS_ROWS = 16
MAX_ROWS = 64
CMP_CHUNK = 256
N_FORCED = 3
PICK_ROWS = 64
PICKED = -2.0
V_ROWS = 80
LOG2E = 1.4426950408889634
VMEM_LIMIT = 56 * 1024 * 1024


def _cparams(*sem):
    return pltpu.CompilerParams(dimension_semantics=sem, vmem_limit_bytes=VMEM_LIMIT)


def _rms_scale(x):
    return x * lax.rsqrt(jnp.mean(x * x, axis=-1, keepdims=True) + RMS_EPS)


def _dot(a, b):
    return jnp.dot(a, b, preferred_element_type=F32)


def _dot_nt(a, b):
    return lax.dot_general(a, b, (((1,), (1,)), ((), ())), preferred_element_type=F32)


def _dot_tn(a, b):
    return lax.dot_general(a, b, (((0,), (0,)), ((), ())), preferred_element_type=F32)


def _ones_row_block(shape):
    row = lax.broadcasted_iota(I32, shape, len(shape) - 2)
    return jnp.where(row == 0, 1.0, 0.0).astype(F32)


def _mixer_kernel(x_ref, g_ref, win_ref, cw_ref, wout_ref, o_ref, tail_ref):
    d = x_ref.shape[-1]
    tm = x_ref.shape[1]

    @pl.when(pl.program_id(1) == 0)
    def _():
        tail_ref[...] = jnp.zeros_like(tail_ref)

    x = x_ref[0]
    h = (_rms_scale(x) * g_ref[...]).astype(BF16)
    proj = _dot(h, win_ref[...])
    b_gate = proj[:, :d]
    u = proj[:, d:2 * d] * proj[:, 2 * d:]
    row = lax.broadcasted_iota(I32, (tm, 1), 0)
    prev1 = tail_ref[7:8, :]
    prev2 = tail_ref[6:7, :]
    u1 = jnp.where(row >= 1, pltpu.roll(u, 1, 0), prev1)
    u2 = jnp.where(row >= 2, pltpu.roll(u, 2, 0), jnp.where(row == 1, prev1, prev2))
    y = cw_ref[2:3, :] * u + cw_ref[1:2, :] * u1 + cw_ref[0:1, :] * u2
    tail_ref[...] = u[tm - 8:, :]
    z = (b_gate * y).astype(BF16)
    o_ref[0] = x + _dot(z, wout_ref[...])


def _mixer(x, g, w_in, conv_w, w_out, tm=512):
    b, s, d = x.shape
    return pl.pallas_call(
        _mixer_kernel,
        grid=(b, s // tm),
        in_specs=[
            pl.BlockSpec((1, tm, d), lambda i, j: (i, j, 0)),
            pl.BlockSpec((1, d), lambda i, j: (0, 0)),
            pl.BlockSpec((d, 3 * d), lambda i, j: (0, 0)),
            pl.BlockSpec((3, d), lambda i, j: (0, 0)),
            pl.BlockSpec((d, d), lambda i, j: (0, 0)),
        ],
        out_specs=pl.BlockSpec((1, tm, d), lambda i, j: (i, j, 0)),
        out_shape=jax.ShapeDtypeStruct((b, s, d), F32),
        scratch_shapes=[pltpu.VMEM((8, d), F32)],
        compiler_params=_cparams("arbitrary", "arbitrary"),
        name="conv_mixer",
    )(x, g.reshape(1, d), w_in.astype(BF16), conv_w, w_out.astype(BF16))


def _router_kernel(x_ref, g_ref, wt_ref, b_ref, hb_ref, eid_ref, wts_ref, counts_ref, cnt_ref):
    tm = x_ref.shape[0]
    h = _rms_scale(x_ref[...]) * g_ref[...]
    bits = lax.bitcast_convert_type(h.astype(BF16).astype(F32), jnp.uint32)
    half = h.shape[1] // 2
    hb_ref[...] = (bits[:, :half] >> 16) | (bits[:, half:] & jnp.uint32(0xFFFF0000))
    lt = lax.dot_general(wt_ref[...], h, (((1,), (1,)), ((), ())),
                         precision=lax.Precision.HIGHEST, preferred_element_type=F32) + b_ref[...]
    r8 = lax.broadcasted_iota(I32, (8, 1), 0)
    gl = jnp.where(r8 < N_GROUPS, lt[0:8], -jnp.inf)
    gmax = jnp.max(gl, axis=0, keepdims=True)
    gtop = jnp.min(jnp.where(gl == gmax, r8, 8), axis=0, keepdims=True)
    p_g = 1.0 / jnp.sum(jnp.exp(gl - gmax), axis=0, keepdims=True)
    chosen = lt[8:16]
    for k in range(1, N_GROUPS):
        chosen = jnp.where(gtop == k, lt[8 + 8 * k:16 + 8 * k], chosen)
    v0 = jnp.max(chosen, axis=0, keepdims=True)
    i0 = jnp.min(jnp.where(chosen == v0, r8, 8), axis=0, keepdims=True)
    rest = jnp.where(r8 == i0, -jnp.inf, chosen)
    v1 = jnp.max(rest, axis=0, keepdims=True)
    i1 = jnp.min(jnp.where(rest == v1, r8, 8), axis=0, keepdims=True)
    e = jnp.exp(v1 - v0)
    w0 = p_g / (1.0 + e)
    w1 = p_g * e / (1.0 + e)
    e0 = gtop * EXPERTS_PER_GROUP + i0
    e1 = gtop * EXPERTS_PER_GROUP + i1
    wts_ref[...] = jnp.where(r8 == 0, w0, jnp.where(r8 == 1, w1, 0.0))

    @pl.when(pl.program_id(0) == 0)
    def _():
        cnt_ref[...] = jnp.zeros_like(cnt_ref)

    rexp = lax.broadcasted_iota(I32, (N_EXPERTS, 1), 0)
    tri = (lax.broadcasted_iota(I32, (tm, tm), 0) <= lax.broadcasted_iota(I32, (tm, tm), 1)).astype(BF16)
    oh0 = rexp == e0
    oh1 = rexp == e1
    cum0 = _dot(oh0.astype(BF16), tri)
    cum1 = _dot(oh1.astype(BF16), tri)
    before0 = cnt_ref[...]
    before1 = before0 + cum0[:, tm - 1:tm]
    rank0 = jnp.sum(jnp.where(oh0, before0 + cum0, 0.0), axis=0, keepdims=True) - 1.0
    rank1 = jnp.sum(jnp.where(oh1, before1 + cum1, 0.0), axis=0, keepdims=True) - 1.0
    cnt_ref[...] = before1 + cum1[:, tm - 1:tm]
    counts_ref[...] = jnp.broadcast_to(cnt_ref[...], counts_ref.shape).astype(I32)
    eid_ref[...] = jnp.where(r8 == 0, e0, jnp.where(r8 == 1, e1, jnp.where(
        r8 == 2, rank0.astype(I32), jnp.where(r8 == 3, rank1.astype(I32), 0))))


def _router(xf, g, w_rg, b_rg, w_re, b_re, tm=512):
    n, d = xf.shape
    rows = 8 + N_EXPERTS
    wt = jnp.zeros((rows, d), F32).at[:N_GROUPS].set(w_rg.T).at[8:].set(w_re.T)
    bt = jnp.zeros((rows, 1), F32).at[:N_GROUPS, 0].set(b_rg).at[8:, 0].set(b_re)
    return pl.pallas_call(
        _router_kernel,
        grid=(n // tm,),
        in_specs=[
            pl.BlockSpec((tm, d), lambda i: (i, 0)),
            pl.BlockSpec((1, d), lambda i: (0, 0)),
            pl.BlockSpec((rows, d), lambda i: (0, 0)),
            pl.BlockSpec((rows, 1), lambda i: (0, 0)),
        ],
        out_specs=[
            pl.BlockSpec((tm, d // 2), lambda i: (i, 0)),
            pl.BlockSpec((8, tm), lambda i: (0, i)),
            pl.BlockSpec((8, tm), lambda i: (0, i)),
            pl.BlockSpec((N_EXPERTS, LANES), lambda i: (0, 0)),
        ],
        out_shape=[
            jax.ShapeDtypeStruct((n, d // 2), jnp.uint32),
            jax.ShapeDtypeStruct((8, n), I32),
            jax.ShapeDtypeStruct((8, n), F32),
            jax.ShapeDtypeStruct((N_EXPERTS, LANES), I32),
        ],
        scratch_shapes=[pltpu.VMEM((N_EXPERTS, 1), F32)],
        compiler_params=_cparams("arbitrary"),
        name="moe_router",
    )(xf, g.reshape(1, d), wt, bt)


def _take_rows(a, idx):
    return a.at[idx].get(mode="promise_in_bounds")


def _dispatch_plan(eid, counts, n_tok):
    a_exp = eid[0:2].reshape(-1)
    rank = eid[2:4].reshape(-1)
    n_asg = a_exp.shape[0]
    a_tok = jnp.arange(n_asg, dtype=I32) % n_tok
    padded = (counts + CHUNK_ROWS - 1) // CHUNK_ROWS * CHUNK_ROWS
    pend = jnp.cumsum(padded)
    pstart = pend - padded
    dest = (_take_rows(pstart, a_exp) + rank).astype(I32)
    n_chunks = -(-n_asg // CHUNK_ROWS) + N_EXPERTS
    row_tok = jnp.zeros((n_chunks * CHUNK_ROWS,), I32).at[dest].set(
        a_tok, unique_indices=True, mode="promise_in_bounds")
    chunk_row0 = jnp.arange(n_chunks, dtype=I32) * CHUNK_ROWS
    chunk_exp = jnp.minimum(jnp.sum((pend[None, :] <= chunk_row0[:, None]).astype(I32), axis=1), N_EXPERTS - 1)
    n_used = (pend[-1] // CHUNK_ROWS).astype(I32).reshape(1)
    return row_tok, dest.reshape(2, n_tok), chunk_exp, n_used, n_chunks


def _expert_kernel(ce_ref, nu_ref, xs_ref, wg_ref, wu_ref, wd_ref, o_ref, wg_bf, wu_bf, wd_bf):
    c = pl.program_id(0)

    @pl.when((c == 0) | (ce_ref[c] != ce_ref[jnp.maximum(c - 1, 0)]))
    def _():
        wg_bf[...] = wg_ref[0, 0].astype(BF16)
        wu_bf[...] = wu_ref[0, 0].astype(BF16)
        wd_bf[...] = wd_ref[0, 0].astype(BF16)

    @pl.when(c < nu_ref[0])
    def _():
        words = xs_ref[...]
        lo = lax.bitcast_convert_type(words << 16, F32)
        hi = lax.bitcast_convert_type(words & jnp.uint32(0xFFFF0000), F32)
        xc = jnp.concatenate([lo, hi], axis=1).astype(BF16)
        gate = _dot(xc, wg_bf[...])
        up = _dot(xc, wu_bf[...])
        act = (gate * jax.nn.sigmoid(gate) * up).astype(BF16)
        o_ref[...] = _dot(act, wd_bf[...])

    @pl.when(c >= nu_ref[0])
    def _():
        o_ref[...] = jnp.zeros_like(o_ref)


def _experts(xs, chunk_exp, n_used, w_gate, w_up, w_down, layer, n_chunks):
    d = w_gate.shape[-2]
    de = w_gate.shape[-1]
    grid_spec = pltpu.PrefetchScalarGridSpec(
        num_scalar_prefetch=2,
        grid=(n_chunks,),
        in_specs=[
            pl.BlockSpec((CHUNK_ROWS, d // 2), lambda c, ce, nu: (c, 0)),
            pl.BlockSpec((1, 1, d, de), lambda c, ce, nu: (layer, ce[c], 0, 0)),
            pl.BlockSpec((1, 1, d, de), lambda c, ce, nu: (layer, ce[c], 0, 0)),
            pl.BlockSpec((1, 1, de, d), lambda c, ce, nu: (layer, ce[c], 0, 0)),
        ],
        out_specs=pl.BlockSpec((CHUNK_ROWS, d), lambda c, ce, nu: (c, 0)),
        scratch_shapes=[pltpu.VMEM((d, de), BF16), pltpu.VMEM((d, de), BF16), pltpu.VMEM((de, d), BF16)],
    )
    return pl.pallas_call(
        _expert_kernel,
        grid_spec=grid_spec,
        out_shape=jax.ShapeDtypeStruct((n_chunks * CHUNK_ROWS, d), F32),
        compiler_params=_cparams("arbitrary"),
        name="moe_experts",
    )(chunk_exp, n_used, xs, w_gate, w_up, w_down)


def _combine_kernel(x_ref, y0_ref, y1_ref, w_ref, g_ref, o_ref, *, final_norm):
    w = w_ref[...]
    out = x_ref[...] + w[:, 0:1] * y0_ref[...] + w[:, 1:2] * y1_ref[...]
    if final_norm:
        out = _rms_scale(out) * g_ref[...]
    o_ref[...] = out


def _combine(xf, y0, y1, wcol, g, final_norm, tm=512):
    n, d = xf.shape
    row = pl.BlockSpec((tm, d), lambda i: (i, 0))
    return pl.pallas_call(
        functools.partial(_combine_kernel, final_norm=final_norm),
        grid=(n // tm,),
        in_specs=[row, row, row, pl.BlockSpec((tm, 8), lambda i: (i, 0)), pl.BlockSpec((1, d), lambda i: (0, 0))],
        out_specs=row,
        out_shape=jax.ShapeDtypeStruct((n, d), F32),
        compiler_params=_cparams("arbitrary"),
        name="moe_combine",
    )(xf, y0, y1, wcol, g.reshape(1, d))


def _hier_moe(xf, g_ffn, w_rg, b_rg, w_re, b_re, w_gate, w_up, w_down, layer, g_final, final_norm):
    n = xf.shape[0]
    hb, eid, wts, counts = _router(xf, g_ffn, w_rg, b_rg, w_re, b_re)
    row_tok, dest, chunk_exp, n_used, n_chunks = _dispatch_plan(eid, counts[:, 0], n)
    xs = _take_rows(hb, row_tok)
    ys = _experts(xs, chunk_exp, n_used, w_gate, w_up, w_down, layer, n_chunks)
    return _combine(xf, _take_rows(ys, dest[0]), _take_rows(ys, dest[1]), wts.T, g_final, final_norm)


def _proj_kernel(x_ref, prow_ref, pcol_ref, gkv_ref, gq_ref, wnat_ref, wvt_ref, wqt_ref,
                 kcv_ref, ks_ref, kw_ref, vst_ref, vwt_ref, qc_ref, qr_ref, gt_ref):
    tm = x_ref.shape[1]
    gd = N_KV_GROUPS * HEAD_DIM
    qd = N_HEADS * HEAD_DIM
    y = _rms_scale(x_ref[0])
    hkv = (y * gkv_ref[...]).astype(BF16)
    hq = (y * gq_ref[...]).astype(BF16)

    nat = _dot(hkv, wnat_ref[...])
    kcv_ref[0] = nat[:, :2 * gd]
    lane = lax.broadcasted_iota(I32, (1, LANES), 1)
    inv = jnp.exp((lane % HALF).astype(F32) * (-jnp.log(ROPE_THETA) / HALF))
    ang = pcol_ref[0].astype(F32) * inv
    ctab = jnp.where(lane < HEAD_DIM, jnp.cos(ang), 0.0)
    stab = jnp.where(lane < HEAD_DIM, jnp.sin(ang), 0.0)
    key = pl.program_id(1) * tm + lax.broadcasted_iota(I32, (tm, 1), 0)
    blk = (key // SLC_BLOCK) % BIAS_ROWS
    onehot = jnp.where(lane == HEAD_DIM + blk, 1.0, 0.0)
    for g in range(N_KV_GROUPS):
        for slot, ref in ((0, ks_ref), (1, kw_ref)):
            lo = 2 * gd + (slot * N_KV_GROUPS + g) * LANES
            t = nat[:, lo:lo + LANES]
            roped = t * ctab + pltpu.roll(t, HEAD_DIM, 1) * stab
            if slot == 0:
                roped = roped + onehot
            ref[0, g] = roped.astype(BF16)

    vt = _dot_nt(wvt_ref[...], hkv)
    ones_rows = _ones_row_block((N_KV_GROUPS, V_ROWS - HEAD_DIM, tm))
    vs3 = jnp.concatenate([vt[:gd].reshape(N_KV_GROUPS, HEAD_DIM, tm), ones_rows], axis=1).astype(BF16)
    vw3 = jnp.concatenate([vt[gd:].reshape(N_KV_GROUPS, HEAD_DIM, tm), ones_rows], axis=1).astype(BF16)
    vst_ref[0, :, 0] = vs3
    for w in range(tm // Q_BLOCK):
        vwt_ref[0, :, w] = vw3[:, :, w * Q_BLOCK:(w + 1) * Q_BLOCK]

    qt = _dot_nt(wqt_ref[...], hq)
    scale = HEAD_DIM ** -0.5 * LOG2E
    q3 = qt[:qd].reshape(N_HEADS, HEAD_DIM, tm)
    qc_ref[0] = (qt[:qd] * scale).astype(BF16)
    frq = lax.broadcasted_iota(I32, (HALF, 1), 0).astype(F32)
    ang_t = jnp.exp(frq * (-jnp.log(ROPE_THETA) / HALF)) * prow_ref[0].astype(F32)
    cos_t = jnp.cos(ang_t)[None]
    sin_t = jnp.sin(ang_t)[None]
    t1 = q3[:, :HALF]
    t2 = q3[:, HALF:]
    qr = jnp.concatenate([t1 * cos_t - t2 * sin_t, t2 * cos_t + t1 * sin_t], axis=1)
    qr_ref[0] = (qr * scale).reshape(qd, tm).astype(BF16)
    gt_ref[0] = jax.nn.sigmoid(qt[qd:])


def _rot_half_cols(w):
    return jnp.concatenate([-w[:, HALF:], w[:, :HALF]], axis=1)


def _projections(x, positions, g_kv, g_q, w_kv, w_q, tm=KEY_TILE):
    b, s, d = x.shape
    gd = N_KV_GROUPS * HEAD_DIM
    qd = N_HEADS * HEAD_DIM
    slot = lambda i: w_kv[:, i * gd:(i + 1) * gd]
    aug = []
    for sl in (2, 4):
        for g in range(N_KV_GROUPS):
            wk = slot(sl)[:, g * HEAD_DIM:(g + 1) * HEAD_DIM]
            aug += [wk, _rot_half_cols(wk)]
    w_nat = jnp.concatenate([slot(0), slot(1)] + aug, axis=1).astype(BF16)
    w_vt = jnp.concatenate([slot(3), slot(5)], axis=1).T.astype(BF16)
    gate_cols = w_q[:, qd:].reshape(d, N_KV_GROUPS, HEADS_PER_GROUP * N_BRANCH)
    gate_cols = jnp.pad(gate_cols, ((0, 0), (0, 0), (0, 16 - HEADS_PER_GROUP * N_BRANCH))).reshape(d, 64)
    w_qt = jnp.concatenate([w_q[:, :qd], gate_cols], axis=1).T.astype(BF16)
    nt = s // tm
    n_nat = w_nat.shape[1]
    const = lambda shape: pl.BlockSpec(shape, lambda i, j: (0,) * len(shape))
    return pl.pallas_call(
        _proj_kernel,
        grid=(b, nt),
        in_specs=[
            pl.BlockSpec((1, tm, d), lambda i, j: (i, j, 0)),
            pl.BlockSpec((1, 1, tm), lambda i, j: (i, 0, j)),
            pl.BlockSpec((1, tm, 1), lambda i, j: (i, j, 0)),
            const((1, d)), const((1, d)), const((d, n_nat)), const((2 * gd, d)), const((qd + 64, d)),
        ],
        out_specs=[
            pl.BlockSpec((1, tm, 2 * gd), lambda i, j: (i, j, 0)),
            pl.BlockSpec((1, N_KV_GROUPS, tm, LANES), lambda i, j: (i, 0, j, 0)),
            pl.BlockSpec((1, N_KV_GROUPS, tm, LANES), lambda i, j: (i, 0, j, 0)),
            pl.BlockSpec((1, N_KV_GROUPS, 1, V_ROWS, tm), lambda i, j: (i, 0, j, 0, 0)),
            pl.BlockSpec((1, N_KV_GROUPS, tm // Q_BLOCK, V_ROWS, Q_BLOCK), lambda i, j: (i, 0, j, 0, 0)),
            pl.BlockSpec((1, qd, tm), lambda i, j: (i, 0, j)),
            pl.BlockSpec((1, qd, tm), lambda i, j: (i, 0, j)),
            pl.BlockSpec((1, 64, tm), lambda i, j: (i, 0, j)),
        ],
        out_shape=[
            jax.ShapeDtypeStruct((b, s, 2 * gd), F32),
            jax.ShapeDtypeStruct((b, N_KV_GROUPS, s, LANES), BF16),
            jax.ShapeDtypeStruct((b, N_KV_GROUPS, s, LANES), BF16),
            jax.ShapeDtypeStruct((b, N_KV_GROUPS, nt, V_ROWS, tm), BF16),
            jax.ShapeDtypeStruct((b, N_KV_GROUPS, s // Q_BLOCK, V_ROWS, Q_BLOCK), BF16),
            jax.ShapeDtypeStruct((b, qd, s), BF16),
            jax.ShapeDtypeStruct((b, qd, s), BF16),
            jax.ShapeDtypeStruct((b, 64, s), F32),
        ],
        compiler_params=_cparams("arbitrary", "arbitrary"),
        name="nsa_projections",
    )(x, positions.reshape(b, 1, s), positions.reshape(b, s, 1), g_kv.reshape(1, d), g_q.reshape(1, d),
      w_nat, w_vt, w_qt)


def _compress_kernel(a_ref, pos_ref, w1_ref, w2_ref, o_ref, *, transposed_out):
    half = w1_ref.shape[0] // 2
    a = a_ref[0, 0]
    top = (a + pos_ref[0:1, :]).astype(BF16)
    bot = (a + pos_ref[1:2, :]).astype(BF16)
    p = _dot(top, w1_ref[:half, :])
    q = _dot(bot, w1_ref[half:, :])
    n = a.shape[0]
    hidden = p + pltpu.roll(q, n - 1, 0)
    act = jax.nn.gelu(hidden, approximate=True).astype(BF16)
    if transposed_out:
        v_t = _dot_nt(w2_ref[...], act)
        o_ref[0, 0] = jnp.concatenate([v_t, _ones_row_block((V_ROWS - HEAD_DIM, n))], axis=0).astype(BF16)
    else:
        o_ref[0, 0] = _dot(act, w2_ref[...]).astype(BF16)


def _compress(a, pos, w1, w2, transposed_out):
    b, g, n, width = a.shape
    hidden = w1.shape[1]
    pos2 = pos.reshape(2, width)
    if transposed_out:
        w2p = w2.T.astype(BF16)
        out_block, out_shape = (1, 1, V_ROWS, n), (b, g, V_ROWS, n)
    else:
        w2p = jnp.pad(w2, ((0, 0), (0, LANES - HEAD_DIM))).astype(BF16)
        out_block, out_shape = (1, 1, n, LANES), (b, g, n, LANES)
    return pl.pallas_call(
        functools.partial(_compress_kernel, transposed_out=transposed_out),
        grid=(b, g),
        in_specs=[
            pl.BlockSpec((1, 1, n, width), lambda i, j: (i, j, 0, 0)),
            pl.BlockSpec((2, width), lambda i, j: (0, 0)),
            pl.BlockSpec((2 * width, hidden), lambda i, j: (0, 0)),
            pl.BlockSpec(w2p.shape, lambda i, j: (0, 0)),
        ],
        out_specs=pl.BlockSpec(out_block, lambda i, j: (i, j, 0, 0)),
        out_shape=jax.ShapeDtypeStruct(out_shape, BF16),
        compiler_params=_cparams("arbitrary", "arbitrary"),
        name="nsa_compress_v" if transposed_out else "nsa_compress_k",
    )(a, pos2, w1.astype(BF16), w2p)


def _heads_on_lanes(ref):
    return jnp.concatenate([ref[0, h * HEAD_DIM:(h + 1) * HEAD_DIM, :] for h in range(HEADS_PER_GROUP)], axis=1)


def _nsa_kernel(qc_ref, qr_ref, g_ref, kc_ref, vct_ref, band_ref, ks_ref, vst_ref, kw_ref, vwt_ref,
                o_ref, bias_ref, score_ref, sa_ref, sb_ref, pa_ref, pb_ref):
    i = pl.program_id(2)
    qb = Q_BLOCK
    width = HEADS_PER_GROUP * qb
    n_cmp = kc_ref.shape[2]
    n_slc = score_ref.shape[0]
    t_q = i * qb + lax.broadcasted_iota(I32, (1, qb), 1)
    t_row = jnp.concatenate([t_q] * HEADS_PER_GROUP, axis=1)
    zeros_pad = jnp.zeros((LANES - HEAD_DIM, width), BF16)

    cc = band_ref.shape[1]
    bpc = cc * CMP_STRIDE // SLC_BLOCK
    qc_t = jnp.concatenate([_heads_on_lanes(qc_ref), zeros_pad], axis=0)
    chunk_scores, maxima = [], []
    for c in range(n_cmp // cc):
        s = _dot(kc_ref[0, 0, c * cc:(c + 1) * cc, :], qc_t)
        cmp_end = (c * cc + lax.broadcasted_iota(I32, (cc, 1), 0)) * CMP_STRIDE + (CMP_BLOCK - 1)
        s = jnp.where(cmp_end <= t_row, s, NEG)
        chunk_scores.append(s)
        maxima.append(jnp.maximum(jnp.max(s, axis=0, keepdims=True), NEG * 1e-10))
    m_fin = functools.reduce(jnp.maximum, maxima)
    shares = []
    for c, (s, mx) in enumerate(zip(chunk_scores, maxima)):
        lhs = jnp.concatenate([vct_ref[0, 0, :, c * cc:(c + 1) * cc], band_ref[...]], axis=0)
        share = _dot(lhs, jnp.exp2(s - mx).astype(BF16))
        shares.append(share * jnp.exp2(mx - m_fin))
    acc_c = shares[0][:V_ROWS]
    for sh in shares[1:]:
        acc_c = acc_c + sh[:V_ROWS]
    rinv = 1.0 / jnp.maximum(acc_c[HEAD_DIM:HEAD_DIM + 1], 1e-30)
    o_cmp = acc_c[:HEAD_DIM] * rinv
    imp_rows = []
    for c, sh in enumerate(shares):
        u = sh[V_ROWS:V_ROWS + bpc]
        if c > 0:
            spill = shares[c - 1][V_ROWS + bpc:V_ROWS + bpc + 8]
            u = jnp.concatenate([u[:8] + spill, u[8:]], axis=0)
        u = u * rinv
        imp_c = u[:, 0:qb]
        for h in range(1, HEADS_PER_GROUP):
            imp_c = imp_c + u[:, h * qb:(h + 1) * qb]
        imp_rows.append(imp_c)
    imp = jnp.concatenate(imp_rows, axis=0)

    j_idx = lax.broadcasted_iota(I32, (n_slc, 1), 0)
    cur = t_q // SLC_BLOCK
    valid = j_idx * SLC_BLOCK <= t_q
    forced = (j_idx == 0) | (j_idx == cur) | (j_idx == cur - 1)
    score_ref[...] = jnp.where(valid & jnp.logical_not(forced), imp, -1.0)

    def pick(_, carry, rows):
        sc = score_ref[0:rows, :]
        best = jnp.max(sc, axis=0, keepdims=True)
        idx = jnp.min(jnp.where(sc == best, j_idx[0:rows], n_slc), axis=0, keepdims=True)
        score_ref[0:rows, :] = jnp.where(j_idx[0:rows] == idx, PICKED, sc)
        return carry

    pick_rows = min(PICK_ROWS, n_slc)
    need = (i * qb + qb - 1) // (SLC_BLOCK * pick_rows)
    for r in range(n_slc // pick_rows):
        @pl.when(need == r)
        def _(rows=(r + 1) * pick_rows):
            lax.fori_loop(0, N_SELECT - N_FORCED, functools.partial(pick, rows=rows), 0)
    bias = jnp.where(forced | (score_ref[...] == PICKED), 0.0, NEG).astype(BF16)
    bias_ref[...] = jnp.concatenate([bias] * HEADS_PER_GROUP, axis=1)

    qr_heads = _heads_on_lanes(qr_ref)
    zeros_sel = jnp.zeros((LANES - HEAD_DIM - BIAS_ROWS, width), BF16)
    tk = KEY_TILE
    blocks_per_tile = tk // SLC_BLOCK

    def scores(kt, s_ref, masked):
        b0 = pl.multiple_of((kt * blocks_per_tile) // BIAS_ROWS * BIAS_ROWS, BIAS_ROWS)
        q_aug = jnp.concatenate([qr_heads, bias_ref[pl.ds(b0, BIAS_ROWS), :], zeros_sel], axis=0)
        k0 = pl.multiple_of(kt * tk, tk)
        s = _dot(ks_ref[0, 0, pl.ds(k0, tk), :], q_aug)
        if masked:
            s = jnp.where(k0 + lax.broadcasted_iota(I32, (tk, 1), 0) <= t_row, s, NEG)
        s_ref[...] = s
        part = jnp.max(s.reshape(tk // MAX_ROWS, MAX_ROWS, width), axis=0)
        return jnp.max(part, axis=0, keepdims=True)

    def values(kt, p_ref):
        return _dot(vst_ref[0, 0, jnp.maximum(kt, 0)], p_ref[...])

    def trip(kk, carry, next_scores):
        m, acc, beta, cmax_a, cmax_b = carry
        kt = 2 * kk
        live = jnp.where(kk > 0, 1.0, 0.0)
        acc = acc + live * (beta * values(kt - 2, pa_ref) + values(kt - 1, pb_ref))
        m1 = jnp.maximum(m, cmax_a)
        pa_ref[...] = jnp.exp2((sa_ref[...] - m1).astype(BF16))
        m2 = jnp.maximum(m1, cmax_b)
        pb_ref[...] = jnp.exp2((sb_ref[...] - m2).astype(BF16))
        beta = jnp.exp2(m1 - m2)
        if next_scores is not None:
            cmax_a = scores(kt + 2, sa_ref, next_scores)
            cmax_b = scores(kt + 3, sb_ref, next_scores)
        return m2, (jnp.exp2(m - m1) * beta) * acc, beta, cmax_a, cmax_b

    @pl.when((pl.program_id(0) == 0) & (pl.program_id(1) == 0) & (i == 0))
    def _():
        pa_ref[...] = jnp.zeros_like(pa_ref)
        pb_ref[...] = jnp.zeros_like(pb_ref)

    n_full = (i * qb) // tk
    n_unmasked = jnp.maximum(n_full // 2 - 1, 0)
    carry = (jnp.full((1, width), NEG, F32), jnp.zeros((V_ROWS, width), F32), jnp.ones((1, width), F32),
             scores(0, sa_ref, True), scores(1, sb_ref, True))
    carry = lax.fori_loop(0, n_unmasked, functools.partial(trip, next_scores=False), carry)
    carry = lax.fori_loop(n_unmasked, n_full // 2, functools.partial(trip, next_scores=True), carry)
    qr_t = jnp.concatenate([qr_heads, zeros_pad], axis=0)
    n_win = WINDOW // qb + 1
    s_w = []
    for r in range(n_win):
        j = i - (n_win - 1) + r
        k0 = pl.multiple_of(jnp.maximum(j, 0) * qb, qb)
        s = _dot(kw_ref[0, 0, pl.ds(k0, qb), :], qr_t)
        if r in (0, n_win - 1):
            kpos = j * qb + lax.broadcasted_iota(I32, (qb, 1), 0)
            diff = t_row - kpos
            s = jnp.where((diff >= 0) & (diff < WINDOW) & (kpos >= 0), s, NEG)
        else:
            s = s + jnp.where(j >= 0, 0.0, NEG)
        s_w.append(s)
    m_w = s_w[0].max(axis=0, keepdims=True)
    for r in range(1, n_win):
        m_w = jnp.maximum(m_w, s_w[r].max(axis=0, keepdims=True))

    last = n_full // 2
    _, acc_s, beta, _, _ = trip(last, carry, None)
    acc_s = acc_s + beta * values(2 * last, pa_ref) + values(2 * last + 1, pb_ref)
    o_slc = acc_s[:HEAD_DIM] / acc_s[HEAD_DIM:HEAD_DIM + 1]

    acc_w = jnp.zeros((V_ROWS, width), F32)
    for r in range(n_win):
        jc = jnp.maximum(i - (n_win - 1) + r, 0)
        acc_w = acc_w + _dot(vwt_ref[0, 0, jc], jnp.exp2((s_w[r] - m_w).astype(BF16)))
    o_win = acc_w[:HEAD_DIM] / acc_w[HEAD_DIM:HEAD_DIM + 1]

    gts = g_ref[0]
    for h in range(HEADS_PER_GROUP):
        sl = slice(h * qb, (h + 1) * qb)
        r0 = h * N_BRANCH
        o = (gts[r0:r0 + 1] * o_cmp[:, sl] + gts[r0 + 1:r0 + 2] * o_slc[:, sl] + gts[r0 + 2:r0 + 3] * o_win[:, sl])
        o_ref[0, h * HEAD_DIM:(h + 1) * HEAD_DIM, :] = o.astype(BF16)


def _nsa_attention(qc_t, qr_t, gates_t, k_cmp, v_cmp_t, band, k_slc, v_slc_t, k_win, v_win_t):
    b, qd, s = qc_t.shape
    g = N_KV_GROUPS
    n_cmp = k_cmp.shape[2]
    n_slc = s // SLC_BLOCK
    nt, tk = v_slc_t.shape[2], v_slc_t.shape[4]
    rows = HEADS_PER_GROUP * HEAD_DIM
    width = HEADS_PER_GROUP * Q_BLOCK
    qspec = pl.BlockSpec((1, rows, Q_BLOCK), lambda bi, gi, i: (bi, gi, i))
    per_bg = lambda shape: pl.BlockSpec((1, 1) + shape, lambda bi, gi, i: (bi, gi) + (0,) * len(shape))
    return pl.pallas_call(
        _nsa_kernel,
        grid=(b, g, s // Q_BLOCK),
        in_specs=[
            qspec, qspec,
            pl.BlockSpec((1, 16, Q_BLOCK), lambda bi, gi, i: (bi, gi, i)),
            per_bg((n_cmp, LANES)), per_bg((V_ROWS, n_cmp)),
            pl.BlockSpec(band.shape, lambda bi, gi, i: (0, 0)),
            per_bg((s, LANES)), per_bg((nt, V_ROWS, tk)),
            per_bg((s, LANES)), per_bg((s // Q_BLOCK, V_ROWS, Q_BLOCK)),
        ],
        out_specs=qspec,
        out_shape=jax.ShapeDtypeStruct((b, qd, s), BF16),
        scratch_shapes=[
            pltpu.VMEM((n_slc, width), BF16),
            pltpu.VMEM((n_slc, Q_BLOCK), F32),
            pltpu.VMEM((tk, width), F32), pltpu.VMEM((tk, width), F32),
            pltpu.VMEM((tk, width), BF16), pltpu.VMEM((tk, width), BF16),
        ],
        compiler_params=_cparams("arbitrary", "arbitrary", "arbitrary"),
        name="nsa_attention",
    )(qc_t, qr_t, gates_t, k_cmp, v_cmp_t, band, k_slc, v_slc_t, k_win, v_win_t)


def _selection_band(n_cmp):
    cc = min(CMP_CHUNK, n_cmp)
    bpc = cc * CMP_STRIDE // SLC_BLOCK
    c0 = jnp.arange(cc) * CMP_STRIDE
    j0 = jnp.arange(bpc + 16) * SLC_BLOCK
    ov = jnp.minimum(c0[None, :] + CMP_BLOCK, j0[:, None] + SLC_BLOCK) - jnp.maximum(c0[None, :], j0[:, None])
    return (jnp.clip(ov, 0).astype(F32) / CMP_BLOCK).astype(BF16)


def _out_proj_kernel(x_ref, ot_ref, w_ref, o_ref):
    o_ref[0] = x_ref[0] + _dot_tn(ot_ref[0], w_ref[...])


def _out_proj(x, o_t, w_out, tm=512):
    b, s, d = x.shape
    qd = o_t.shape[1]
    return pl.pallas_call(
        _out_proj_kernel,
        grid=(b, s // tm),
        in_specs=[
            pl.BlockSpec((1, tm, d), lambda i, j: (i, j, 0)),
            pl.BlockSpec((1, qd, tm), lambda i, j: (i, 0, j)),
            pl.BlockSpec((qd, d), lambda i, j: (0, 0)),
        ],
        out_specs=pl.BlockSpec((1, tm, d), lambda i, j: (i, j, 0)),
        out_shape=jax.ShapeDtypeStruct((b, s, d), F32),
        compiler_params=_cparams("arbitrary", "arbitrary"),
        name="nsa_out_proj",
    )(x, o_t, w_out.astype(BF16))


def _nsa_layer(x, positions, g_kv, g_q, w_kv, cmp_pos_k, cmp_w1_k, cmp_w2_k, cmp_pos_v, cmp_w1_v, cmp_w2_v,
               w_in, w_out):
    b, s, d = x.shape
    g = N_KV_GROUPS
    kcv, k_slc, k_win, v_slc_t, v_win_t, qc_t, qr_t, gates_t = _projections(x, positions, g_kv, g_q, w_kv, w_in)
    n_half = s // CMP_STRIDE
    halves = kcv.reshape(b, n_half, CMP_STRIDE, 2, g, HEAD_DIM).transpose(0, 3, 4, 1, 2, 5)
    halves = halves.reshape(b, 2, g, n_half, CMP_STRIDE * HEAD_DIM)
    k_cmp = _compress(halves[:, 0], cmp_pos_k, cmp_w1_k, cmp_w2_k, transposed_out=False)
    v_cmp_t = _compress(halves[:, 1], cmp_pos_v, cmp_w1_v, cmp_w2_v, transposed_out=True)
    o_t = _nsa_attention(qc_t, qr_t, gates_t, k_cmp, v_cmp_t, _selection_band(n_half), k_slc, v_slc_t, k_win,
                         v_win_t)
    return _out_proj(x, o_t, w_out)


def kernel(x, positions, norm_mix, norm_ffn, conv_w_in, conv_w, conv_w_out, norm_kv, w_kv, cmp_pos_k, cmp_w1_k, cmp_w2_k, cmp_pos_v, cmp_w1_v, cmp_w2_v, attn_w_in, attn_w_out, router_group_w, router_group_b, router_expert_w, router_expert_b, expert_w_gate, expert_w_up, expert_w_down, norm_final):
    b, s, d = x.shape
    assert s % (2 * KEY_TILE) == 0 and (s // SLC_BLOCK) % BIAS_ROWS == 0 and WINDOW % Q_BLOCK == 0

    def moe(xs, l, final_norm):
        return _hier_moe(xs.reshape(b * s, d), norm_ffn[l], router_group_w[l], router_group_b[l],
                         router_expert_w[l], router_expert_b[l], expert_w_gate, expert_w_up,
                         expert_w_down, l, norm_final, final_norm).reshape(b, s, d)

    x = _mixer(x, norm_mix[0], conv_w_in[0], conv_w[0], conv_w_out[0])
    x = moe(x, 0, False)
    x = _nsa_layer(x, positions, norm_kv, norm_mix[1], w_kv, cmp_pos_k, cmp_w1_k, cmp_w2_k,
                   cmp_pos_v, cmp_w1_v, cmp_w2_v, attn_w_in[0], attn_w_out[0])
    return moe(x, 1, True)
```

```python
import functools

import jax
import jax.numpy as jnp
from jax import lax
from jax.experimental import pallas as pl
from jax.experimental.pallas import tpu as pltpu

F32 = jnp.float32
BF16 = jnp.bfloat16
I32 = jnp.int32

RMS_EPS = 1e-6
ROPE_THETA = 10000.0
HEAD_DIM = 64
HALF = HEAD_DIM // 2
N_HEADS = 16
N_KV_GROUPS = 4
HEADS_PER_GROUP = N_HEADS // N_KV_GROUPS
N_BRANCH = 3
CMP_BLOCK = 32
CMP_STRIDE = 16
SLC_BLOCK = 64
N_SELECT = 16
WINDOW = 512
Q_BLOCK = 256
FORCE_BONUS = 1000.0
N_GROUPS = 4
EXPERTS_PER_GROUP = 8
N_EXPERTS = N_GROUPS * EXPERTS_PER_GROUP
CHUNK_ROWS = 256
MOE_SEGMENTS = 4
NEG = -1e30
LANES = 128
KEY_TILE = 512
BIAS_ROWS = 16
MAX_ROWS = 64
CMP_CHUNK = 256
N_FORCED = 3
PICK_ROWS = 64
PICKED = -2.0
V_ROWS = 80
LOG2E = 1.4426950408889634
VMEM_LIMIT = 56 * 1024 * 1024


def _cparams(*sem):
    return pltpu.CompilerParams(dimension_semantics=sem, vmem_limit_bytes=VMEM_LIMIT)


def _rms_scale(x):
    return x * lax.rsqrt(jnp.mean(x * x, axis=-1, keepdims=True) + RMS_EPS)


def _dot(a, b):
    return jnp.dot(a, b, preferred_element_type=F32)


def _dot_nt(a, b):
    return lax.dot_general(a, b, (((1,), (1,)), ((), ())), preferred_element_type=F32)


def _dot_tn(a, b):
    return lax.dot_general(a, b, (((0,), (0,)), ((), ())), preferred_element_type=F32)


def _ones_row_block(shape):
    row = lax.broadcasted_iota(I32, shape, len(shape) - 2)
    return jnp.where(row == 0, 1.0, 0.0).astype(F32)


def _mixer_kernel(x_ref, g_ref, win_ref, cw_ref, wout_ref, o_ref, tail_ref):
    d = x_ref.shape[-1]
    tm = x_ref.shape[1]

    @pl.when(pl.program_id(1) == 0)
    def _():
        tail_ref[...] = jnp.zeros_like(tail_ref)

    x = x_ref[0]
    h = (_rms_scale(x) * g_ref[...]).astype(BF16)
    proj = _dot(h, win_ref[...])
    b_gate = proj[:, :d]
    u = proj[:, d:2 * d] * proj[:, 2 * d:]
    row = lax.broadcasted_iota(I32, (tm, 1), 0)
    prev1 = tail_ref[7:8, :]
    prev2 = tail_ref[6:7, :]
    u1 = jnp.where(row >= 1, pltpu.roll(u, 1, 0), prev1)
    u2 = jnp.where(row >= 2, pltpu.roll(u, 2, 0), jnp.where(row == 1, prev1, prev2))
    y = cw_ref[2:3, :] * u + cw_ref[1:2, :] * u1 + cw_ref[0:1, :] * u2
    tail_ref[...] = u[tm - 8:, :]
    z = (b_gate * y).astype(BF16)
    o_ref[0] = x + _dot(z, wout_ref[...])


def _mixer(x, g, w_in, conv_w, w_out, tm=512):
    b, s, d = x.shape
    return pl.pallas_call(
        _mixer_kernel,
        grid=(b, s // tm),
        in_specs=[
            pl.BlockSpec((1, tm, d), lambda i, j: (i, j, 0)),
            pl.BlockSpec((1, d), lambda i, j: (0, 0)),
            pl.BlockSpec((d, 3 * d), lambda i, j: (0, 0)),
            pl.BlockSpec((3, d), lambda i, j: (0, 0)),
            pl.BlockSpec((d, d), lambda i, j: (0, 0)),
        ],
        out_specs=pl.BlockSpec((1, tm, d), lambda i, j: (i, j, 0)),
        out_shape=jax.ShapeDtypeStruct((b, s, d), F32),
        scratch_shapes=[pltpu.VMEM((8, d), F32)],
        compiler_params=_cparams("arbitrary", "arbitrary"),
        name="conv_mixer",
    )(x, g.reshape(1, d), w_in.astype(BF16), conv_w, w_out.astype(BF16))


def _router_kernel(x_ref, g_ref, wt_ref, b_ref, hb_ref, eid_ref, wts_ref, counts_ref, cnt_ref):
    tm = x_ref.shape[0]
    h = _rms_scale(x_ref[...]) * g_ref[...]
    bits = lax.bitcast_convert_type(h.astype(BF16).astype(F32), jnp.uint32)
    half = h.shape[1] // 2
    hb_ref[...] = (bits[:, :half] >> 16) | (bits[:, half:] & jnp.uint32(0xFFFF0000))
    w = wt_ref[...]
    w_hi = w.astype(BF16)
    w_lo = (w - w_hi.astype(F32)).astype(BF16)
    h_hi = h.astype(BF16)
    h_lo = (h - h_hi.astype(F32)).astype(BF16)
    lt = _dot_nt(w_hi, h_hi) + _dot_nt(w_hi, h_lo) + _dot_nt(w_lo, h_hi) + b_ref[...]
    r8 = lax.broadcasted_iota(I32, (8, 1), 0)
    gl = jnp.where(r8 < N_GROUPS, lt[0:8], -jnp.inf)
    gmax = jnp.max(gl, axis=0, keepdims=True)
    gtop = jnp.min(jnp.where(gl == gmax, r8, 8), axis=0, keepdims=True)
    p_g = 1.0 / jnp.sum(jnp.exp(gl - gmax), axis=0, keepdims=True)
    chosen = lt[8:16]
    for k in range(1, N_GROUPS):
        chosen = jnp.where(gtop == k, lt[8 + 8 * k:16 + 8 * k], chosen)
    v0 = jnp.max(chosen, axis=0, keepdims=True)
    i0 = jnp.min(jnp.where(chosen == v0, r8, 8), axis=0, keepdims=True)
    rest = jnp.where(r8 == i0, -jnp.inf, chosen)
    v1 = jnp.max(rest, axis=0, keepdims=True)
    i1 = jnp.min(jnp.where(rest == v1, r8, 8), axis=0, keepdims=True)
    e = jnp.exp(v1 - v0)
    w0 = p_g / (1.0 + e)
    w1 = p_g * e / (1.0 + e)
    e0 = gtop * EXPERTS_PER_GROUP + i0
    e1 = gtop * EXPERTS_PER_GROUP + i1
    wts_ref[...] = jnp.where(r8 == 0, w0, jnp.where(r8 == 1, w1, 0.0))

    @pl.when(pl.program_id(0) == 0)
    def _():
        cnt_ref[...] = jnp.zeros_like(cnt_ref)

    rexp = lax.broadcasted_iota(I32, (N_EXPERTS, 1), 0)
    tri = (lax.broadcasted_iota(I32, (tm, tm), 0) <= lax.broadcasted_iota(I32, (tm, tm), 1)).astype(BF16)
    oh0 = rexp == e0
    oh1 = rexp == e1
    cum0 = _dot(oh0.astype(BF16), tri)
    cum1 = _dot(oh1.astype(BF16), tri)
    before0 = cnt_ref[...]
    before1 = before0 + cum0[:, tm - 1:tm]
    rank0 = jnp.sum(jnp.where(oh0, before0 + cum0, 0.0), axis=0, keepdims=True) - 1.0
    rank1 = jnp.sum(jnp.where(oh1, before1 + cum1, 0.0), axis=0, keepdims=True) - 1.0
    cnt_ref[...] = before1 + cum1[:, tm - 1:tm]
    counts_ref[...] = jnp.broadcast_to(cnt_ref[...], counts_ref.shape).astype(I32)
    eid_ref[...] = jnp.where(r8 == 0, e0, jnp.where(r8 == 1, e1, jnp.where(
        r8 == 2, rank0.astype(I32), jnp.where(r8 == 3, rank1.astype(I32), 0))))


def _router(xf, g, w_rg, b_rg, w_re, b_re, tm=512):
    n, d = xf.shape
    rows = 8 + N_EXPERTS
    wt = jnp.zeros((rows, d), F32).at[:N_GROUPS].set(w_rg.T).at[8:].set(w_re.T)
    bt = jnp.zeros((rows, 1), F32).at[:N_GROUPS, 0].set(b_rg).at[8:, 0].set(b_re)
    return pl.pallas_call(
        _router_kernel,
        grid=(n // tm,),
        in_specs=[
            pl.BlockSpec((tm, d), lambda i: (i, 0)),
            pl.BlockSpec((1, d), lambda i: (0, 0)),
            pl.BlockSpec((rows, d), lambda i: (0, 0)),
            pl.BlockSpec((rows, 1), lambda i: (0, 0)),
        ],
        out_specs=[
            pl.BlockSpec((tm, d // 2), lambda i: (i, 0)),
            pl.BlockSpec((8, tm), lambda i: (0, i)),
            pl.BlockSpec((8, tm), lambda i: (0, i)),
            pl.BlockSpec((N_EXPERTS, LANES), lambda i: (0, 0)),
        ],
        out_shape=[
            jax.ShapeDtypeStruct((n, d // 2), jnp.uint32),
            jax.ShapeDtypeStruct((8, n), I32),
            jax.ShapeDtypeStruct((8, n), F32),
            jax.ShapeDtypeStruct((N_EXPERTS, LANES), I32),
        ],
        scratch_shapes=[pltpu.VMEM((N_EXPERTS, 1), F32)],
        compiler_params=_cparams("arbitrary"),
        name="moe_router",
    )(xf, g.reshape(1, d), wt, bt)


def _take_rows(a, idx):
    return a.at[idx].get(mode="promise_in_bounds")


def _dispatch_plan(eid, counts, n_tok):
    a_exp = eid[0:2].reshape(-1)
    rank = eid[2:4].reshape(-1)
    n_asg = a_exp.shape[0]
    a_tok = jnp.arange(n_asg, dtype=I32) % n_tok
    padded = (counts + CHUNK_ROWS - 1) // CHUNK_ROWS * CHUNK_ROWS
    pend = jnp.cumsum(padded)
    pstart = pend - padded
    dest = (_take_rows(pstart, a_exp) + rank).astype(I32)
    n_chunks = -(-n_asg // CHUNK_ROWS) + N_EXPERTS
    row_tok = jnp.zeros((n_chunks * CHUNK_ROWS,), I32).at[dest].set(
        a_tok, unique_indices=True, mode="promise_in_bounds")
    chunk_row0 = jnp.arange(n_chunks, dtype=I32) * CHUNK_ROWS
    chunk_exp = jnp.minimum(jnp.sum((pend[None, :] <= chunk_row0[:, None]).astype(I32), axis=1), N_EXPERTS - 1)
    n_used = (pend[-1] // CHUNK_ROWS).astype(I32).reshape(1)
    return row_tok, dest.reshape(2, n_tok), chunk_exp, n_used, n_chunks


def _expert_kernel(ce_ref, nu_ref, xs_ref, wg_ref, wu_ref, wd_ref, *rest, first_chunk):
    o_ref, wg_bf, wu_bf, wd_bf = rest[-4:]
    c = pl.program_id(0) + first_chunk

    @pl.when((pl.program_id(0) == 0) | (ce_ref[c] != ce_ref[jnp.maximum(c - 1, 0)]))
    def _():
        wg_bf[...] = wg_ref[0, 0].astype(BF16)
        wu_bf[...] = wu_ref[0, 0].astype(BF16)
        wd_bf[...] = wd_ref[0, 0].astype(BF16)

    @pl.when(c < nu_ref[0])
    def _():
        words = xs_ref[...]
        lo = lax.bitcast_convert_type(words << 16, F32)
        hi = lax.bitcast_convert_type(words & jnp.uint32(0xFFFF0000), F32)
        xc = jnp.concatenate([lo, hi], axis=1).astype(BF16)
        gate = _dot(xc, wg_bf[...])
        up = _dot(xc, wu_bf[...])
        act = (gate * jax.nn.sigmoid(gate) * up).astype(BF16)
        o_ref[...] = _dot(act, wd_bf[...])

    @pl.when(c >= nu_ref[0])
    def _():
        o_ref[...] = jnp.zeros_like(o_ref)


def _experts(xs, chunk_exp, n_used, w_gate, w_up, w_down, layer, n_chunks, first_chunk, ys_prev):
    d = w_gate.shape[-2]
    de = w_gate.shape[-1]
    weight = lambda shape: pl.BlockSpec((1, 1) + shape, lambda c, ce, nu: (layer, ce[c + first_chunk], 0, 0))
    in_specs = [pl.BlockSpec((CHUNK_ROWS, d // 2), lambda c, ce, nu: (c, 0)),
                weight((d, de)), weight((d, de)), weight((de, d))]
    args = [chunk_exp, n_used, xs, w_gate, w_up, w_down]
    aliases = {}
    if ys_prev is not None:
        in_specs.append(pl.BlockSpec(memory_space=pl.ANY))
        aliases = {len(args): 0}
        args.append(ys_prev)
    grid_spec = pltpu.PrefetchScalarGridSpec(
        num_scalar_prefetch=2,
        grid=(xs.shape[0] // CHUNK_ROWS,),
        in_specs=in_specs,
        out_specs=pl.BlockSpec((CHUNK_ROWS, d), lambda c, ce, nu: (c + first_chunk, 0)),
        scratch_shapes=[pltpu.VMEM((d, de), BF16), pltpu.VMEM((d, de), BF16), pltpu.VMEM((de, d), BF16)],
    )
    return pl.pallas_call(
        functools.partial(_expert_kernel, first_chunk=first_chunk),
        grid_spec=grid_spec,
        out_shape=jax.ShapeDtypeStruct((n_chunks * CHUNK_ROWS, d), F32),
        input_output_aliases=aliases,
        compiler_params=_cparams("arbitrary"),
        name="moe_experts",
    )(*args)


def _combine_kernel(x_ref, y0_ref, y1_ref, w_ref, g_ref, o_ref, *, final_norm):
    w = w_ref[...]
    out = x_ref[...] + w[:, 0:1] * y0_ref[...] + w[:, 1:2] * y1_ref[...]
    if final_norm:
        out = _rms_scale(out) * g_ref[...]
    o_ref[...] = out


def _combine(xf, y0, y1, wcol, g, final_norm, tm=512):
    n, d = xf.shape
    row = pl.BlockSpec((tm, d), lambda i: (i, 0))
    return pl.pallas_call(
        functools.partial(_combine_kernel, final_norm=final_norm),
        grid=(n // tm,),
        in_specs=[row, row, row, pl.BlockSpec((tm, 8), lambda i: (i, 0)), pl.BlockSpec((1, d), lambda i: (0, 0))],
        out_specs=row,
        out_shape=jax.ShapeDtypeStruct((n, d), F32),
        compiler_params=_cparams("arbitrary"),
        name="moe_combine",
    )(xf, y0, y1, wcol, g.reshape(1, d))


def _hier_moe(xf, g_ffn, w_rg, b_rg, w_re, b_re, w_gate, w_up, w_down, layer, g_final, final_norm):
    n = xf.shape[0]
    hb, eid, wts, counts = _router(xf, g_ffn, w_rg, b_rg, w_re, b_re)
    row_tok, dest, chunk_exp, n_used, n_chunks = _dispatch_plan(eid, counts[:, 0], n)
    n_seg = MOE_SEGMENTS if n_chunks % MOE_SEGMENTS == 0 else 1
    seg_chunks = n_chunks // n_seg
    ys = None
    for seg in range(n_seg):
        rows = row_tok[seg * seg_chunks * CHUNK_ROWS:(seg + 1) * seg_chunks * CHUNK_ROWS]
        ys = _experts(_take_rows(hb, rows), chunk_exp, n_used, w_gate, w_up, w_down, layer, n_chunks,
                      seg * seg_chunks, ys)
    return _combine(xf, _take_rows(ys, dest[0]), _take_rows(ys, dest[1]), wts.T, g_final, final_norm)


def _proj_kernel(x_ref, prow_ref, pcol_ref, gkv_ref, gq_ref, wnat_ref, wvt_ref, wqt_ref,
                 kcv_ref, ks_ref, kw_ref, vst_ref, vwt_ref, qc_ref, qr_ref, gt_ref):
    tm = x_ref.shape[1]
    gd = N_KV_GROUPS * HEAD_DIM
    qd = N_HEADS * HEAD_DIM
    y = _rms_scale(x_ref[0])
    hkv = (y * gkv_ref[...]).astype(BF16)
    hq = (y * gq_ref[...]).astype(BF16)

    nat = _dot(hkv, wnat_ref[...])
    kcv_ref[0] = nat[:, :2 * gd]
    lane = lax.broadcasted_iota(I32, (1, LANES), 1)
    inv = jnp.exp((lane % HALF).astype(F32) * (-jnp.log(ROPE_THETA) / HALF))
    ang = pcol_ref[0].astype(F32) * inv
    ctab = jnp.where(lane < HEAD_DIM, jnp.cos(ang), 0.0)
    stab = jnp.where(lane < HEAD_DIM, jnp.sin(ang), 0.0)
    key = pl.program_id(1) * tm + lax.broadcasted_iota(I32, (tm, 1), 0)
    blk = (key // SLC_BLOCK) % BIAS_ROWS
    onehot = jnp.where(lane == HEAD_DIM + blk, 1.0, 0.0)
    for g in range(N_KV_GROUPS):
        for slot, ref in ((0, ks_ref), (1, kw_ref)):
            lo = 2 * gd + (slot * N_KV_GROUPS + g) * LANES
            t = nat[:, lo:lo + LANES]
            roped = t * ctab + pltpu.roll(t, HEAD_DIM, 1) * stab
            if slot == 0:
                roped = roped + onehot
            ref[0, g] = roped.astype(BF16)

    vt = _dot_nt(wvt_ref[...], hkv)
    ones_rows = _ones_row_block((N_KV_GROUPS, V_ROWS - HEAD_DIM, tm))
    vs3 = jnp.concatenate([vt[:gd].reshape(N_KV_GROUPS, HEAD_DIM, tm), ones_rows], axis=1).astype(BF16)
    vw3 = jnp.concatenate([vt[gd:].reshape(N_KV_GROUPS, HEAD_DIM, tm), ones_rows], axis=1).astype(BF16)
    vst_ref[0, :, 0] = vs3
    for w in range(tm // Q_BLOCK):
        vwt_ref[0, :, w] = vw3[:, :, w * Q_BLOCK:(w + 1) * Q_BLOCK]

    qt = _dot_nt(wqt_ref[...], hq)
    scale = HEAD_DIM ** -0.5 * LOG2E
    q3 = qt[:qd].reshape(N_HEADS, HEAD_DIM, tm)
    qc_ref[0] = (qt[:qd] * scale).astype(BF16)
    frq = lax.broadcasted_iota(I32, (HALF, 1), 0).astype(F32)
    ang_t = jnp.exp(frq * (-jnp.log(ROPE_THETA) / HALF)) * prow_ref[0].astype(F32)
    cos_t = jnp.cos(ang_t)[None]
    sin_t = jnp.sin(ang_t)[None]
    t1 = q3[:, :HALF]
    t2 = q3[:, HALF:]
    qr = jnp.concatenate([t1 * cos_t - t2 * sin_t, t2 * cos_t + t1 * sin_t], axis=1)
    qr_ref[0] = (qr * scale).reshape(qd, tm).astype(BF16)
    gt_ref[0] = jax.nn.sigmoid(qt[qd:])


def _rot_half_cols(w):
    return jnp.concatenate([-w[:, HALF:], w[:, :HALF]], axis=1)


def _projections(x, positions, g_kv, g_q, w_kv, w_q, tm=KEY_TILE):
    b, s, d = x.shape
    gd = N_KV_GROUPS * HEAD_DIM
    qd = N_HEADS * HEAD_DIM
    slot = lambda i: w_kv[:, i * gd:(i + 1) * gd]
    aug = []
    for sl in (2, 4):
        for g in range(N_KV_GROUPS):
            wk = slot(sl)[:, g * HEAD_DIM:(g + 1) * HEAD_DIM]
            aug += [wk, _rot_half_cols(wk)]
    w_nat = jnp.concatenate([slot(0), slot(1)] + aug, axis=1).astype(BF16)
    w_vt = jnp.concatenate([slot(3), slot(5)], axis=1).T.astype(BF16)
    gate_cols = w_q[:, qd:].reshape(d, N_KV_GROUPS, HEADS_PER_GROUP * N_BRANCH)
    gate_cols = jnp.pad(gate_cols, ((0, 0), (0, 0), (0, 16 - HEADS_PER_GROUP * N_BRANCH))).reshape(d, 64)
    w_qt = jnp.concatenate([w_q[:, :qd], gate_cols], axis=1).T.astype(BF16)
    nt = s // tm
    n_nat = w_nat.shape[1]
    const = lambda shape: pl.BlockSpec(shape, lambda i, j: (0,) * len(shape))
    return pl.pallas_call(
        _proj_kernel,
        grid=(b, nt),
        in_specs=[
            pl.BlockSpec((1, tm, d), lambda i, j: (i, j, 0)),
            pl.BlockSpec((1, 1, tm), lambda i, j: (i, 0, j)),
            pl.BlockSpec((1, tm, 1), lambda i, j: (i, j, 0)),
            const((1, d)), const((1, d)), const((d, n_nat)), const((2 * gd, d)), const((qd + 64, d)),
        ],
        out_specs=[
            pl.BlockSpec((1, tm, 2 * gd), lambda i, j: (i, j, 0)),
            pl.BlockSpec((1, N_KV_GROUPS, tm, LANES), lambda i, j: (i, 0, j, 0)),
            pl.BlockSpec((1, N_KV_GROUPS, tm, LANES), lambda i, j: (i, 0, j, 0)),
            pl.BlockSpec((1, N_KV_GROUPS, 1, V_ROWS, tm), lambda i, j: (i, 0, j, 0, 0)),
            pl.BlockSpec((1, N_KV_GROUPS, tm // Q_BLOCK, V_ROWS, Q_BLOCK), lambda i, j: (i, 0, j, 0, 0)),
            pl.BlockSpec((1, qd, tm), lambda i, j: (i, 0, j)),
            pl.BlockSpec((1, qd, tm), lambda i, j: (i, 0, j)),
            pl.BlockSpec((1, 64, tm), lambda i, j: (i, 0, j)),
        ],
        out_shape=[
            jax.ShapeDtypeStruct((b, s, 2 * gd), F32),
            jax.ShapeDtypeStruct((b, N_KV_GROUPS, s, LANES), BF16),
            jax.ShapeDtypeStruct((b, N_KV_GROUPS, s, LANES), BF16),
            jax.ShapeDtypeStruct((b, N_KV_GROUPS, nt, V_ROWS, tm), BF16),
            jax.ShapeDtypeStruct((b, N_KV_GROUPS, s // Q_BLOCK, V_ROWS, Q_BLOCK), BF16),
            jax.ShapeDtypeStruct((b, qd, s), BF16),
            jax.ShapeDtypeStruct((b, qd, s), BF16),
            jax.ShapeDtypeStruct((b, 64, s), F32),
        ],
        compiler_params=_cparams("arbitrary", "arbitrary"),
        name="nsa_projections",
    )(x, positions.reshape(b, 1, s), positions.reshape(b, s, 1), g_kv.reshape(1, d), g_q.reshape(1, d),
      w_nat, w_vt, w_qt)


def _compress_kernel(a_ref, pos_ref, w1_ref, w2_ref, o_ref, *, transposed_out):
    half = w1_ref.shape[0] // 2
    a = a_ref[0, 0]
    top = (a + pos_ref[0:1, :]).astype(BF16)
    bot = (a + pos_ref[1:2, :]).astype(BF16)
    p = _dot(top, w1_ref[:half, :])
    q = _dot(bot, w1_ref[half:, :])
    n = a.shape[0]
    hidden = p + pltpu.roll(q, n - 1, 0)
    act = jax.nn.gelu(hidden, approximate=True).astype(BF16)
    if transposed_out:
        v_t = _dot_nt(w2_ref[...], act)
        o_ref[0, 0] = jnp.concatenate([v_t, _ones_row_block((V_ROWS - HEAD_DIM, n))], axis=0).astype(BF16)
    else:
        o_ref[0, 0] = _dot(act, w2_ref[...]).astype(BF16)


def _compress(a, pos, w1, w2, transposed_out):
    b, g, n, width = a.shape
    hidden = w1.shape[1]
    pos2 = pos.reshape(2, width)
    if transposed_out:
        w2p = w2.T.astype(BF16)
        out_block, out_shape = (1, 1, V_ROWS, n), (b, g, V_ROWS, n)
    else:
        w2p = jnp.pad(w2, ((0, 0), (0, LANES - HEAD_DIM))).astype(BF16)
        out_block, out_shape = (1, 1, n, LANES), (b, g, n, LANES)
    return pl.pallas_call(
        functools.partial(_compress_kernel, transposed_out=transposed_out),
        grid=(b, g),
        in_specs=[
            pl.BlockSpec((1, 1, n, width), lambda i, j: (i, j, 0, 0)),
            pl.BlockSpec((2, width), lambda i, j: (0, 0)),
            pl.BlockSpec((2 * width, hidden), lambda i, j: (0, 0)),
            pl.BlockSpec(w2p.shape, lambda i, j: (0, 0)),
        ],
        out_specs=pl.BlockSpec(out_block, lambda i, j: (i, j, 0, 0)),
        out_shape=jax.ShapeDtypeStruct(out_shape, BF16),
        compiler_params=_cparams("arbitrary", "arbitrary"),
        name="nsa_compress_v" if transposed_out else "nsa_compress_k",
    )(a, pos2, w1.astype(BF16), w2p)


def _heads_on_lanes(ref):
    return jnp.concatenate([ref[0, h * HEAD_DIM:(h + 1) * HEAD_DIM, :] for h in range(HEADS_PER_GROUP)], axis=1)


def _nsa_kernel(qc_ref, qr_ref, g_ref, kc_ref, vct_ref, band_ref, ks_ref, vst_ref, kw_ref, vwt_ref,
                o_ref, bias_ref, score_ref, sa_ref, sb_ref, pa_ref, pb_ref):
    i = pl.program_id(2)
    qb = Q_BLOCK
    width = HEADS_PER_GROUP * qb
    n_cmp = kc_ref.shape[2]
    n_slc = score_ref.shape[0]
    t_q = i * qb + lax.broadcasted_iota(I32, (1, qb), 1)
    t_row = jnp.concatenate([t_q] * HEADS_PER_GROUP, axis=1)
    zeros_pad = jnp.zeros((LANES - HEAD_DIM, width), BF16)

    cc = band_ref.shape[1]
    bpc = cc * CMP_STRIDE // SLC_BLOCK
    qc_t = jnp.concatenate([_heads_on_lanes(qc_ref), zeros_pad], axis=0)
    chunk_scores, maxima = [], []
    for c in range(n_cmp // cc):
        s = _dot(kc_ref[0, 0, c * cc:(c + 1) * cc, :], qc_t)
        cmp_end = (c * cc + lax.broadcasted_iota(I32, (cc, 1), 0)) * CMP_STRIDE + (CMP_BLOCK - 1)
        s = jnp.where(cmp_end <= t_row, s, NEG)
        chunk_scores.append(s)
        maxima.append(jnp.maximum(jnp.max(s, axis=0, keepdims=True), NEG * 1e-10))
    m_fin = functools.reduce(jnp.maximum, maxima)
    shares = []
    for c, (s, mx) in enumerate(zip(chunk_scores, maxima)):
        lhs = jnp.concatenate([vct_ref[0, 0, :, c * cc:(c + 1) * cc], band_ref[...]], axis=0)
        share = _dot(lhs, jnp.exp2(s - mx).astype(BF16))
        shares.append(share * jnp.exp2(mx - m_fin))
    acc_c = shares[0][:V_ROWS]
    for sh in shares[1:]:
        acc_c = acc_c + sh[:V_ROWS]
    rinv = 1.0 / jnp.maximum(acc_c[HEAD_DIM:HEAD_DIM + 1], 1e-30)
    o_cmp = acc_c[:HEAD_DIM] * rinv
    imp_rows = []
    for c, sh in enumerate(shares):
        u = sh[V_ROWS:V_ROWS + bpc]
        if c > 0:
            spill = shares[c - 1][V_ROWS + bpc:V_ROWS + bpc + 8]
            u = jnp.concatenate([u[:8] + spill, u[8:]], axis=0)
        u = u * rinv
        imp_c = u[:, 0:qb]
        for h in range(1, HEADS_PER_GROUP):
            imp_c = imp_c + u[:, h * qb:(h + 1) * qb]
        imp_rows.append(imp_c)
    imp = jnp.concatenate(imp_rows, axis=0)

    j_idx = lax.broadcasted_iota(I32, (n_slc, 1), 0)
    cur = t_q // SLC_BLOCK
    valid = j_idx * SLC_BLOCK <= t_q
    forced = (j_idx == 0) | (j_idx == cur) | (j_idx == cur - 1)
    score_ref[...] = jnp.where(valid & jnp.logical_not(forced), imp, -1.0)

    def pick(_, carry, rows):
        sc = score_ref[0:rows, :]
        best = jnp.max(sc, axis=0, keepdims=True)
        idx = jnp.min(jnp.where(sc == best, j_idx[0:rows], n_slc), axis=0, keepdims=True)
        score_ref[0:rows, :] = jnp.where(j_idx[0:rows] == idx, PICKED, sc)
        return carry

    pick_rows = min(PICK_ROWS, n_slc)
    need = (i * qb + qb - 1) // (SLC_BLOCK * pick_rows)
    for r in range(n_slc // pick_rows):
        @pl.when(need == r)
        def _(rows=(r + 1) * pick_rows):
            lax.fori_loop(0, N_SELECT - N_FORCED, functools.partial(pick, rows=rows), 0)
    bias = jnp.where(forced | (score_ref[...] == PICKED), 0.0, NEG).astype(BF16)
    bias_ref[...] = jnp.concatenate([bias] * HEADS_PER_GROUP, axis=1)

    qr_heads = _heads_on_lanes(qr_ref)
    zeros_sel = jnp.zeros((LANES - HEAD_DIM - BIAS_ROWS, width), BF16)
    tk = KEY_TILE
    blocks_per_tile = tk // SLC_BLOCK

    def scores(kt, s_ref, masked):
        b0 = pl.multiple_of((kt * blocks_per_tile) // BIAS_ROWS * BIAS_ROWS, BIAS_ROWS)
        q_aug = jnp.concatenate([qr_heads, bias_ref[pl.ds(b0, BIAS_ROWS), :], zeros_sel], axis=0)
        k0 = pl.multiple_of(kt * tk, tk)
        s = _dot(ks_ref[0, 0, pl.ds(k0, tk), :], q_aug)
        if masked:
            s = jnp.where(k0 + lax.broadcasted_iota(I32, (tk, 1), 0) <= t_row, s, NEG)
        s_ref[...] = s
        part = jnp.max(s.reshape(tk // MAX_ROWS, MAX_ROWS, width), axis=0)
        return jnp.max(part, axis=0, keepdims=True)

    def values(kt, p_ref):
        return _dot(vst_ref[0, 0, jnp.maximum(kt, 0)], p_ref[...])

    def trip(kk, carry, next_scores):
        m, acc, beta, cmax_a, cmax_b = carry
        kt = 2 * kk
        live = jnp.where(kk > 0, 1.0, 0.0)
        acc = acc + live * (beta * values(kt - 2, pa_ref) + values(kt - 1, pb_ref))
        m1 = jnp.maximum(m, cmax_a)
        pa_ref[...] = jnp.exp2((sa_ref[...] - m1).astype(BF16))
        m2 = jnp.maximum(m1, cmax_b)
        pb_ref[...] = jnp.exp2((sb_ref[...] - m2).astype(BF16))
        beta = jnp.exp2(m1 - m2)
        if next_scores is not None:
            cmax_a = scores(kt + 2, sa_ref, next_scores)
            cmax_b = scores(kt + 3, sb_ref, next_scores)
        return m2, (jnp.exp2(m - m1) * beta) * acc, beta, cmax_a, cmax_b

    @pl.when((pl.program_id(0) == 0) & (pl.program_id(1) == 0) & (i == 0))
    def _():
        pa_ref[...] = jnp.zeros_like(pa_ref)
        pb_ref[...] = jnp.zeros_like(pb_ref)

    n_full = (i * qb) // tk
    n_unmasked = jnp.maximum(n_full // 2 - 1, 0)
    carry = (jnp.full((1, width), NEG, F32), jnp.zeros((V_ROWS, width), F32), jnp.ones((1, width), F32),
             scores(0, sa_ref, True), scores(1, sb_ref, True))
    carry = lax.fori_loop(0, n_unmasked, functools.partial(trip, next_scores=False), carry)
    carry = lax.fori_loop(n_unmasked, n_full // 2, functools.partial(trip, next_scores=True), carry)
    qr_t = jnp.concatenate([qr_heads, zeros_pad], axis=0)
    n_win = WINDOW // qb + 1
    s_w = []
    for r in range(n_win):
        j = i - (n_win - 1) + r
        k0 = pl.multiple_of(jnp.maximum(j, 0) * qb, qb)
        s = _dot(kw_ref[0, 0, pl.ds(k0, qb), :], qr_t)
        if r in (0, n_win - 1):
            kpos = j * qb + lax.broadcasted_iota(I32, (qb, 1), 0)
            diff = t_row - kpos
            s = jnp.where((diff >= 0) & (diff < WINDOW) & (kpos >= 0), s, NEG)
        else:
            s = s + jnp.where(j >= 0, 0.0, NEG)
        s_w.append(s)
    m_w = s_w[0].max(axis=0, keepdims=True)
    for r in range(1, n_win):
        m_w = jnp.maximum(m_w, s_w[r].max(axis=0, keepdims=True))

    last = n_full // 2
    _, acc_s, beta, _, _ = trip(last, carry, None)
    acc_s = acc_s + beta * values(2 * last, pa_ref) + values(2 * last + 1, pb_ref)
    o_slc = acc_s[:HEAD_DIM] / acc_s[HEAD_DIM:HEAD_DIM + 1]

    acc_w = jnp.zeros((V_ROWS, width), F32)
    for r in range(n_win):
        jc = jnp.maximum(i - (n_win - 1) + r, 0)
        acc_w = acc_w + _dot(vwt_ref[0, 0, jc], jnp.exp2((s_w[r] - m_w).astype(BF16)))
    o_win = acc_w[:HEAD_DIM] / acc_w[HEAD_DIM:HEAD_DIM + 1]

    gts = g_ref[0]
    for h in range(HEADS_PER_GROUP):
        sl = slice(h * qb, (h + 1) * qb)
        r0 = h * N_BRANCH
        o = (gts[r0:r0 + 1] * o_cmp[:, sl] + gts[r0 + 1:r0 + 2] * o_slc[:, sl] + gts[r0 + 2:r0 + 3] * o_win[:, sl])
        o_ref[0, h * HEAD_DIM:(h + 1) * HEAD_DIM, :] = o.astype(BF16)


def _nsa_attention(qc_t, qr_t, gates_t, k_cmp, v_cmp_t, band, k_slc, v_slc_t, k_win, v_win_t):
    b, qd, s = qc_t.shape
    g = N_KV_GROUPS
    n_cmp = k_cmp.shape[2]
    n_slc = s // SLC_BLOCK
    nt, tk = v_slc_t.shape[2], v_slc_t.shape[4]
    rows = HEADS_PER_GROUP * HEAD_DIM
    width = HEADS_PER_GROUP * Q_BLOCK
    qspec = pl.BlockSpec((1, rows, Q_BLOCK), lambda bi, gi, i: (bi, gi, i))
    per_bg = lambda shape: pl.BlockSpec((1, 1) + shape, lambda bi, gi, i: (bi, gi) + (0,) * len(shape))
    return pl.pallas_call(
        _nsa_kernel,
        grid=(b, g, s // Q_BLOCK),
        in_specs=[
            qspec, qspec,
            pl.BlockSpec((1, 16, Q_BLOCK), lambda bi, gi, i: (bi, gi, i)),
            per_bg((n_cmp, LANES)), per_bg((V_ROWS, n_cmp)),
            pl.BlockSpec(band.shape, lambda bi, gi, i: (0, 0)),
            per_bg((s, LANES)), per_bg((nt, V_ROWS, tk)),
            per_bg((s, LANES)), per_bg((s // Q_BLOCK, V_ROWS, Q_BLOCK)),
        ],
        out_specs=qspec,
        out_shape=jax.ShapeDtypeStruct((b, qd, s), BF16),
        scratch_shapes=[
            pltpu.VMEM((n_slc, width), BF16),
            pltpu.VMEM((n_slc, Q_BLOCK), F32),
            pltpu.VMEM((tk, width), F32), pltpu.VMEM((tk, width), F32),
            pltpu.VMEM((tk, width), BF16), pltpu.VMEM((tk, width), BF16),
        ],
        compiler_params=_cparams("arbitrary", "arbitrary", "arbitrary"),
        name="nsa_attention",
    )(qc_t, qr_t, gates_t, k_cmp, v_cmp_t, band, k_slc, v_slc_t, k_win, v_win_t)


def _selection_band(n_cmp):
    cc = min(CMP_CHUNK, n_cmp)
    bpc = cc * CMP_STRIDE // SLC_BLOCK
    c0 = jnp.arange(cc) * CMP_STRIDE
    j0 = jnp.arange(bpc + 16) * SLC_BLOCK
    ov = jnp.minimum(c0[None, :] + CMP_BLOCK, j0[:, None] + SLC_BLOCK) - jnp.maximum(c0[None, :], j0[:, None])
    return (jnp.clip(ov, 0).astype(F32) / CMP_BLOCK).astype(BF16)


def _out_proj_kernel(x_ref, ot_ref, w_ref, o_ref):
    o_ref[0] = x_ref[0] + _dot_tn(ot_ref[0], w_ref[...])


def _out_proj(x, o_t, w_out, tm=512):
    b, s, d = x.shape
    qd = o_t.shape[1]
    return pl.pallas_call(
        _out_proj_kernel,
        grid=(b, s // tm),
        in_specs=[
            pl.BlockSpec((1, tm, d), lambda i, j: (i, j, 0)),
            pl.BlockSpec((1, qd, tm), lambda i, j: (i, 0, j)),
            pl.BlockSpec((qd, d), lambda i, j: (0, 0)),
        ],
        out_specs=pl.BlockSpec((1, tm, d), lambda i, j: (i, j, 0)),
        out_shape=jax.ShapeDtypeStruct((b, s, d), F32),
        compiler_params=_cparams("arbitrary", "arbitrary"),
        name="nsa_out_proj",
    )(x, o_t, w_out.astype(BF16))


def _nsa_layer(x, positions, g_kv, g_q, w_kv, cmp_pos_k, cmp_w1_k, cmp_w2_k, cmp_pos_v, cmp_w1_v, cmp_w2_v,
               w_in, w_out):
    b, s, d = x.shape
    g = N_KV_GROUPS
    kcv, k_slc, k_win, v_slc_t, v_win_t, qc_t, qr_t, gates_t = _projections(x, positions, g_kv, g_q, w_kv, w_in)
    n_half = s // CMP_STRIDE
    halves = kcv.reshape(b, n_half, CMP_STRIDE, 2, g, HEAD_DIM).transpose(0, 3, 4, 1, 2, 5)
    halves = halves.reshape(b, 2, g, n_half, CMP_STRIDE * HEAD_DIM)
    k_cmp = _compress(halves[:, 0], cmp_pos_k, cmp_w1_k, cmp_w2_k, transposed_out=False)
    v_cmp_t = _compress(halves[:, 1], cmp_pos_v, cmp_w1_v, cmp_w2_v, transposed_out=True)
    o_t = _nsa_attention(qc_t, qr_t, gates_t, k_cmp, v_cmp_t, _selection_band(n_half), k_slc, v_slc_t, k_win,
                         v_win_t)
    return _out_proj(x, o_t, w_out)


def kernel(x, positions, norm_mix, norm_ffn, conv_w_in, conv_w, conv_w_out, norm_kv, w_kv, cmp_pos_k, cmp_w1_k, cmp_w2_k, cmp_pos_v, cmp_w1_v, cmp_w2_v, attn_w_in, attn_w_out, router_group_w, router_group_b, router_expert_w, router_expert_b, expert_w_gate, expert_w_up, expert_w_down, norm_final):
    b, s, d = x.shape
    assert s % (2 * KEY_TILE) == 0 and (s // SLC_BLOCK) % BIAS_ROWS == 0 and WINDOW % Q_BLOCK == 0

    def moe(xs, l, final_norm):
        return _hier_moe(xs.reshape(b * s, d), norm_ffn[l], router_group_w[l], router_group_b[l],
                         router_expert_w[l], router_expert_b[l], expert_w_gate, expert_w_up,
                         expert_w_down, l, norm_final, final_norm).reshape(b, s, d)

    x = _mixer(x, norm_mix[0], conv_w_in[0], conv_w[0], conv_w_out[0])
    x = moe(x, 0, False)
    x = _nsa_layer(x, positions, norm_kv, norm_mix[1], w_kv, cmp_pos_k, cmp_w1_k, cmp_w2_k,
                   cmp_pos_v, cmp_w1_v, cmp_w2_v, attn_w_in[0], attn_w_out[0])
    return moe(x, 1, True)
```

```python
import functools

import jax
import jax.numpy as jnp
from jax import lax
from jax.experimental import pallas as pl
from jax.experimental.pallas import tpu as pltpu

F32 = jnp.float32
BF16 = jnp.bfloat16
I32 = jnp.int32

RMS_EPS = 1e-6
ROPE_THETA = 10000.0
HEAD_DIM = 64
HALF = HEAD_DIM // 2
N_HEADS = 16
N_KV_GROUPS = 4
HEADS_PER_GROUP = N_HEADS // N_KV_GROUPS
N_BRANCH = 3
CMP_BLOCK = 32
CMP_STRIDE = 16
SLC_BLOCK = 64
N_SELECT = 16
WINDOW = 512
Q_BLOCK = 256
FORCE_BONUS = 1000.0
N_GROUPS = 4
EXPERTS_PER_GROUP = 8
N_EXPERTS = N_GROUPS * EXPERTS_PER_GROUP
CHUNK_ROWS = 256
NEG = -1e30
LANES = 128
KEY_TILE = 512
BIAS_ROWS = 16
MAX_ROWS = 64
CMP_CHUNK = 256
N_FORCED = 3
PICK_ROWS = 64
PICKED = -2.0
V_ROWS = 80
LOG2E = 1.4426950408889634
VMEM_LIMIT = 56 * 1024 * 1024


def _cparams(*sem):
    return pltpu.CompilerParams(dimension_semantics=sem, vmem_limit_bytes=VMEM_LIMIT)


def _rms_scale(x):
    return x * lax.rsqrt(jnp.mean(x * x, axis=-1, keepdims=True) + RMS_EPS)


def _dot(a, b):
    return jnp.dot(a, b, preferred_element_type=F32)


def _dot_nt(a, b):
    return lax.dot_general(a, b, (((1,), (1,)), ((), ())), preferred_element_type=F32)


def _dot_tn(a, b):
    return lax.dot_general(a, b, (((0,), (0,)), ((), ())), preferred_element_type=F32)


def _ones_row_block(shape):
    row = lax.broadcasted_iota(I32, shape, len(shape) - 2)
    return jnp.where(row == 0, 1.0, 0.0).astype(F32)


def _mixer_kernel(x_ref, g_ref, win_ref, cw_ref, wout_ref, o_ref, tail_ref):
    d = x_ref.shape[-1]
    tm = x_ref.shape[1]

    @pl.when(pl.program_id(1) == 0)
    def _():
        tail_ref[...] = jnp.zeros_like(tail_ref)

    x = x_ref[0]
    h = (_rms_scale(x) * g_ref[...]).astype(BF16)
    proj = _dot(h, win_ref[...])
    b_gate = proj[:, :d]
    u = proj[:, d:2 * d] * proj[:, 2 * d:]
    row = lax.broadcasted_iota(I32, (tm, 1), 0)
    prev1 = tail_ref[7:8, :]
    prev2 = tail_ref[6:7, :]
    u1 = jnp.where(row >= 1, pltpu.roll(u, 1, 0), prev1)
    u2 = jnp.where(row >= 2, pltpu.roll(u, 2, 0), jnp.where(row == 1, prev1, prev2))
    y = cw_ref[2:3, :] * u + cw_ref[1:2, :] * u1 + cw_ref[0:1, :] * u2
    tail_ref[...] = u[tm - 8:, :]
    z = (b_gate * y).astype(BF16)
    o_ref[0] = x + _dot(z, wout_ref[...])


def _mixer(x, g, w_in, conv_w, w_out, tm=512):
    b, s, d = x.shape
    return pl.pallas_call(
        _mixer_kernel,
        grid=(b, s // tm),
        in_specs=[
            pl.BlockSpec((1, tm, d), lambda i, j: (i, j, 0)),
            pl.BlockSpec((1, d), lambda i, j: (0, 0)),
            pl.BlockSpec((d, 3 * d), lambda i, j: (0, 0)),
            pl.BlockSpec((3, d), lambda i, j: (0, 0)),
            pl.BlockSpec((d, d), lambda i, j: (0, 0)),
        ],
        out_specs=pl.BlockSpec((1, tm, d), lambda i, j: (i, j, 0)),
        out_shape=jax.ShapeDtypeStruct((b, s, d), F32),
        scratch_shapes=[pltpu.VMEM((8, d), F32)],
        compiler_params=_cparams("arbitrary", "arbitrary"),
        name="conv_mixer",
    )(x, g.reshape(1, d), w_in.astype(BF16), conv_w, w_out.astype(BF16))


def _router_kernel(x_ref, g_ref, wt_ref, b_ref, hb_ref, eid_ref, wts_ref, counts_ref, cnt_ref):
    tm = x_ref.shape[0]
    h = _rms_scale(x_ref[...]) * g_ref[...]
    bits = lax.bitcast_convert_type(h.astype(BF16).astype(F32), jnp.uint32)
    half = h.shape[1] // 2
    hb_ref[...] = (bits[:, :half] >> 16) | (bits[:, half:] & jnp.uint32(0xFFFF0000))
    w = wt_ref[...]
    w_hi = w.astype(BF16)
    w_lo = (w - w_hi.astype(F32)).astype(BF16)
    h_hi = h.astype(BF16)
    h_lo = (h - h_hi.astype(F32)).astype(BF16)
    lt = _dot_nt(w_hi, h_hi) + _dot_nt(w_hi, h_lo) + _dot_nt(w_lo, h_hi) + b_ref[...]
    r8 = lax.broadcasted_iota(I32, (8, 1), 0)
    gl = jnp.where(r8 < N_GROUPS, lt[0:8], -jnp.inf)
    gmax = jnp.max(gl, axis=0, keepdims=True)
    gtop = jnp.min(jnp.where(gl == gmax, r8, 8), axis=0, keepdims=True)
    p_g = 1.0 / jnp.sum(jnp.exp(gl - gmax), axis=0, keepdims=True)
    chosen = lt[8:16]
    for k in range(1, N_GROUPS):
        chosen = jnp.where(gtop == k, lt[8 + 8 * k:16 + 8 * k], chosen)
    v0 = jnp.max(chosen, axis=0, keepdims=True)
    i0 = jnp.min(jnp.where(chosen == v0, r8, 8), axis=0, keepdims=True)
    rest = jnp.where(r8 == i0, -jnp.inf, chosen)
    v1 = jnp.max(rest, axis=0, keepdims=True)
    i1 = jnp.min(jnp.where(rest == v1, r8, 8), axis=0, keepdims=True)
    e = jnp.exp(v1 - v0)
    w0 = p_g / (1.0 + e)
    w1 = p_g * e / (1.0 + e)
    e0 = gtop * EXPERTS_PER_GROUP + i0
    e1 = gtop * EXPERTS_PER_GROUP + i1
    wts_ref[...] = jnp.where(r8 == 0, w0, jnp.where(r8 == 1, w1, 0.0))

    @pl.when(pl.program_id(0) == 0)
    def _():
        cnt_ref[...] = jnp.zeros_like(cnt_ref)

    rexp = lax.broadcasted_iota(I32, (N_EXPERTS, 1), 0)
    tri = (lax.broadcasted_iota(I32, (tm, tm), 0) <= lax.broadcasted_iota(I32, (tm, tm), 1)).astype(BF16)
    oh0 = rexp == e0
    oh1 = rexp == e1
    cum0 = _dot(oh0.astype(BF16), tri)
    cum1 = _dot(oh1.astype(BF16), tri)
    before0 = cnt_ref[...]
    before1 = before0 + cum0[:, tm - 1:tm]
    rank0 = jnp.sum(jnp.where(oh0, before0 + cum0, 0.0), axis=0, keepdims=True) - 1.0
    rank1 = jnp.sum(jnp.where(oh1, before1 + cum1, 0.0), axis=0, keepdims=True) - 1.0
    cnt_ref[...] = before1 + cum1[:, tm - 1:tm]
    counts_ref[...] = jnp.broadcast_to(cnt_ref[...], counts_ref.shape).astype(I32)
    eid_ref[...] = jnp.where(r8 == 0, e0, jnp.where(r8 == 1, e1, jnp.where(
        r8 == 2, rank0.astype(I32), jnp.where(r8 == 3, rank1.astype(I32), 0))))


def _router(xf, g, w_rg, b_rg, w_re, b_re, tm=512):
    n, d = xf.shape
    rows = 8 + N_EXPERTS
    wt = jnp.zeros((rows, d), F32).at[:N_GROUPS].set(w_rg.T).at[8:].set(w_re.T)
    bt = jnp.zeros((rows, 1), F32).at[:N_GROUPS, 0].set(b_rg).at[8:, 0].set(b_re)
    return pl.pallas_call(
        _router_kernel,
        grid=(n // tm,),
        in_specs=[
            pl.BlockSpec((tm, d), lambda i: (i, 0)),
            pl.BlockSpec((1, d), lambda i: (0, 0)),
            pl.BlockSpec((rows, d), lambda i: (0, 0)),
            pl.BlockSpec((rows, 1), lambda i: (0, 0)),
        ],
        out_specs=[
            pl.BlockSpec((tm, d // 2), lambda i: (i, 0)),
            pl.BlockSpec((8, tm), lambda i: (0, i)),
            pl.BlockSpec((8, tm), lambda i: (0, i)),
            pl.BlockSpec((N_EXPERTS, LANES), lambda i: (0, 0)),
        ],
        out_shape=[
            jax.ShapeDtypeStruct((n, d // 2), jnp.uint32),
            jax.ShapeDtypeStruct((8, n), I32),
            jax.ShapeDtypeStruct((8, n), F32),
            jax.ShapeDtypeStruct((N_EXPERTS, LANES), I32),
        ],
        scratch_shapes=[pltpu.VMEM((N_EXPERTS, 1), F32)],
        compiler_params=_cparams("arbitrary"),
        name="moe_router",
    )(xf, g.reshape(1, d), wt, bt)


def _take_rows(a, idx):
    return a.at[idx].get(mode="promise_in_bounds")


def _dispatch_plan(eid, counts, n_tok):
    a_exp = eid[0:2].reshape(-1)
    rank = eid[2:4].reshape(-1)
    n_asg = a_exp.shape[0]
    a_tok = jnp.arange(n_asg, dtype=I32) % n_tok
    padded = (counts + CHUNK_ROWS - 1) // CHUNK_ROWS * CHUNK_ROWS
    pend = jnp.cumsum(padded)
    pstart = pend - padded
    dest = (_take_rows(pstart, a_exp) + rank).astype(I32)
    n_chunks = -(-n_asg // CHUNK_ROWS) + N_EXPERTS
    row_tok = jnp.zeros((n_chunks * CHUNK_ROWS,), I32).at[dest].set(
        a_tok, unique_indices=True, mode="promise_in_bounds")
    chunk_row0 = jnp.arange(n_chunks, dtype=I32) * CHUNK_ROWS
    chunk_exp = jnp.minimum(jnp.sum((pend[None, :] <= chunk_row0[:, None]).astype(I32), axis=1), N_EXPERTS - 1)
    return row_tok, dest.reshape(2, n_tok), chunk_exp


def _expert_kernel(ce_ref, rt_ref, hb_ref, wg_ref, wu_ref, wd_ref, o_ref, wg_bf, wu_bf, wd_bf, xbuf, xc_ref, sem):
    c = pl.program_id(0)
    last = pl.num_programs(0) - 1
    slot = c % 2

    def start_gather(chunk, to_slot):
        for r in range(CHUNK_ROWS):
            tok = rt_ref[chunk * CHUNK_ROWS + r]
            pltpu.make_async_copy(hb_ref.at[pl.ds(tok, 1), :], xbuf.at[to_slot, pl.ds(r, 1), :],
                                  sem.at[to_slot]).start()

    def wait_gather(of_slot):
        pltpu.make_async_copy(hb_ref.at[pl.ds(0, CHUNK_ROWS), :], xbuf.at[of_slot], sem.at[of_slot]).wait()

    @pl.when(c == 0)
    def _():
        start_gather(0, 0)

    @pl.when((c == 0) | (ce_ref[c] != ce_ref[jnp.maximum(c - 1, 0)]))
    def _():
        wg_bf[...] = wg_ref[0, 0].astype(BF16)
        wu_bf[...] = wu_ref[0, 0].astype(BF16)
        wd_bf[...] = wd_ref[0, 0].astype(BF16)

    wait_gather(slot)
    words = xbuf[slot]
    lo = lax.bitcast_convert_type(words << 16, F32)
    hi = lax.bitcast_convert_type(words & jnp.uint32(0xFFFF0000), F32)
    xc_ref[...] = jnp.concatenate([lo, hi], axis=1).astype(BF16)
    start_gather(jnp.minimum(c + 1, last), 1 - slot)
    xc = xc_ref[...]
    gate = _dot(xc, wg_bf[...])
    up = _dot(xc, wu_bf[...])
    act = (gate * jax.nn.sigmoid(gate) * up).astype(BF16)
    o_ref[...] = _dot(act, wd_bf[...])

    @pl.when(c == last)
    def _():
        wait_gather(1 - slot)


def _experts(hb, row_tok, chunk_exp, w_gate, w_up, w_down, layer):
    d = w_gate.shape[-2]
    de = w_gate.shape[-1]
    n_chunks = chunk_exp.shape[0]
    weight = lambda shape: pl.BlockSpec((1, 1) + shape, lambda c, ce, rt: (layer, ce[c], 0, 0))
    grid_spec = pltpu.PrefetchScalarGridSpec(
        num_scalar_prefetch=2,
        grid=(n_chunks,),
        in_specs=[pl.BlockSpec(memory_space=pl.ANY), weight((d, de)), weight((d, de)), weight((de, d))],
        out_specs=pl.BlockSpec((CHUNK_ROWS, d), lambda c, ce, rt: (c, 0)),
        scratch_shapes=[pltpu.VMEM((d, de), BF16), pltpu.VMEM((d, de), BF16), pltpu.VMEM((de, d), BF16),
                        pltpu.VMEM((2, CHUNK_ROWS, d // 2), jnp.uint32), pltpu.VMEM((CHUNK_ROWS, d), BF16),
                        pltpu.SemaphoreType.DMA((2,))],
    )
    return pl.pallas_call(
        _expert_kernel,
        grid_spec=grid_spec,
        out_shape=jax.ShapeDtypeStruct((n_chunks * CHUNK_ROWS, d), F32),
        compiler_params=_cparams("arbitrary"),
        name="moe_experts",
    )(chunk_exp, row_tok, hb, w_gate, w_up, w_down)


def _combine_kernel(x_ref, y0_ref, y1_ref, w_ref, g_ref, o_ref, *, final_norm):
    w = w_ref[...]
    out = x_ref[...] + w[:, 0:1] * y0_ref[...] + w[:, 1:2] * y1_ref[...]
    if final_norm:
        out = _rms_scale(out) * g_ref[...]
    o_ref[...] = out


def _combine(xf, y0, y1, wcol, g, final_norm, tm=512):
    n, d = xf.shape
    row = pl.BlockSpec((tm, d), lambda i: (i, 0))
    return pl.pallas_call(
        functools.partial(_combine_kernel, final_norm=final_norm),
        grid=(n // tm,),
        in_specs=[row, row, row, pl.BlockSpec((tm, 8), lambda i: (i, 0)), pl.BlockSpec((1, d), lambda i: (0, 0))],
        out_specs=row,
        out_shape=jax.ShapeDtypeStruct((n, d), F32),
        compiler_params=_cparams("arbitrary"),
        name="moe_combine",
    )(xf, y0, y1, wcol, g.reshape(1, d))


def _hier_moe(xf, g_ffn, w_rg, b_rg, w_re, b_re, w_gate, w_up, w_down, layer, g_final, final_norm):
    n = xf.shape[0]
    hb, eid, wts, counts = _router(xf, g_ffn, w_rg, b_rg, w_re, b_re)
    row_tok, dest, chunk_exp = _dispatch_plan(eid, counts[:, 0], n)
    ys = _experts(hb, row_tok, chunk_exp, w_gate, w_up, w_down, layer)
    return _combine(xf, _take_rows(ys, dest[0]), _take_rows(ys, dest[1]), wts.T, g_final, final_norm)


def _proj_kernel(x_ref, prow_ref, pcol_ref, gkv_ref, gq_ref, wnat_ref, wvt_ref, wqt_ref,
                 kcv_ref, ks_ref, kw_ref, vst_ref, vwt_ref, qc_ref, qr_ref, gt_ref):
    tm = x_ref.shape[1]
    gd = N_KV_GROUPS * HEAD_DIM
    qd = N_HEADS * HEAD_DIM
    y = _rms_scale(x_ref[0])
    hkv = (y * gkv_ref[...]).astype(BF16)
    hq = (y * gq_ref[...]).astype(BF16)

    nat = _dot(hkv, wnat_ref[...])
    kcv_ref[0] = nat[:, :2 * gd]
    lane = lax.broadcasted_iota(I32, (1, LANES), 1)
    inv = jnp.exp((lane % HALF).astype(F32) * (-jnp.log(ROPE_THETA) / HALF))
    ang = pcol_ref[0].astype(F32) * inv
    ctab = jnp.where(lane < HEAD_DIM, jnp.cos(ang), 0.0)
    stab = jnp.where(lane < HEAD_DIM, jnp.sin(ang), 0.0)
    key = pl.program_id(1) * tm + lax.broadcasted_iota(I32, (tm, 1), 0)
    blk = (key // SLC_BLOCK) % BIAS_ROWS
    onehot = jnp.where(lane == HEAD_DIM + blk, 1.0, 0.0)
    for g in range(N_KV_GROUPS):
        for slot, ref in ((0, ks_ref), (1, kw_ref)):
            lo = 2 * gd + (slot * N_KV_GROUPS + g) * LANES
            t = nat[:, lo:lo + LANES]
            roped = t * ctab + pltpu.roll(t, HEAD_DIM, 1) * stab
            if slot == 0:
                roped = roped + onehot
            ref[0, g] = roped.astype(BF16)

    vt = _dot_nt(wvt_ref[...], hkv)
    ones_rows = _ones_row_block((N_KV_GROUPS, V_ROWS - HEAD_DIM, tm))
    vs3 = jnp.concatenate([vt[:gd].reshape(N_KV_GROUPS, HEAD_DIM, tm), ones_rows], axis=1).astype(BF16)
    vw3 = jnp.concatenate([vt[gd:].reshape(N_KV_GROUPS, HEAD_DIM, tm), ones_rows], axis=1).astype(BF16)
    vst_ref[0, :, 0] = vs3
    for w in range(tm // Q_BLOCK):
        vwt_ref[0, :, w] = vw3[:, :, w * Q_BLOCK:(w + 1) * Q_BLOCK]

    qt = _dot_nt(wqt_ref[...], hq)
    scale = HEAD_DIM ** -0.5 * LOG2E
    q3 = qt[:qd].reshape(N_HEADS, HEAD_DIM, tm)
    qc_ref[0] = (qt[:qd] * scale).astype(BF16)
    frq = lax.broadcasted_iota(I32, (HALF, 1), 0).astype(F32)
    ang_t = jnp.exp(frq * (-jnp.log(ROPE_THETA) / HALF)) * prow_ref[0].astype(F32)
    cos_t = jnp.cos(ang_t)[None]
    sin_t = jnp.sin(ang_t)[None]
    t1 = q3[:, :HALF]
    t2 = q3[:, HALF:]
    qr = jnp.concatenate([t1 * cos_t - t2 * sin_t, t2 * cos_t + t1 * sin_t], axis=1)
    qr_ref[0] = (qr * scale).reshape(qd, tm).astype(BF16)
    gt_ref[0] = jax.nn.sigmoid(qt[qd:])


def _rot_half_cols(w):
    return jnp.concatenate([-w[:, HALF:], w[:, :HALF]], axis=1)


def _projections(x, positions, g_kv, g_q, w_kv, w_q, tm=KEY_TILE):
    b, s, d = x.shape
    gd = N_KV_GROUPS * HEAD_DIM
    qd = N_HEADS * HEAD_DIM
    slot = lambda i: w_kv[:, i * gd:(i + 1) * gd]
    aug = []
    for sl in (2, 4):
        for g in range(N_KV_GROUPS):
            wk = slot(sl)[:, g * HEAD_DIM:(g + 1) * HEAD_DIM]
            aug += [wk, _rot_half_cols(wk)]
    w_nat = jnp.concatenate([slot(0), slot(1)] + aug, axis=1).astype(BF16)
    w_vt = jnp.concatenate([slot(3), slot(5)], axis=1).T.astype(BF16)
    gate_cols = w_q[:, qd:].reshape(d, N_KV_GROUPS, HEADS_PER_GROUP * N_BRANCH)
    gate_cols = jnp.pad(gate_cols, ((0, 0), (0, 0), (0, 16 - HEADS_PER_GROUP * N_BRANCH))).reshape(d, 64)
    w_qt = jnp.concatenate([w_q[:, :qd], gate_cols], axis=1).T.astype(BF16)
    nt = s // tm
    n_nat = w_nat.shape[1]
    const = lambda shape: pl.BlockSpec(shape, lambda i, j: (0,) * len(shape))
    return pl.pallas_call(
        _proj_kernel,
        grid=(b, nt),
        in_specs=[
            pl.BlockSpec((1, tm, d), lambda i, j: (i, j, 0)),
            pl.BlockSpec((1, 1, tm), lambda i, j: (i, 0, j)),
            pl.BlockSpec((1, tm, 1), lambda i, j: (i, j, 0)),
            const((1, d)), const((1, d)), const((d, n_nat)), const((2 * gd, d)), const((qd + 64, d)),
        ],
        out_specs=[
            pl.BlockSpec((1, tm, 2 * gd), lambda i, j: (i, j, 0)),
            pl.BlockSpec((1, N_KV_GROUPS, tm, LANES), lambda i, j: (i, 0, j, 0)),
            pl.BlockSpec((1, N_KV_GROUPS, tm, LANES), lambda i, j: (i, 0, j, 0)),
            pl.BlockSpec((1, N_KV_GROUPS, 1, V_ROWS, tm), lambda i, j: (i, 0, j, 0, 0)),
            pl.BlockSpec((1, N_KV_GROUPS, tm // Q_BLOCK, V_ROWS, Q_BLOCK), lambda i, j: (i, 0, j, 0, 0)),
            pl.BlockSpec((1, qd, tm), lambda i, j: (i, 0, j)),
            pl.BlockSpec((1, qd, tm), lambda i, j: (i, 0, j)),
            pl.BlockSpec((1, 64, tm), lambda i, j: (i, 0, j)),
        ],
        out_shape=[
            jax.ShapeDtypeStruct((b, s, 2 * gd), F32),
            jax.ShapeDtypeStruct((b, N_KV_GROUPS, s, LANES), BF16),
            jax.ShapeDtypeStruct((b, N_KV_GROUPS, s, LANES), BF16),
            jax.ShapeDtypeStruct((b, N_KV_GROUPS, nt, V_ROWS, tm), BF16),
            jax.ShapeDtypeStruct((b, N_KV_GROUPS, s // Q_BLOCK, V_ROWS, Q_BLOCK), BF16),
            jax.ShapeDtypeStruct((b, qd, s), BF16),
            jax.ShapeDtypeStruct((b, qd, s), BF16),
            jax.ShapeDtypeStruct((b, 64, s), F32),
        ],
        compiler_params=_cparams("arbitrary", "arbitrary"),
        name="nsa_projections",
    )(x, positions.reshape(b, 1, s), positions.reshape(b, s, 1), g_kv.reshape(1, d), g_q.reshape(1, d),
      w_nat, w_vt, w_qt)


def _compress_kernel(a_ref, pos_ref, w1_ref, w2_ref, o_ref, *, transposed_out):
    half = w1_ref.shape[0] // 2
    a = a_ref[0, 0]
    top = (a + pos_ref[0:1, :]).astype(BF16)
    bot = (a + pos_ref[1:2, :]).astype(BF16)
    p = _dot(top, w1_ref[:half, :])
    q = _dot(bot, w1_ref[half:, :])
    n = a.shape[0]
    hidden = p + pltpu.roll(q, n - 1, 0)
    act = jax.nn.gelu(hidden, approximate=True).astype(BF16)
    if transposed_out:
        v_t = _dot_nt(w2_ref[...], act)
        o_ref[0, 0] = jnp.concatenate([v_t, _ones_row_block((V_ROWS - HEAD_DIM, n))], axis=0).astype(BF16)
    else:
        o_ref[0, 0] = _dot(act, w2_ref[...]).astype(BF16)


def _compress(a, pos, w1, w2, transposed_out):
    b, g, n, width = a.shape
    hidden = w1.shape[1]
    pos2 = pos.reshape(2, width)
    if transposed_out:
        w2p = w2.T.astype(BF16)
        out_block, out_shape = (1, 1, V_ROWS, n), (b, g, V_ROWS, n)
    else:
        w2p = jnp.pad(w2, ((0, 0), (0, LANES - HEAD_DIM))).astype(BF16)
        out_block, out_shape = (1, 1, n, LANES), (b, g, n, LANES)
    return pl.pallas_call(
        functools.partial(_compress_kernel, transposed_out=transposed_out),
        grid=(b, g),
        in_specs=[
            pl.BlockSpec((1, 1, n, width), lambda i, j: (i, j, 0, 0)),
            pl.BlockSpec((2, width), lambda i, j: (0, 0)),
            pl.BlockSpec((2 * width, hidden), lambda i, j: (0, 0)),
            pl.BlockSpec(w2p.shape, lambda i, j: (0, 0)),
        ],
        out_specs=pl.BlockSpec(out_block, lambda i, j: (i, j, 0, 0)),
        out_shape=jax.ShapeDtypeStruct(out_shape, BF16),
        compiler_params=_cparams("arbitrary", "arbitrary"),
        name="nsa_compress_v" if transposed_out else "nsa_compress_k",
    )(a, pos2, w1.astype(BF16), w2p)


def _heads_on_lanes(ref):
    return jnp.concatenate([ref[0, h * HEAD_DIM:(h + 1) * HEAD_DIM, :] for h in range(HEADS_PER_GROUP)], axis=1)


def _nsa_kernel(qc_ref, qr_ref, g_ref, kc_ref, vct_ref, band_ref, ks_ref, vst_ref, kw_ref, vwt_ref,
                o_ref, bias_ref, score_ref, sa_ref, sb_ref, pa_ref, pb_ref):
    i = pl.program_id(2)
    qb = Q_BLOCK
    width = HEADS_PER_GROUP * qb
    n_cmp = kc_ref.shape[2]
    n_slc = score_ref.shape[0]
    t_q = i * qb + lax.broadcasted_iota(I32, (1, qb), 1)
    t_row = jnp.concatenate([t_q] * HEADS_PER_GROUP, axis=1)
    zeros_pad = jnp.zeros((LANES - HEAD_DIM, width), BF16)

    cc = band_ref.shape[1]
    bpc = cc * CMP_STRIDE // SLC_BLOCK
    qc_t = jnp.concatenate([_heads_on_lanes(qc_ref), zeros_pad], axis=0)
    chunk_scores, maxima = [], []
    for c in range(n_cmp // cc):
        s = _dot(kc_ref[0, 0, c * cc:(c + 1) * cc, :], qc_t)
        cmp_end = (c * cc + lax.broadcasted_iota(I32, (cc, 1), 0)) * CMP_STRIDE + (CMP_BLOCK - 1)
        s = jnp.where(cmp_end <= t_row, s, NEG)
        chunk_scores.append(s)
        maxima.append(jnp.maximum(jnp.max(s, axis=0, keepdims=True), NEG * 1e-10))
    m_fin = functools.reduce(jnp.maximum, maxima)
    shares = []
    for c, (s, mx) in enumerate(zip(chunk_scores, maxima)):
        lhs = jnp.concatenate([vct_ref[0, 0, :, c * cc:(c + 1) * cc], band_ref[...]], axis=0)
        share = _dot(lhs, jnp.exp2(s - mx).astype(BF16))
        shares.append(share * jnp.exp2(mx - m_fin))
    acc_c = shares[0][:V_ROWS]
    for sh in shares[1:]:
        acc_c = acc_c + sh[:V_ROWS]
    rinv = 1.0 / jnp.maximum(acc_c[HEAD_DIM:HEAD_DIM + 1], 1e-30)
    o_cmp = acc_c[:HEAD_DIM] * rinv
    imp_rows = []
    for c, sh in enumerate(shares):
        u = sh[V_ROWS:V_ROWS + bpc]
        if c > 0:
            spill = shares[c - 1][V_ROWS + bpc:V_ROWS + bpc + 8]
            u = jnp.concatenate([u[:8] + spill, u[8:]], axis=0)
        u = u * rinv
        imp_c = u[:, 0:qb]
        for h in range(1, HEADS_PER_GROUP):
            imp_c = imp_c + u[:, h * qb:(h + 1) * qb]
        imp_rows.append(imp_c)
    imp = jnp.concatenate(imp_rows, axis=0)

    j_idx = lax.broadcasted_iota(I32, (n_slc, 1), 0)
    cur = t_q // SLC_BLOCK
    valid = j_idx * SLC_BLOCK <= t_q
    forced = (j_idx == 0) | (j_idx == cur) | (j_idx == cur - 1)
    score_ref[...] = jnp.where(valid & jnp.logical_not(forced), imp, -1.0)

    def pick(_, carry, rows):
        sc = score_ref[0:rows, :]
        best = jnp.max(sc, axis=0, keepdims=True)
        idx = jnp.min(jnp.where(sc == best, j_idx[0:rows], n_slc), axis=0, keepdims=True)
        score_ref[0:rows, :] = jnp.where(j_idx[0:rows] == idx, PICKED, sc)
        return carry

    pick_rows = min(PICK_ROWS, n_slc)
    need = (i * qb + qb - 1) // (SLC_BLOCK * pick_rows)
    for r in range(n_slc // pick_rows):
        @pl.when(need == r)
        def _(rows=(r + 1) * pick_rows):
            lax.fori_loop(0, N_SELECT - N_FORCED, functools.partial(pick, rows=rows), 0)
    bias = jnp.where(forced | (score_ref[...] == PICKED), 0.0, NEG).astype(BF16)
    bias_ref[...] = jnp.concatenate([bias] * HEADS_PER_GROUP, axis=1)

    qr_heads = _heads_on_lanes(qr_ref)
    zeros_sel = jnp.zeros((LANES - HEAD_DIM - BIAS_ROWS, width), BF16)
    tk = KEY_TILE
    blocks_per_tile = tk // SLC_BLOCK

    def scores(kt, s_ref, masked):
        b0 = pl.multiple_of((kt * blocks_per_tile) // BIAS_ROWS * BIAS_ROWS, BIAS_ROWS)
        q_aug = jnp.concatenate([qr_heads, bias_ref[pl.ds(b0, BIAS_ROWS), :], zeros_sel], axis=0)
        k0 = pl.multiple_of(kt * tk, tk)
        s = _dot(ks_ref[0, 0, pl.ds(k0, tk), :], q_aug)
        if masked:
            s = jnp.where(k0 + lax.broadcasted_iota(I32, (tk, 1), 0) <= t_row, s, NEG)
        s_ref[...] = s
        part = jnp.max(s.reshape(tk // MAX_ROWS, MAX_ROWS, width), axis=0)
        return jnp.max(part, axis=0, keepdims=True)

    def values(kt, p_ref):
        return _dot(vst_ref[0, 0, jnp.maximum(kt, 0)], p_ref[...])

    def trip(kk, carry, next_scores):
        m, acc, beta, cmax_a, cmax_b = carry
        kt = 2 * kk
        live = jnp.where(kk > 0, 1.0, 0.0)
        acc = acc + live * (beta * values(kt - 2, pa_ref) + values(kt - 1, pb_ref))
        m1 = jnp.maximum(m, cmax_a)
        pa_ref[...] = jnp.exp2((sa_ref[...] - m1).astype(BF16))
        m2 = jnp.maximum(m1, cmax_b)
        pb_ref[...] = jnp.exp2((sb_ref[...] - m2).astype(BF16))
        beta = jnp.exp2(m1 - m2)
        if next_scores is not None:
            cmax_a = scores(kt + 2, sa_ref, next_scores)
            cmax_b = scores(kt + 3, sb_ref, next_scores)
        return m2, (jnp.exp2(m - m1) * beta) * acc, beta, cmax_a, cmax_b

    @pl.when((pl.program_id(0) == 0) & (pl.program_id(1) == 0) & (i == 0))
    def _():
        pa_ref[...] = jnp.zeros_like(pa_ref)
        pb_ref[...] = jnp.zeros_like(pb_ref)

    n_full = (i * qb) // tk
    n_unmasked = jnp.maximum(n_full // 2 - 1, 0)
    carry = (jnp.full((1, width), NEG, F32), jnp.zeros((V_ROWS, width), F32), jnp.ones((1, width), F32),
             scores(0, sa_ref, True), scores(1, sb_ref, True))
    carry = lax.fori_loop(0, n_unmasked, functools.partial(trip, next_scores=False), carry)
    carry = lax.fori_loop(n_unmasked, n_full // 2, functools.partial(trip, next_scores=True), carry)
    qr_t = jnp.concatenate([qr_heads, zeros_pad], axis=0)
    n_win = WINDOW // qb + 1
    s_w = []
    for r in range(n_win):
        j = i - (n_win - 1) + r
        k0 = pl.multiple_of(jnp.maximum(j, 0) * qb, qb)
        s = _dot(kw_ref[0, 0, pl.ds(k0, qb), :], qr_t)
        if r in (0, n_win - 1):
            kpos = j * qb + lax.broadcasted_iota(I32, (qb, 1), 0)
            diff = t_row - kpos
            s = jnp.where((diff >= 0) & (diff < WINDOW) & (kpos >= 0), s, NEG)
        else:
            s = s + jnp.where(j >= 0, 0.0, NEG)
        s_w.append(s)
    m_w = s_w[0].max(axis=0, keepdims=True)
    for r in range(1, n_win):
        m_w = jnp.maximum(m_w, s_w[r].max(axis=0, keepdims=True))

    last = n_full // 2
    _, acc_s, beta, _, _ = trip(last, carry, None)
    acc_s = acc_s + beta * values(2 * last, pa_ref) + values(2 * last + 1, pb_ref)
    o_slc = acc_s[:HEAD_DIM] / acc_s[HEAD_DIM:HEAD_DIM + 1]

    acc_w = jnp.zeros((V_ROWS, width), F32)
    for r in range(n_win):
        jc = jnp.maximum(i - (n_win - 1) + r, 0)
        acc_w = acc_w + _dot(vwt_ref[0, 0, jc], jnp.exp2((s_w[r] - m_w).astype(BF16)))
    o_win = acc_w[:HEAD_DIM] / acc_w[HEAD_DIM:HEAD_DIM + 1]

    gts = g_ref[0]
    for h in range(HEADS_PER_GROUP):
        sl = slice(h * qb, (h + 1) * qb)
        r0 = h * N_BRANCH
        o = (gts[r0:r0 + 1] * o_cmp[:, sl] + gts[r0 + 1:r0 + 2] * o_slc[:, sl] + gts[r0 + 2:r0 + 3] * o_win[:, sl])
        o_ref[0, h * HEAD_DIM:(h + 1) * HEAD_DIM, :] = o.astype(BF16)


def _nsa_attention(qc_t, qr_t, gates_t, k_cmp, v_cmp_t, band, k_slc, v_slc_t, k_win, v_win_t):
    b, qd, s = qc_t.shape
    g = N_KV_GROUPS
    n_cmp = k_cmp.shape[2]
    n_slc = s // SLC_BLOCK
    nt, tk = v_slc_t.shape[2], v_slc_t.shape[4]
    rows = HEADS_PER_GROUP * HEAD_DIM
    width = HEADS_PER_GROUP * Q_BLOCK
    qspec = pl.BlockSpec((1, rows, Q_BLOCK), lambda bi, gi, i: (bi, gi, i))
    per_bg = lambda shape: pl.BlockSpec((1, 1) + shape, lambda bi, gi, i: (bi, gi) + (0,) * len(shape))
    return pl.pallas_call(
        _nsa_kernel,
        grid=(b, g, s // Q_BLOCK),
        in_specs=[
            qspec, qspec,
            pl.BlockSpec((1, 16, Q_BLOCK), lambda bi, gi, i: (bi, gi, i)),
            per_bg((n_cmp, LANES)), per_bg((V_ROWS, n_cmp)),
            pl.BlockSpec(band.shape, lambda bi, gi, i: (0, 0)),
            per_bg((s, LANES)), per_bg((nt, V_ROWS, tk)),
            per_bg((s, LANES)), per_bg((s // Q_BLOCK, V_ROWS, Q_BLOCK)),
        ],
        out_specs=qspec,
        out_shape=jax.ShapeDtypeStruct((b, qd, s), BF16),
        scratch_shapes=[
            pltpu.VMEM((n_slc, width), BF16),
            pltpu.VMEM((n_slc, Q_BLOCK), F32),
            pltpu.VMEM((tk, width), F32), pltpu.VMEM((tk, width), F32),
            pltpu.VMEM((tk, width), BF16), pltpu.VMEM((tk, width), BF16),
        ],
        compiler_params=_cparams("arbitrary", "arbitrary", "arbitrary"),
        name="nsa_attention",
    )(qc_t, qr_t, gates_t, k_cmp, v_cmp_t, band, k_slc, v_slc_t, k_win, v_win_t)


def _selection_band(n_cmp):
    cc = min(CMP_CHUNK, n_cmp)
    bpc = cc * CMP_STRIDE // SLC_BLOCK
    c0 = jnp.arange(cc) * CMP_STRIDE
    j0 = jnp.arange(bpc + 16) * SLC_BLOCK
    ov = jnp.minimum(c0[None, :] + CMP_BLOCK, j0[:, None] + SLC_BLOCK) - jnp.maximum(c0[None, :], j0[:, None])
    return (jnp.clip(ov, 0).astype(F32) / CMP_BLOCK).astype(BF16)


def _out_proj_kernel(x_ref, ot_ref, w_ref, o_ref):
    o_ref[0] = x_ref[0] + _dot_tn(ot_ref[0], w_ref[...])


def _out_proj(x, o_t, w_out, tm=512):
    b, s, d = x.shape
    qd = o_t.shape[1]
    return pl.pallas_call(
        _out_proj_kernel,
        grid=(b, s // tm),
        in_specs=[
            pl.BlockSpec((1, tm, d), lambda i, j: (i, j, 0)),
            pl.BlockSpec((1, qd, tm), lambda i, j: (i, 0, j)),
            pl.BlockSpec((qd, d), lambda i, j: (0, 0)),
        ],
        out_specs=pl.BlockSpec((1, tm, d), lambda i, j: (i, j, 0)),
        out_shape=jax.ShapeDtypeStruct((b, s, d), F32),
        compiler_params=_cparams("arbitrary", "arbitrary"),
        name="nsa_out_proj",
    )(x, o_t, w_out.astype(BF16))


def _nsa_layer(x, positions, g_kv, g_q, w_kv, cmp_pos_k, cmp_w1_k, cmp_w2_k, cmp_pos_v, cmp_w1_v, cmp_w2_v,
               w_in, w_out):
    b, s, d = x.shape
    g = N_KV_GROUPS
    kcv, k_slc, k_win, v_slc_t, v_win_t, qc_t, qr_t, gates_t = _projections(x, positions, g_kv, g_q, w_kv, w_in)
    n_half = s // CMP_STRIDE
    halves = kcv.reshape(b, n_half, CMP_STRIDE, 2, g, HEAD_DIM).transpose(0, 3, 4, 1, 2, 5)
    halves = halves.reshape(b, 2, g, n_half, CMP_STRIDE * HEAD_DIM)
    k_cmp = _compress(halves[:, 0], cmp_pos_k, cmp_w1_k, cmp_w2_k, transposed_out=False)
    v_cmp_t = _compress(halves[:, 1], cmp_pos_v, cmp_w1_v, cmp_w2_v, transposed_out=True)
    o_t = _nsa_attention(qc_t, qr_t, gates_t, k_cmp, v_cmp_t, _selection_band(n_half), k_slc, v_slc_t, k_win,
                         v_win_t)
    return _out_proj(x, o_t, w_out)


def kernel(x, positions, norm_mix, norm_ffn, conv_w_in, conv_w, conv_w_out, norm_kv, w_kv, cmp_pos_k, cmp_w1_k, cmp_w2_k, cmp_pos_v, cmp_w1_v, cmp_w2_v, attn_w_in, attn_w_out, router_group_w, router_group_b, router_expert_w, router_expert_b, expert_w_gate, expert_w_up, expert_w_down, norm_final):
    b, s, d = x.shape
    assert s % (2 * KEY_TILE) == 0 and (s // SLC_BLOCK) % BIAS_ROWS == 0 and WINDOW % Q_BLOCK == 0

    def moe(xs, l, final_norm):
        return _hier_moe(xs.reshape(b * s, d), norm_ffn[l], router_group_w[l], router_group_b[l],
                         router_expert_w[l], router_expert_b[l], expert_w_gate, expert_w_up,
                         expert_w_down, l, norm_final, final_norm).reshape(b, s, d)

    x = _mixer(x, norm_mix[0], conv_w_in[0], conv_w[0], conv_w_out[0])
    x = moe(x, 0, False)
    x = _nsa_layer(x, positions, norm_kv, norm_mix[1], w_kv, cmp_pos_k, cmp_w1_k, cmp_w2_k,
                   cmp_pos_v, cmp_w1_v, cmp_w2_v, attn_w_in[0], attn_w_out[0])
    return moe(x, 1, True)
```

```python
import functools

import jax
import jax.numpy as jnp
from jax import lax
from jax.experimental import pallas as pl
from jax.experimental.pallas import tpu as pltpu

F32 = jnp.float32
BF16 = jnp.bfloat16
I32 = jnp.int32

RMS_EPS = 1e-6
ROPE_THETA = 10000.0
HEAD_DIM = 64
HALF = HEAD_DIM // 2
N_HEADS = 16
N_KV_GROUPS = 4
HEADS_PER_GROUP = N_HEADS // N_KV_GROUPS
N_BRANCH = 3
CMP_BLOCK = 32
CMP_STRIDE = 16
SLC_BLOCK = 64
N_SELECT = 16
WINDOW = 512
Q_BLOCK = 256
FORCE_BONUS = 1000.0
N_GROUPS = 4
EXPERTS_PER_GROUP = 8
N_EXPERTS = N_GROUPS * EXPERTS_PER_GROUP
CHUNK_ROWS = 256
NEG = -1e30
LANES = 128
KEY_TILE = 512
BIAS_ROWS = 16
MAX_ROWS = 64
CMP_CHUNK = 256
N_FORCED = 3
PICK_ROWS = 64
PICKED = -2.0
V_ROWS = 80
LOG2E = 1.4426950408889634
VMEM_LIMIT = 56 * 1024 * 1024


def _cparams(*sem):
    return pltpu.CompilerParams(dimension_semantics=sem, vmem_limit_bytes=VMEM_LIMIT)


def _rms_scale(x):
    return x * lax.rsqrt(jnp.mean(x * x, axis=-1, keepdims=True) + RMS_EPS)


def _dot(a, b):
    return jnp.dot(a, b, preferred_element_type=F32)


def _dot_nt(a, b):
    return lax.dot_general(a, b, (((1,), (1,)), ((), ())), preferred_element_type=F32)


def _dot_tn(a, b):
    return lax.dot_general(a, b, (((0,), (0,)), ((), ())), preferred_element_type=F32)


def _ones_row_block(shape):
    row = lax.broadcasted_iota(I32, shape, len(shape) - 2)
    return jnp.where(row == 0, 1.0, 0.0).astype(F32)


def _mixer_kernel(x_ref, g_ref, win_ref, cw_ref, wout_ref, o_ref, tail_ref):
    d = x_ref.shape[-1]
    tm = x_ref.shape[1]

    @pl.when(pl.program_id(1) == 0)
    def _():
        tail_ref[...] = jnp.zeros_like(tail_ref)

    x = x_ref[0]
    h = (_rms_scale(x) * g_ref[...]).astype(BF16)
    proj = _dot(h, win_ref[...])
    b_gate = proj[:, :d]
    u = proj[:, d:2 * d] * proj[:, 2 * d:]
    row = lax.broadcasted_iota(I32, (tm, 1), 0)
    prev1 = tail_ref[7:8, :]
    prev2 = tail_ref[6:7, :]
    u1 = jnp.where(row >= 1, pltpu.roll(u, 1, 0), prev1)
    u2 = jnp.where(row >= 2, pltpu.roll(u, 2, 0), jnp.where(row == 1, prev1, prev2))
    y = cw_ref[2:3, :] * u + cw_ref[1:2, :] * u1 + cw_ref[0:1, :] * u2
    tail_ref[...] = u[tm - 8:, :]
    z = (b_gate * y).astype(BF16)
    o_ref[0] = x + _dot(z, wout_ref[...])


def _mixer(x, g, w_in, conv_w, w_out, tm=512):
    b, s, d = x.shape
    return pl.pallas_call(
        _mixer_kernel,
        grid=(b, s // tm),
        in_specs=[
            pl.BlockSpec((1, tm, d), lambda i, j: (i, j, 0)),
            pl.BlockSpec((1, d), lambda i, j: (0, 0)),
            pl.BlockSpec((d, 3 * d), lambda i, j: (0, 0)),
            pl.BlockSpec((3, d), lambda i, j: (0, 0)),
            pl.BlockSpec((d, d), lambda i, j: (0, 0)),
        ],
        out_specs=pl.BlockSpec((1, tm, d), lambda i, j: (i, j, 0)),
        out_shape=jax.ShapeDtypeStruct((b, s, d), F32),
        scratch_shapes=[pltpu.VMEM((8, d), F32)],
        compiler_params=_cparams("arbitrary", "arbitrary"),
        name="conv_mixer",
    )(x, g.reshape(1, d), w_in.astype(BF16), conv_w, w_out.astype(BF16))


def _router_kernel(x_ref, g_ref, wt_ref, b_ref, hb_ref, eid_ref, wts_ref, counts_ref, cnt_ref):
    tm = x_ref.shape[0]
    h = _rms_scale(x_ref[...]) * g_ref[...]
    bits = lax.bitcast_convert_type(h.astype(BF16).astype(F32), jnp.uint32)
    half = h.shape[1] // 2
    hb_ref[...] = (bits[:, :half] >> 16) | (bits[:, half:] & jnp.uint32(0xFFFF0000))
    w = wt_ref[...]
    w_hi = w.astype(BF16)
    w_lo = (w - w_hi.astype(F32)).astype(BF16)
    h_hi = h.astype(BF16)
    h_lo = (h - h_hi.astype(F32)).astype(BF16)
    lt = _dot_nt(w_hi, h_hi) + _dot_nt(w_hi, h_lo) + _dot_nt(w_lo, h_hi) + b_ref[...]
    r8 = lax.broadcasted_iota(I32, (8, 1), 0)
    gl = jnp.where(r8 < N_GROUPS, lt[0:8], -jnp.inf)
    gmax = jnp.max(gl, axis=0, keepdims=True)
    gtop = jnp.min(jnp.where(gl == gmax, r8, 8), axis=0, keepdims=True)
    p_g = 1.0 / jnp.sum(jnp.exp(gl - gmax), axis=0, keepdims=True)
    chosen = lt[8:16]
    for k in range(1, N_GROUPS):
        chosen = jnp.where(gtop == k, lt[8 + 8 * k:16 + 8 * k], chosen)
    v0 = jnp.max(chosen, axis=0, keepdims=True)
    i0 = jnp.min(jnp.where(chosen == v0, r8, 8), axis=0, keepdims=True)
    rest = jnp.where(r8 == i0, -jnp.inf, chosen)
    v1 = jnp.max(rest, axis=0, keepdims=True)
    i1 = jnp.min(jnp.where(rest == v1, r8, 8), axis=0, keepdims=True)
    e = jnp.exp(v1 - v0)
    w0 = p_g / (1.0 + e)
    w1 = p_g * e / (1.0 + e)
    e0 = gtop * EXPERTS_PER_GROUP + i0
    e1 = gtop * EXPERTS_PER_GROUP + i1
    wts_ref[...] = jnp.where(r8 == 0, w0, jnp.where(r8 == 1, w1, 0.0))

    @pl.when(pl.program_id(0) == 0)
    def _():
        cnt_ref[...] = jnp.zeros_like(cnt_ref)

    rexp = lax.broadcasted_iota(I32, (N_EXPERTS, 1), 0)
    tri = (lax.broadcasted_iota(I32, (tm, tm), 0) <= lax.broadcasted_iota(I32, (tm, tm), 1)).astype(BF16)
    oh0 = rexp == e0
    oh1 = rexp == e1
    cum0 = _dot(oh0.astype(BF16), tri)
    cum1 = _dot(oh1.astype(BF16), tri)
    before0 = cnt_ref[...]
    before1 = before0 + cum0[:, tm - 1:tm]
    rank0 = jnp.sum(jnp.where(oh0, before0 + cum0, 0.0), axis=0, keepdims=True) - 1.0
    rank1 = jnp.sum(jnp.where(oh1, before1 + cum1, 0.0), axis=0, keepdims=True) - 1.0
    cnt_ref[...] = before1 + cum1[:, tm - 1:tm]
    counts_ref[...] = jnp.broadcast_to(cnt_ref[...], counts_ref.shape).astype(I32)
    eid_ref[...] = jnp.where(r8 == 0, e0, jnp.where(r8 == 1, e1, jnp.where(
        r8 == 2, rank0.astype(I32), jnp.where(r8 == 3, rank1.astype(I32), 0))))


def _router(xf, g, w_rg, b_rg, w_re, b_re, tm=512):
    n, d = xf.shape
    rows = 8 + N_EXPERTS
    wt = jnp.zeros((rows, d), F32).at[:N_GROUPS].set(w_rg.T).at[8:].set(w_re.T)
    bt = jnp.zeros((rows, 1), F32).at[:N_GROUPS, 0].set(b_rg).at[8:, 0].set(b_re)
    return pl.pallas_call(
        _router_kernel,
        grid=(n // tm,),
        in_specs=[
            pl.BlockSpec((tm, d), lambda i: (i, 0)),
            pl.BlockSpec((1, d), lambda i: (0, 0)),
            pl.BlockSpec((rows, d), lambda i: (0, 0)),
            pl.BlockSpec((rows, 1), lambda i: (0, 0)),
        ],
        out_specs=[
            pl.BlockSpec((tm, d // 2), lambda i: (i, 0)),
            pl.BlockSpec((8, tm), lambda i: (0, i)),
            pl.BlockSpec((8, tm), lambda i: (0, i)),
            pl.BlockSpec((N_EXPERTS, LANES), lambda i: (0, 0)),
        ],
        out_shape=[
            jax.ShapeDtypeStruct((n, d // 2), jnp.uint32),
            jax.ShapeDtypeStruct((8, n), I32),
            jax.ShapeDtypeStruct((8, n), F32),
            jax.ShapeDtypeStruct((N_EXPERTS, LANES), I32),
        ],
        scratch_shapes=[pltpu.VMEM((N_EXPERTS, 1), F32)],
        compiler_params=_cparams("arbitrary"),
        name="moe_router",
    )(xf, g.reshape(1, d), wt, bt)


def _take_rows(a, idx):
    return a.at[idx].get(mode="promise_in_bounds")


def _dispatch_plan(eid, counts, n_tok):
    a_exp = eid[0:2].reshape(-1)
    rank = eid[2:4].reshape(-1)
    n_asg = a_exp.shape[0]
    a_tok = jnp.arange(n_asg, dtype=I32) % n_tok
    padded = (counts + CHUNK_ROWS - 1) // CHUNK_ROWS * CHUNK_ROWS
    pend = jnp.cumsum(padded)
    pstart = pend - padded
    dest = (_take_rows(pstart, a_exp) + rank).astype(I32)
    n_chunks = -(-n_asg // CHUNK_ROWS) + N_EXPERTS
    row_tok = jnp.zeros((n_chunks * CHUNK_ROWS,), I32).at[dest].set(
        a_tok, unique_indices=True, mode="promise_in_bounds")
    chunk_row0 = jnp.arange(n_chunks, dtype=I32) * CHUNK_ROWS
    chunk_exp = jnp.minimum(jnp.sum((pend[None, :] <= chunk_row0[:, None]).astype(I32), axis=1), N_EXPERTS - 1)
    return row_tok, dest.reshape(2, n_tok), chunk_exp


def _expert_kernel(ce_ref, rt_ref, hb_ref, wg_ref, wu_ref, wd_ref, o_ref, wg_bf, wu_bf, wd_bf, xbuf, xc_ref, sem):
    c = pl.program_id(0)
    last = pl.num_programs(0) - 1
    slot = c % 2
    nw = hb_ref.shape[1]

    def start_gather(chunk, to_slot):
        for r in range(CHUNK_ROWS):
            tok = rt_ref[chunk * CHUNK_ROWS + r]
            pltpu.make_async_copy(hb_ref.at[tok], xbuf.at[to_slot, pl.ds(r * nw, nw), :], sem.at[to_slot]).start()

    def wait_gather(of_slot):
        pltpu.make_async_copy(xbuf.at[of_slot], xbuf.at[of_slot], sem.at[of_slot]).wait()

    @pl.when(c == 0)
    def _():
        start_gather(0, 0)

    @pl.when((c == 0) | (ce_ref[c] != ce_ref[jnp.maximum(c - 1, 0)]))
    def _():
        wg_bf[...] = wg_ref[0, 0].astype(BF16)
        wu_bf[...] = wu_ref[0, 0].astype(BF16)
        wd_bf[...] = wd_ref[0, 0].astype(BF16)

    wait_gather(slot)
    words = [xbuf[slot, pl.ds(j, CHUNK_ROWS, stride=nw), :] for j in range(nw)]
    lo = [lax.bitcast_convert_type(w << 16, F32) for w in words]
    hi = [lax.bitcast_convert_type(w & jnp.uint32(0xFFFF0000), F32) for w in words]
    xc_ref[...] = jnp.concatenate(lo + hi, axis=1).astype(BF16)
    start_gather(jnp.minimum(c + 1, last), 1 - slot)
    xc = xc_ref[...]
    gate = _dot(xc, wg_bf[...])
    up = _dot(xc, wu_bf[...])
    act = (gate * jax.nn.sigmoid(gate) * up).astype(BF16)
    o_ref[...] = _dot(act, wd_bf[...])

    @pl.when(c == last)
    def _():
        wait_gather(1 - slot)


def _experts(hb, row_tok, chunk_exp, w_gate, w_up, w_down, layer):
    d = w_gate.shape[-2]
    de = w_gate.shape[-1]
    n_chunks = chunk_exp.shape[0]
    weight = lambda shape: pl.BlockSpec((1, 1) + shape, lambda c, ce, rt: (layer, ce[c], 0, 0))
    grid_spec = pltpu.PrefetchScalarGridSpec(
        num_scalar_prefetch=2,
        grid=(n_chunks,),
        in_specs=[pl.BlockSpec(memory_space=pl.ANY), weight((d, de)), weight((d, de)), weight((de, d))],
        out_specs=pl.BlockSpec((CHUNK_ROWS, d), lambda c, ce, rt: (c, 0)),
        scratch_shapes=[pltpu.VMEM((d, de), BF16), pltpu.VMEM((d, de), BF16), pltpu.VMEM((de, d), BF16),
                        pltpu.VMEM((2, CHUNK_ROWS * hb.shape[1], LANES), jnp.uint32),
                        pltpu.VMEM((CHUNK_ROWS, d), BF16),
                        pltpu.SemaphoreType.DMA((2,))],
    )
    return pl.pallas_call(
        _expert_kernel,
        grid_spec=grid_spec,
        out_shape=jax.ShapeDtypeStruct((n_chunks * CHUNK_ROWS, d), F32),
        compiler_params=_cparams("arbitrary"),
        name="moe_experts",
    )(chunk_exp, row_tok, hb, w_gate, w_up, w_down)


def _combine_kernel(x_ref, y0_ref, y1_ref, w_ref, g_ref, o_ref, *, final_norm):
    w = w_ref[...]
    out = x_ref[...] + w[:, 0:1] * y0_ref[...] + w[:, 1:2] * y1_ref[...]
    if final_norm:
        out = _rms_scale(out) * g_ref[...]
    o_ref[...] = out


def _combine(xf, y0, y1, wcol, g, final_norm, tm=512):
    n, d = xf.shape
    row = pl.BlockSpec((tm, d), lambda i: (i, 0))
    return pl.pallas_call(
        functools.partial(_combine_kernel, final_norm=final_norm),
        grid=(n // tm,),
        in_specs=[row, row, row, pl.BlockSpec((tm, 8), lambda i: (i, 0)), pl.BlockSpec((1, d), lambda i: (0, 0))],
        out_specs=row,
        out_shape=jax.ShapeDtypeStruct((n, d), F32),
        compiler_params=_cparams("arbitrary"),
        name="moe_combine",
    )(xf, y0, y1, wcol, g.reshape(1, d))


def _hier_moe(xf, g_ffn, w_rg, b_rg, w_re, b_re, w_gate, w_up, w_down, layer, g_final, final_norm):
    n = xf.shape[0]
    hb, eid, wts, counts = _router(xf, g_ffn, w_rg, b_rg, w_re, b_re)
    row_tok, dest, chunk_exp = _dispatch_plan(eid, counts[:, 0], n)
    ys = _experts(hb.reshape(n, -1, LANES), row_tok, chunk_exp, w_gate, w_up, w_down, layer)
    return _combine(xf, _take_rows(ys, dest[0]), _take_rows(ys, dest[1]), wts.T, g_final, final_norm)


def _proj_kernel(x_ref, prow_ref, pcol_ref, gkv_ref, gq_ref, wnat_ref, wvt_ref, wqt_ref,
                 kcv_ref, ks_ref, kw_ref, vst_ref, vwt_ref, qc_ref, qr_ref, gt_ref):
    tm = x_ref.shape[1]
    gd = N_KV_GROUPS * HEAD_DIM
    qd = N_HEADS * HEAD_DIM
    y = _rms_scale(x_ref[0])
    hkv = (y * gkv_ref[...]).astype(BF16)
    hq = (y * gq_ref[...]).astype(BF16)

    nat = _dot(hkv, wnat_ref[...])
    kcv_ref[0] = nat[:, :2 * gd]
    lane = lax.broadcasted_iota(I32, (1, LANES), 1)
    inv = jnp.exp((lane % HALF).astype(F32) * (-jnp.log(ROPE_THETA) / HALF))
    ang = pcol_ref[0].astype(F32) * inv
    ctab = jnp.where(lane < HEAD_DIM, jnp.cos(ang), 0.0)
    stab = jnp.where(lane < HEAD_DIM, jnp.sin(ang), 0.0)
    key = pl.program_id(1) * tm + lax.broadcasted_iota(I32, (tm, 1), 0)
    blk = (key // SLC_BLOCK) % BIAS_ROWS
    onehot = jnp.where(lane == HEAD_DIM + blk, 1.0, 0.0)
    for g in range(N_KV_GROUPS):
        for slot, ref in ((0, ks_ref), (1, kw_ref)):
            lo = 2 * gd + (slot * N_KV_GROUPS + g) * LANES
            t = nat[:, lo:lo + LANES]
            roped = t * ctab + pltpu.roll(t, HEAD_DIM, 1) * stab
            if slot == 0:
                roped = roped + onehot
            ref[0, g] = roped.astype(BF16)

    vt = _dot_nt(wvt_ref[...], hkv)
    ones_rows = _ones_row_block((N_KV_GROUPS, V_ROWS - HEAD_DIM, tm))
    vs3 = jnp.concatenate([vt[:gd].reshape(N_KV_GROUPS, HEAD_DIM, tm), ones_rows], axis=1).astype(BF16)
    vw3 = jnp.concatenate([vt[gd:].reshape(N_KV_GROUPS, HEAD_DIM, tm), ones_rows], axis=1).astype(BF16)
    vst_ref[0, :, 0] = vs3
    for w in range(tm // Q_BLOCK):
        vwt_ref[0, :, w] = vw3[:, :, w * Q_BLOCK:(w + 1) * Q_BLOCK]

    qt = _dot_nt(wqt_ref[...], hq)
    scale = HEAD_DIM ** -0.5 * LOG2E
    q3 = qt[:qd].reshape(N_HEADS, HEAD_DIM, tm)
    qc_ref[0] = (qt[:qd] * scale).astype(BF16)
    frq = lax.broadcasted_iota(I32, (HALF, 1), 0).astype(F32)
    ang_t = jnp.exp(frq * (-jnp.log(ROPE_THETA) / HALF)) * prow_ref[0].astype(F32)
    cos_t = jnp.cos(ang_t)[None]
    sin_t = jnp.sin(ang_t)[None]
    t1 = q3[:, :HALF]
    t2 = q3[:, HALF:]
    qr = jnp.concatenate([t1 * cos_t - t2 * sin_t, t2 * cos_t + t1 * sin_t], axis=1)
    qr_ref[0] = (qr * scale).reshape(qd, tm).astype(BF16)
    gt_ref[0] = jax.nn.sigmoid(qt[qd:])


def _rot_half_cols(w):
    return jnp.concatenate([-w[:, HALF:], w[:, :HALF]], axis=1)


def _projections(x, positions, g_kv, g_q, w_kv, w_q, tm=KEY_TILE):
    b, s, d = x.shape
    gd = N_KV_GROUPS * HEAD_DIM
    qd = N_HEADS * HEAD_DIM
    slot = lambda i: w_kv[:, i * gd:(i + 1) * gd]
    aug = []
    for sl in (2, 4):
        for g in range(N_KV_GROUPS):
            wk = slot(sl)[:, g * HEAD_DIM:(g + 1) * HEAD_DIM]
            aug += [wk, _rot_half_cols(wk)]
    w_nat = jnp.concatenate([slot(0), slot(1)] + aug, axis=1).astype(BF16)
    w_vt = jnp.concatenate([slot(3), slot(5)], axis=1).T.astype(BF16)
    gate_cols = w_q[:, qd:].reshape(d, N_KV_GROUPS, HEADS_PER_GROUP * N_BRANCH)
    gate_cols = jnp.pad(gate_cols, ((0, 0), (0, 0), (0, 16 - HEADS_PER_GROUP * N_BRANCH))).reshape(d, 64)
    w_qt = jnp.concatenate([w_q[:, :qd], gate_cols], axis=1).T.astype(BF16)
    nt = s // tm
    n_nat = w_nat.shape[1]
    const = lambda shape: pl.BlockSpec(shape, lambda i, j: (0,) * len(shape))
    return pl.pallas_call(
        _proj_kernel,
        grid=(b, nt),
        in_specs=[
            pl.BlockSpec((1, tm, d), lambda i, j: (i, j, 0)),
            pl.BlockSpec((1, 1, tm), lambda i, j: (i, 0, j)),
            pl.BlockSpec((1, tm, 1), lambda i, j: (i, j, 0)),
            const((1, d)), const((1, d)), const((d, n_nat)), const((2 * gd, d)), const((qd + 64, d)),
        ],
        out_specs=[
            pl.BlockSpec((1, tm, 2 * gd), lambda i, j: (i, j, 0)),
            pl.BlockSpec((1, N_KV_GROUPS, tm, LANES), lambda i, j: (i, 0, j, 0)),
            pl.BlockSpec((1, N_KV_GROUPS, tm, LANES), lambda i, j: (i, 0, j, 0)),
            pl.BlockSpec((1, N_KV_GROUPS, 1, V_ROWS, tm), lambda i, j: (i, 0, j, 0, 0)),
            pl.BlockSpec((1, N_KV_GROUPS, tm // Q_BLOCK, V_ROWS, Q_BLOCK), lambda i, j: (i, 0, j, 0, 0)),
            pl.BlockSpec((1, qd, tm), lambda i, j: (i, 0, j)),
            pl.BlockSpec((1, qd, tm), lambda i, j: (i, 0, j)),
            pl.BlockSpec((1, 64, tm), lambda i, j: (i, 0, j)),
        ],
        out_shape=[
            jax.ShapeDtypeStruct((b, s, 2 * gd), F32),
            jax.ShapeDtypeStruct((b, N_KV_GROUPS, s, LANES), BF16),
            jax.ShapeDtypeStruct((b, N_KV_GROUPS, s, LANES), BF16),
            jax.ShapeDtypeStruct((b, N_KV_GROUPS, nt, V_ROWS, tm), BF16),
            jax.ShapeDtypeStruct((b, N_KV_GROUPS, s // Q_BLOCK, V_ROWS, Q_BLOCK), BF16),
            jax.ShapeDtypeStruct((b, qd, s), BF16),
            jax.ShapeDtypeStruct((b, qd, s), BF16),
            jax.ShapeDtypeStruct((b, 64, s), F32),
        ],
        compiler_params=_cparams("arbitrary", "arbitrary"),
        name="nsa_projections",
    )(x, positions.reshape(b, 1, s), positions.reshape(b, s, 1), g_kv.reshape(1, d), g_q.reshape(1, d),
      w_nat, w_vt, w_qt)


def _compress_kernel(a_ref, pos_ref, w1_ref, w2_ref, o_ref, *, transposed_out):
    half = w1_ref.shape[0] // 2
    a = a_ref[0, 0]
    top = (a + pos_ref[0:1, :]).astype(BF16)
    bot = (a + pos_ref[1:2, :]).astype(BF16)
    p = _dot(top, w1_ref[:half, :])
    q = _dot(bot, w1_ref[half:, :])
    n = a.shape[0]
    hidden = p + pltpu.roll(q, n - 1, 0)
    act = jax.nn.gelu(hidden, approximate=True).astype(BF16)
    if transposed_out:
        v_t = _dot_nt(w2_ref[...], act)
        o_ref[0, 0] = jnp.concatenate([v_t, _ones_row_block((V_ROWS - HEAD_DIM, n))], axis=0).astype(BF16)
    else:
        o_ref[0, 0] = _dot(act, w2_ref[...]).astype(BF16)


def _compress(a, pos, w1, w2, transposed_out):
    b, g, n, width = a.shape
    hidden = w1.shape[1]
    pos2 = pos.reshape(2, width)
    if transposed_out:
        w2p = w2.T.astype(BF16)
        out_block, out_shape = (1, 1, V_ROWS, n), (b, g, V_ROWS, n)
    else:
        w2p = jnp.pad(w2, ((0, 0), (0, LANES - HEAD_DIM))).astype(BF16)
        out_block, out_shape = (1, 1, n, LANES), (b, g, n, LANES)
    return pl.pallas_call(
        functools.partial(_compress_kernel, transposed_out=transposed_out),
        grid=(b, g),
        in_specs=[
            pl.BlockSpec((1, 1, n, width), lambda i, j: (i, j, 0, 0)),
            pl.BlockSpec((2, width), lambda i, j: (0, 0)),
            pl.BlockSpec((2 * width, hidden), lambda i, j: (0, 0)),
            pl.BlockSpec(w2p.shape, lambda i, j: (0, 0)),
        ],
        out_specs=pl.BlockSpec(out_block, lambda i, j: (i, j, 0, 0)),
        out_shape=jax.ShapeDtypeStruct(out_shape, BF16),
        compiler_params=_cparams("arbitrary", "arbitrary"),
        name="nsa_compress_v" if transposed_out else "nsa_compress_k",
    )(a, pos2, w1.astype(BF16), w2p)


def _heads_on_lanes(ref):
    return jnp.concatenate([ref[0, h * HEAD_DIM:(h + 1) * HEAD_DIM, :] for h in range(HEADS_PER_GROUP)], axis=1)


def _nsa_kernel(qc_ref, qr_ref, g_ref, kc_ref, vct_ref, band_ref, ks_ref, vst_ref, kw_ref, vwt_ref,
                o_ref, bias_ref, score_ref, sa_ref, sb_ref, pa_ref, pb_ref):
    i = pl.program_id(2)
    qb = Q_BLOCK
    width = HEADS_PER_GROUP * qb
    n_cmp = kc_ref.shape[2]
    n_slc = score_ref.shape[0]
    t_q = i * qb + lax.broadcasted_iota(I32, (1, qb), 1)
    t_row = jnp.concatenate([t_q] * HEADS_PER_GROUP, axis=1)
    zeros_pad = jnp.zeros((LANES - HEAD_DIM, width), BF16)

    cc = band_ref.shape[1]
    bpc = cc * CMP_STRIDE // SLC_BLOCK
    qc_t = jnp.concatenate([_heads_on_lanes(qc_ref), zeros_pad], axis=0)
    chunk_scores, maxima = [], []
    for c in range(n_cmp // cc):
        s = _dot(kc_ref[0, 0, c * cc:(c + 1) * cc, :], qc_t)
        cmp_end = (c * cc + lax.broadcasted_iota(I32, (cc, 1), 0)) * CMP_STRIDE + (CMP_BLOCK - 1)
        s = jnp.where(cmp_end <= t_row, s, NEG)
        chunk_scores.append(s)
        maxima.append(jnp.maximum(jnp.max(s, axis=0, keepdims=True), NEG * 1e-10))
    m_fin = functools.reduce(jnp.maximum, maxima)
    shares = []
    for c, (s, mx) in enumerate(zip(chunk_scores, maxima)):
        lhs = jnp.concatenate([vct_ref[0, 0, :, c * cc:(c + 1) * cc], band_ref[...]], axis=0)
        share = _dot(lhs, jnp.exp2(s - mx).astype(BF16))
        shares.append(share * jnp.exp2(mx - m_fin))
    acc_c = shares[0][:V_ROWS]
    for sh in shares[1:]:
        acc_c = acc_c + sh[:V_ROWS]
    rinv = 1.0 / jnp.maximum(acc_c[HEAD_DIM:HEAD_DIM + 1], 1e-30)
    o_cmp = acc_c[:HEAD_DIM] * rinv
    imp_rows = []
    for c, sh in enumerate(shares):
        u = sh[V_ROWS:V_ROWS + bpc]
        if c > 0:
            spill = shares[c - 1][V_ROWS + bpc:V_ROWS + bpc + 8]
            u = jnp.concatenate([u[:8] + spill, u[8:]], axis=0)
        u = u * rinv
        imp_c = u[:, 0:qb]
        for h in range(1, HEADS_PER_GROUP):
            imp_c = imp_c + u[:, h * qb:(h + 1) * qb]
        imp_rows.append(imp_c)
    imp = jnp.concatenate(imp_rows, axis=0)

    j_idx = lax.broadcasted_iota(I32, (n_slc, 1), 0)
    cur = t_q // SLC_BLOCK
    valid = j_idx * SLC_BLOCK <= t_q
    forced = (j_idx == 0) | (j_idx == cur) | (j_idx == cur - 1)
    score_ref[...] = jnp.where(valid & jnp.logical_not(forced), imp, -1.0)

    def pick(_, carry, rows):
        sc = score_ref[0:rows, :]
        best = jnp.max(sc, axis=0, keepdims=True)
        idx = jnp.min(jnp.where(sc == best, j_idx[0:rows], n_slc), axis=0, keepdims=True)
        score_ref[0:rows, :] = jnp.where(j_idx[0:rows] == idx, PICKED, sc)
        return carry

    pick_rows = min(PICK_ROWS, n_slc)
    need = (i * qb + qb - 1) // (SLC_BLOCK * pick_rows)
    for r in range(n_slc // pick_rows):
        @pl.when(need == r)
        def _(rows=(r + 1) * pick_rows):
            lax.fori_loop(0, N_SELECT - N_FORCED, functools.partial(pick, rows=rows), 0)
    bias = jnp.where(forced | (score_ref[...] == PICKED), 0.0, NEG).astype(BF16)
    bias_ref[...] = jnp.concatenate([bias] * HEADS_PER_GROUP, axis=1)

    qr_heads = _heads_on_lanes(qr_ref)
    zeros_sel = jnp.zeros((LANES - HEAD_DIM - BIAS_ROWS, width), BF16)
    tk = KEY_TILE
    blocks_per_tile = tk // SLC_BLOCK

    def scores(kt, s_ref, masked):
        b0 = pl.multiple_of((kt * blocks_per_tile) // BIAS_ROWS * BIAS_ROWS, BIAS_ROWS)
        q_aug = jnp.concatenate([qr_heads, bias_ref[pl.ds(b0, BIAS_ROWS), :], zeros_sel], axis=0)
        k0 = pl.multiple_of(kt * tk, tk)
        s = _dot(ks_ref[0, 0, pl.ds(k0, tk), :], q_aug)
        if masked:
            s = jnp.where(k0 + lax.broadcasted_iota(I32, (tk, 1), 0) <= t_row, s, NEG)
        s_ref[...] = s
        part = jnp.max(s.reshape(tk // MAX_ROWS, MAX_ROWS, width), axis=0)
        return jnp.max(part, axis=0, keepdims=True)

    def values(kt, p_ref):
        return _dot(vst_ref[0, 0, jnp.maximum(kt, 0)], p_ref[...])

    def trip(kk, carry, next_scores):
        m, acc, beta, cmax_a, cmax_b = carry
        kt = 2 * kk
        live = jnp.where(kk > 0, 1.0, 0.0)
        acc = acc + live * (beta * values(kt - 2, pa_ref) + values(kt - 1, pb_ref))
        m1 = jnp.maximum(m, cmax_a)
        pa_ref[...] = jnp.exp2((sa_ref[...] - m1).astype(BF16))
        m2 = jnp.maximum(m1, cmax_b)
        pb_ref[...] = jnp.exp2((sb_ref[...] - m2).astype(BF16))
        beta = jnp.exp2(m1 - m2)
        if next_scores is not None:
            cmax_a = scores(kt + 2, sa_ref, next_scores)
            cmax_b = scores(kt + 3, sb_ref, next_scores)
        return m2, (jnp.exp2(m - m1) * beta) * acc, beta, cmax_a, cmax_b

    @pl.when((pl.program_id(0) == 0) & (pl.program_id(1) == 0) & (i == 0))
    def _():
        pa_ref[...] = jnp.zeros_like(pa_ref)
        pb_ref[...] = jnp.zeros_like(pb_ref)

    n_full = (i * qb) // tk
    n_unmasked = jnp.maximum(n_full // 2 - 1, 0)
    carry = (jnp.full((1, width), NEG, F32), jnp.zeros((V_ROWS, width), F32), jnp.ones((1, width), F32),
             scores(0, sa_ref, True), scores(1, sb_ref, True))
    carry = lax.fori_loop(0, n_unmasked, functools.partial(trip, next_scores=False), carry)
    carry = lax.fori_loop(n_unmasked, n_full // 2, functools.partial(trip, next_scores=True), carry)
    qr_t = jnp.concatenate([qr_heads, zeros_pad], axis=0)
    n_win = WINDOW // qb + 1
    s_w = []
    for r in range(n_win):
        j = i - (n_win - 1) + r
        k0 = pl.multiple_of(jnp.maximum(j, 0) * qb, qb)
        s = _dot(kw_ref[0, 0, pl.ds(k0, qb), :], qr_t)
        if r in (0, n_win - 1):
            kpos = j * qb + lax.broadcasted_iota(I32, (qb, 1), 0)
            diff = t_row - kpos
            s = jnp.where((diff >= 0) & (diff < WINDOW) & (kpos >= 0), s, NEG)
        else:
            s = s + jnp.where(j >= 0, 0.0, NEG)
        s_w.append(s)
    m_w = s_w[0].max(axis=0, keepdims=True)
    for r in range(1, n_win):
        m_w = jnp.maximum(m_w, s_w[r].max(axis=0, keepdims=True))

    last = n_full // 2
    _, acc_s, beta, _, _ = trip(last, carry, None)
    acc_s = acc_s + beta * values(2 * last, pa_ref) + values(2 * last + 1, pb_ref)
    o_slc = acc_s[:HEAD_DIM] / acc_s[HEAD_DIM:HEAD_DIM + 1]

    acc_w = jnp.zeros((V_ROWS, width), F32)
    for r in range(n_win):
        jc = jnp.maximum(i - (n_win - 1) + r, 0)
        acc_w = acc_w + _dot(vwt_ref[0, 0, jc], jnp.exp2((s_w[r] - m_w).astype(BF16)))
    o_win = acc_w[:HEAD_DIM] / acc_w[HEAD_DIM:HEAD_DIM + 1]

    gts = g_ref[0]
    for h in range(HEADS_PER_GROUP):
        sl = slice(h * qb, (h + 1) * qb)
        r0 = h * N_BRANCH
        o = (gts[r0:r0 + 1] * o_cmp[:, sl] + gts[r0 + 1:r0 + 2] * o_slc[:, sl] + gts[r0 + 2:r0 + 3] * o_win[:, sl])
        o_ref[0, h * HEAD_DIM:(h + 1) * HEAD_DIM, :] = o.astype(BF16)


def _nsa_attention(qc_t, qr_t, gates_t, k_cmp, v_cmp_t, band, k_slc, v_slc_t, k_win, v_win_t):
    b, qd, s = qc_t.shape
    g = N_KV_GROUPS
    n_cmp = k_cmp.shape[2]
    n_slc = s // SLC_BLOCK
    nt, tk = v_slc_t.shape[2], v_slc_t.shape[4]
    rows = HEADS_PER_GROUP * HEAD_DIM
    width = HEADS_PER_GROUP * Q_BLOCK
    qspec = pl.BlockSpec((1, rows, Q_BLOCK), lambda bi, gi, i: (bi, gi, i))
    per_bg = lambda shape: pl.BlockSpec((1, 1) + shape, lambda bi, gi, i: (bi, gi) + (0,) * len(shape))
    return pl.pallas_call(
        _nsa_kernel,
        grid=(b, g, s // Q_BLOCK),
        in_specs=[
            qspec, qspec,
            pl.BlockSpec((1, 16, Q_BLOCK), lambda bi, gi, i: (bi, gi, i)),
            per_bg((n_cmp, LANES)), per_bg((V_ROWS, n_cmp)),
            pl.BlockSpec(band.shape, lambda bi, gi, i: (0, 0)),
            per_bg((s, LANES)), per_bg((nt, V_ROWS, tk)),
            per_bg((s, LANES)), per_bg((s // Q_BLOCK, V_ROWS, Q_BLOCK)),
        ],
        out_specs=qspec,
        out_shape=jax.ShapeDtypeStruct((b, qd, s), BF16),
        scratch_shapes=[
            pltpu.VMEM((n_slc, width), BF16),
            pltpu.VMEM((n_slc, Q_BLOCK), F32),
            pltpu.VMEM((tk, width), F32), pltpu.VMEM((tk, width), F32),
            pltpu.VMEM((tk, width), BF16), pltpu.VMEM((tk, width), BF16),
        ],
        compiler_params=_cparams("arbitrary", "arbitrary", "arbitrary"),
        name="nsa_attention",
    )(qc_t, qr_t, gates_t, k_cmp, v_cmp_t, band, k_slc, v_slc_t, k_win, v_win_t)


def _selection_band(n_cmp):
    cc = min(CMP_CHUNK, n_cmp)
    bpc = cc * CMP_STRIDE // SLC_BLOCK
    c0 = jnp.arange(cc) * CMP_STRIDE
    j0 = jnp.arange(bpc + 16) * SLC_BLOCK
    ov = jnp.minimum(c0[None, :] + CMP_BLOCK, j0[:, None] + SLC_BLOCK) - jnp.maximum(c0[None, :], j0[:, None])
    return (jnp.clip(ov, 0).astype(F32) / CMP_BLOCK).astype(BF16)


def _out_proj_kernel(x_ref, ot_ref, w_ref, o_ref):
    o_ref[0] = x_ref[0] + _dot_tn(ot_ref[0], w_ref[...])


def _out_proj(x, o_t, w_out, tm=512):
    b, s, d = x.shape
    qd = o_t.shape[1]
    return pl.pallas_call(
        _out_proj_kernel,
        grid=(b, s // tm),
        in_specs=[
            pl.BlockSpec((1, tm, d), lambda i, j: (i, j, 0)),
            pl.BlockSpec((1, qd, tm), lambda i, j: (i, 0, j)),
            pl.BlockSpec((qd, d), lambda i, j: (0, 0)),
        ],
        out_specs=pl.BlockSpec((1, tm, d), lambda i, j: (i, j, 0)),
        out_shape=jax.ShapeDtypeStruct((b, s, d), F32),
        compiler_params=_cparams("arbitrary", "arbitrary"),
        name="nsa_out_proj",
    )(x, o_t, w_out.astype(BF16))


def _nsa_layer(x, positions, g_kv, g_q, w_kv, cmp_pos_k, cmp_w1_k, cmp_w2_k, cmp_pos_v, cmp_w1_v, cmp_w2_v,
               w_in, w_out):
    b, s, d = x.shape
    g = N_KV_GROUPS
    kcv, k_slc, k_win, v_slc_t, v_win_t, qc_t, qr_t, gates_t = _projections(x, positions, g_kv, g_q, w_kv, w_in)
    n_half = s // CMP_STRIDE
    halves = kcv.reshape(b, n_half, CMP_STRIDE, 2, g, HEAD_DIM).transpose(0, 3, 4, 1, 2, 5)
    halves = halves.reshape(b, 2, g, n_half, CMP_STRIDE * HEAD_DIM)
    k_cmp = _compress(halves[:, 0], cmp_pos_k, cmp_w1_k, cmp_w2_k, transposed_out=False)
    v_cmp_t = _compress(halves[:, 1], cmp_pos_v, cmp_w1_v, cmp_w2_v, transposed_out=True)
    o_t = _nsa_attention(qc_t, qr_t, gates_t, k_cmp, v_cmp_t, _selection_band(n_half), k_slc, v_slc_t, k_win,
                         v_win_t)
    return _out_proj(x, o_t, w_out)


def kernel(x, positions, norm_mix, norm_ffn, conv_w_in, conv_w, conv_w_out, norm_kv, w_kv, cmp_pos_k, cmp_w1_k, cmp_w2_k, cmp_pos_v, cmp_w1_v, cmp_w2_v, attn_w_in, attn_w_out, router_group_w, router_group_b, router_expert_w, router_expert_b, expert_w_gate, expert_w_up, expert_w_down, norm_final):
    b, s, d = x.shape
    assert s % (2 * KEY_TILE) == 0 and (s // SLC_BLOCK) % BIAS_ROWS == 0 and WINDOW % Q_BLOCK == 0

    def moe(xs, l, final_norm):
        return _hier_moe(xs.reshape(b * s, d), norm_ffn[l], router_group_w[l], router_group_b[l],
                         router_expert_w[l], router_expert_b[l], expert_w_gate, expert_w_up,
                         expert_w_down, l, norm_final, final_norm).reshape(b, s, d)

    x = _mixer(x, norm_mix[0], conv_w_in[0], conv_w[0], conv_w_out[0])
    x = moe(x, 0, False)
    x = _nsa_layer(x, positions, norm_kv, norm_mix[1], w_kv, cmp_pos_k, cmp_w1_k, cmp_w2_k,
                   cmp_pos_v, cmp_w1_v, cmp_w2_v, attn_w_in[0], attn_w_out[0])
    return moe(x, 1, True)
```

```python
import functools

import jax
import jax.numpy as jnp
from jax import lax
from jax.experimental import pallas as pl
from jax.experimental.pallas import tpu as pltpu

F32 = jnp.float32
BF16 = jnp.bfloat16
I32 = jnp.int32

RMS_EPS = 1e-6
ROPE_THETA = 10000.0
HEAD_DIM = 64
HALF = HEAD_DIM // 2
N_HEADS = 16
N_KV_GROUPS = 4
HEADS_PER_GROUP = N_HEADS // N_KV_GROUPS
N_BRANCH = 3
CMP_BLOCK = 32
CMP_STRIDE = 16
SLC_BLOCK = 64
N_SELECT = 16
WINDOW = 512
Q_BLOCK = 256
FORCE_BONUS = 1000.0
N_GROUPS = 4
EXPERTS_PER_GROUP = 8
N_EXPERTS = N_GROUPS * EXPERTS_PER_GROUP
CHUNK_ROWS = 256
GATHER_DEPTH = 3
NEG = -1e30
LANES = 128
KEY_TILE = 512
BIAS_ROWS = 16
MAX_ROWS = 64
CMP_CHUNK = 256
N_FORCED = 3
PICK_ROWS = 64
PICKED = -2.0
V_ROWS = 80
LOG2E = 1.4426950408889634
VMEM_LIMIT = 56 * 1024 * 1024


def _cparams(*sem):
    return pltpu.CompilerParams(dimension_semantics=sem, vmem_limit_bytes=VMEM_LIMIT)


def _rms_scale(x):
    return x * lax.rsqrt(jnp.mean(x * x, axis=-1, keepdims=True) + RMS_EPS)


def _dot(a, b):
    return jnp.dot(a, b, preferred_element_type=F32)


def _dot_nt(a, b):
    return lax.dot_general(a, b, (((1,), (1,)), ((), ())), preferred_element_type=F32)


def _dot_tn(a, b):
    return lax.dot_general(a, b, (((0,), (0,)), ((), ())), preferred_element_type=F32)


def _ones_row_block(shape):
    row = lax.broadcasted_iota(I32, shape, len(shape) - 2)
    return jnp.where(row == 0, 1.0, 0.0).astype(F32)


def _mixer_kernel(x_ref, g_ref, win_ref, cw_ref, wout_ref, o_ref, tail_ref):
    d = x_ref.shape[-1]
    tm = x_ref.shape[1]

    @pl.when(pl.program_id(1) == 0)
    def _():
        tail_ref[...] = jnp.zeros_like(tail_ref)

    x = x_ref[0]
    h = (_rms_scale(x) * g_ref[...]).astype(BF16)
    proj = _dot(h, win_ref[...])
    b_gate = proj[:, :d]
    u = proj[:, d:2 * d] * proj[:, 2 * d:]
    row = lax.broadcasted_iota(I32, (tm, 1), 0)
    prev1 = tail_ref[7:8, :]
    prev2 = tail_ref[6:7, :]
    u1 = jnp.where(row >= 1, pltpu.roll(u, 1, 0), prev1)
    u2 = jnp.where(row >= 2, pltpu.roll(u, 2, 0), jnp.where(row == 1, prev1, prev2))
    y = cw_ref[2:3, :] * u + cw_ref[1:2, :] * u1 + cw_ref[0:1, :] * u2
    tail_ref[...] = u[tm - 8:, :]
    z = (b_gate * y).astype(BF16)
    o_ref[0] = x + _dot(z, wout_ref[...])


def _mixer(x, g, w_in, conv_w, w_out, tm=512):
    b, s, d = x.shape
    return pl.pallas_call(
        _mixer_kernel,
        grid=(b, s // tm),
        in_specs=[
            pl.BlockSpec((1, tm, d), lambda i, j: (i, j, 0)),
            pl.BlockSpec((1, d), lambda i, j: (0, 0)),
            pl.BlockSpec((d, 3 * d), lambda i, j: (0, 0)),
            pl.BlockSpec((3, d), lambda i, j: (0, 0)),
            pl.BlockSpec((d, d), lambda i, j: (0, 0)),
        ],
        out_specs=pl.BlockSpec((1, tm, d), lambda i, j: (i, j, 0)),
        out_shape=jax.ShapeDtypeStruct((b, s, d), F32),
        scratch_shapes=[pltpu.VMEM((8, d), F32)],
        compiler_params=_cparams("arbitrary", "arbitrary"),
        name="conv_mixer",
    )(x, g.reshape(1, d), w_in.astype(BF16), conv_w, w_out.astype(BF16))


def _router_kernel(x_ref, g_ref, wt_ref, b_ref, hb_ref, eid_ref, wts_ref, counts_ref, cnt_ref):
    tm = x_ref.shape[0]
    h = _rms_scale(x_ref[...]) * g_ref[...]
    bits = lax.bitcast_convert_type(h.astype(BF16).astype(F32), jnp.uint32)
    half = h.shape[1] // 2
    hb_ref[...] = (bits[:, :half] >> 16) | (bits[:, half:] & jnp.uint32(0xFFFF0000))
    w = wt_ref[...]
    w_hi = w.astype(BF16)
    w_lo = (w - w_hi.astype(F32)).astype(BF16)
    h_hi = h.astype(BF16)
    h_lo = (h - h_hi.astype(F32)).astype(BF16)
    lt = _dot_nt(w_hi, h_hi) + _dot_nt(w_hi, h_lo) + _dot_nt(w_lo, h_hi) + b_ref[...]
    r8 = lax.broadcasted_iota(I32, (8, 1), 0)
    gl = jnp.where(r8 < N_GROUPS, lt[0:8], -jnp.inf)
    gmax = jnp.max(gl, axis=0, keepdims=True)
    gtop = jnp.min(jnp.where(gl == gmax, r8, 8), axis=0, keepdims=True)
    p_g = 1.0 / jnp.sum(jnp.exp(gl - gmax), axis=0, keepdims=True)
    chosen = lt[8:16]
    for k in range(1, N_GROUPS):
        chosen = jnp.where(gtop == k, lt[8 + 8 * k:16 + 8 * k], chosen)
    v0 = jnp.max(chosen, axis=0, keepdims=True)
    i0 = jnp.min(jnp.where(chosen == v0, r8, 8), axis=0, keepdims=True)
    rest = jnp.where(r8 == i0, -jnp.inf, chosen)
    v1 = jnp.max(rest, axis=0, keepdims=True)
    i1 = jnp.min(jnp.where(rest == v1, r8, 8), axis=0, keepdims=True)
    e = jnp.exp(v1 - v0)
    w0 = p_g / (1.0 + e)
    w1 = p_g * e / (1.0 + e)
    e0 = gtop * EXPERTS_PER_GROUP + i0
    e1 = gtop * EXPERTS_PER_GROUP + i1
    wts_ref[...] = jnp.where(r8 == 0, w0, jnp.where(r8 == 1, w1, 0.0))

    @pl.when(pl.program_id(0) == 0)
    def _():
        cnt_ref[...] = jnp.zeros_like(cnt_ref)

    rexp = lax.broadcasted_iota(I32, (N_EXPERTS, 1), 0)
    tri = (lax.broadcasted_iota(I32, (tm, tm), 0) <= lax.broadcasted_iota(I32, (tm, tm), 1)).astype(BF16)
    oh0 = rexp == e0
    oh1 = rexp == e1
    cum0 = _dot(oh0.astype(BF16), tri)
    cum1 = _dot(oh1.astype(BF16), tri)
    before0 = cnt_ref[...]
    before1 = before0 + cum0[:, tm - 1:tm]
    rank0 = jnp.sum(jnp.where(oh0, before0 + cum0, 0.0), axis=0, keepdims=True) - 1.0
    rank1 = jnp.sum(jnp.where(oh1, before1 + cum1, 0.0), axis=0, keepdims=True) - 1.0
    cnt_ref[...] = before1 + cum1[:, tm - 1:tm]
    counts_ref[...] = jnp.broadcast_to(cnt_ref[...], counts_ref.shape).astype(I32)
    eid_ref[...] = jnp.where(r8 == 0, e0, jnp.where(r8 == 1, e1, jnp.where(
        r8 == 2, rank0.astype(I32), jnp.where(r8 == 3, rank1.astype(I32), 0))))


def _router(xf, g, w_rg, b_rg, w_re, b_re, tm=512):
    n, d = xf.shape
    rows = 8 + N_EXPERTS
    wt = jnp.zeros((rows, d), F32).at[:N_GROUPS].set(w_rg.T).at[8:].set(w_re.T)
    bt = jnp.zeros((rows, 1), F32).at[:N_GROUPS, 0].set(b_rg).at[8:, 0].set(b_re)
    return pl.pallas_call(
        _router_kernel,
        grid=(n // tm,),
        in_specs=[
            pl.BlockSpec((tm, d), lambda i: (i, 0)),
            pl.BlockSpec((1, d), lambda i: (0, 0)),
            pl.BlockSpec((rows, d), lambda i: (0, 0)),
            pl.BlockSpec((rows, 1), lambda i: (0, 0)),
        ],
        out_specs=[
            pl.BlockSpec((tm, d // 2), lambda i: (i, 0)),
            pl.BlockSpec((8, tm), lambda i: (0, i)),
            pl.BlockSpec((8, tm), lambda i: (0, i)),
            pl.BlockSpec((N_EXPERTS, LANES), lambda i: (0, 0)),
        ],
        out_shape=[
            jax.ShapeDtypeStruct((n, d // 2), jnp.uint32),
            jax.ShapeDtypeStruct((8, n), I32),
            jax.ShapeDtypeStruct((8, n), F32),
            jax.ShapeDtypeStruct((N_EXPERTS, LANES), I32),
        ],
        scratch_shapes=[pltpu.VMEM((N_EXPERTS, 1), F32)],
        compiler_params=_cparams("arbitrary"),
        name="moe_router",
    )(xf, g.reshape(1, d), wt, bt)


def _take_rows(a, idx):
    return a.at[idx].get(mode="promise_in_bounds")


def _dispatch_plan(eid, counts, n_tok):
    a_exp = eid[0:2].reshape(-1)
    rank = eid[2:4].reshape(-1)
    n_asg = a_exp.shape[0]
    a_tok = jnp.arange(n_asg, dtype=I32) % n_tok
    padded = (counts + CHUNK_ROWS - 1) // CHUNK_ROWS * CHUNK_ROWS
    pend = jnp.cumsum(padded)
    pstart = pend - padded
    dest = (_take_rows(pstart, a_exp) + rank).astype(I32)
    n_chunks = -(-n_asg // CHUNK_ROWS) + N_EXPERTS
    row_tok = jnp.zeros((n_chunks * CHUNK_ROWS,), I32).at[dest].set(
        a_tok, unique_indices=True, mode="promise_in_bounds")
    chunk_row0 = jnp.arange(n_chunks, dtype=I32) * CHUNK_ROWS
    chunk_exp = jnp.minimum(jnp.sum((pend[None, :] <= chunk_row0[:, None]).astype(I32), axis=1), N_EXPERTS - 1)
    return row_tok, dest.reshape(2, n_tok), chunk_exp


def _expert_kernel(ce_ref, rt_ref, hb_ref, wg_ref, wu_ref, wd_ref, o_ref, wg_bf, wu_bf, wd_bf, xbuf, xc_ref, sem):
    c = pl.program_id(0)
    last = pl.num_programs(0) - 1
    depth = xbuf.shape[0]
    slot = c % depth
    nw = hb_ref.shape[1]

    def start_gather(chunk, to_slot):
        for r in range(CHUNK_ROWS):
            tok = rt_ref[chunk * CHUNK_ROWS + r]
            pltpu.make_async_copy(hb_ref.at[tok], xbuf.at[to_slot, pl.ds(r * nw, nw), :], sem.at[to_slot]).start()

    def wait_gather(of_slot):
        pltpu.make_async_copy(xbuf.at[of_slot], xbuf.at[of_slot], sem.at[of_slot]).wait()

    @pl.when(c == 0)
    def _():
        for k in range(depth - 1):
            start_gather(jnp.minimum(k, last), k)

    @pl.when((c == 0) | (ce_ref[c] != ce_ref[jnp.maximum(c - 1, 0)]))
    def _():
        wg_bf[...] = wg_ref[0, 0].astype(BF16)
        wu_bf[...] = wu_ref[0, 0].astype(BF16)
        wd_bf[...] = wd_ref[0, 0].astype(BF16)

    wait_gather(slot)
    words = [xbuf[slot, pl.ds(j, CHUNK_ROWS, stride=nw), :] for j in range(nw)]
    lo = [lax.bitcast_convert_type(w << 16, F32) for w in words]
    hi = [lax.bitcast_convert_type(w & jnp.uint32(0xFFFF0000), F32) for w in words]
    xc_ref[...] = jnp.concatenate(lo + hi, axis=1).astype(BF16)
    start_gather(jnp.minimum(c + depth - 1, last), (c + depth - 1) % depth)
    xc = xc_ref[...]
    gate = _dot(xc, wg_bf[...])
    up = _dot(xc, wu_bf[...])
    act = (gate * jax.nn.sigmoid(gate) * up).astype(BF16)
    o_ref[...] = _dot(act, wd_bf[...])

    @pl.when(c == last)
    def _():
        for k in range(1, depth):
            wait_gather((c + k) % depth)


def _experts(hb, row_tok, chunk_exp, w_gate, w_up, w_down, layer):
    d = w_gate.shape[-2]
    de = w_gate.shape[-1]
    n_chunks = chunk_exp.shape[0]
    weight = lambda shape: pl.BlockSpec((1, 1) + shape, lambda c, ce, rt: (layer, ce[c], 0, 0))
    grid_spec = pltpu.PrefetchScalarGridSpec(
        num_scalar_prefetch=2,
        grid=(n_chunks,),
        in_specs=[pl.BlockSpec(memory_space=pl.ANY), weight((d, de)), weight((d, de)), weight((de, d))],
        out_specs=pl.BlockSpec((CHUNK_ROWS, d), lambda c, ce, rt: (c, 0)),
        scratch_shapes=[pltpu.VMEM((d, de), BF16), pltpu.VMEM((d, de), BF16), pltpu.VMEM((de, d), BF16),
                        pltpu.VMEM((GATHER_DEPTH, CHUNK_ROWS * hb.shape[1], LANES), jnp.uint32),
                        pltpu.VMEM((CHUNK_ROWS, d), BF16),
                        pltpu.SemaphoreType.DMA((GATHER_DEPTH,))],
    )
    return pl.pallas_call(
        _expert_kernel,
        grid_spec=grid_spec,
        out_shape=jax.ShapeDtypeStruct((n_chunks * CHUNK_ROWS, d), F32),
        compiler_params=_cparams("arbitrary"),
        name="moe_experts",
    )(chunk_exp, row_tok, hb, w_gate, w_up, w_down)


def _combine_kernel(x_ref, y0_ref, y1_ref, w_ref, g_ref, o_ref, *, final_norm):
    w = w_ref[...]
    out = x_ref[...] + w[:, 0:1] * y0_ref[...] + w[:, 1:2] * y1_ref[...]
    if final_norm:
        out = _rms_scale(out) * g_ref[...]
    o_ref[...] = out


def _combine(xf, y0, y1, wcol, g, final_norm, tm=512):
    n, d = xf.shape
    row = pl.BlockSpec((tm, d), lambda i: (i, 0))
    return pl.pallas_call(
        functools.partial(_combine_kernel, final_norm=final_norm),
        grid=(n // tm,),
        in_specs=[row, row, row, pl.BlockSpec((tm, 8), lambda i: (i, 0)), pl.BlockSpec((1, d), lambda i: (0, 0))],
        out_specs=row,
        out_shape=jax.ShapeDtypeStruct((n, d), F32),
        compiler_params=_cparams("arbitrary"),
        name="moe_combine",
    )(xf, y0, y1, wcol, g.reshape(1, d))


def _hier_moe(xf, g_ffn, w_rg, b_rg, w_re, b_re, w_gate, w_up, w_down, layer, g_final, final_norm):
    n = xf.shape[0]
    hb, eid, wts, counts = _router(xf, g_ffn, w_rg, b_rg, w_re, b_re)
    row_tok, dest, chunk_exp = _dispatch_plan(eid, counts[:, 0], n)
    ys = _experts(hb.reshape(n, -1, LANES), row_tok, chunk_exp, w_gate, w_up, w_down, layer)
    return _combine(xf, _take_rows(ys, dest[0]), _take_rows(ys, dest[1]), wts.T, g_final, final_norm)


def _proj_kernel(x_ref, prow_ref, pcol_ref, gkv_ref, gq_ref, wnat_ref, wvt_ref, wqt_ref,
                 kcv_ref, ks_ref, kw_ref, vst_ref, vwt_ref, qc_ref, qr_ref, gt_ref):
    tm = x_ref.shape[1]
    gd = N_KV_GROUPS * HEAD_DIM
    qd = N_HEADS * HEAD_DIM
    y = _rms_scale(x_ref[0])
    hkv = (y * gkv_ref[...]).astype(BF16)
    hq = (y * gq_ref[...]).astype(BF16)

    nat = _dot(hkv, wnat_ref[...])
    kcv_ref[0] = nat[:, :2 * gd]
    lane = lax.broadcasted_iota(I32, (1, LANES), 1)
    inv = jnp.exp((lane % HALF).astype(F32) * (-jnp.log(ROPE_THETA) / HALF))
    ang = pcol_ref[0].astype(F32) * inv
    ctab = jnp.where(lane < HEAD_DIM, jnp.cos(ang), 0.0)
    stab = jnp.where(lane < HEAD_DIM, jnp.sin(ang), 0.0)
    key = pl.program_id(1) * tm + lax.broadcasted_iota(I32, (tm, 1), 0)
    blk = (key // SLC_BLOCK) % BIAS_ROWS
    onehot = jnp.where(lane == HEAD_DIM + blk, 1.0, 0.0)
    for g in range(N_KV_GROUPS):
        for slot, ref in ((0, ks_ref), (1, kw_ref)):
            lo = 2 * gd + (slot * N_KV_GROUPS + g) * LANES
            t = nat[:, lo:lo + LANES]
            roped = t * ctab + pltpu.roll(t, HEAD_DIM, 1) * stab
            if slot == 0:
                roped = roped + onehot
            ref[0, g] = roped.astype(BF16)

    vt = _dot_nt(wvt_ref[...], hkv)
    ones_rows = _ones_row_block((N_KV_GROUPS, V_ROWS - HEAD_DIM, tm))
    vs3 = jnp.concatenate([vt[:gd].reshape(N_KV_GROUPS, HEAD_DIM, tm), ones_rows], axis=1).astype(BF16)
    vw3 = jnp.concatenate([vt[gd:].reshape(N_KV_GROUPS, HEAD_DIM, tm), ones_rows], axis=1).astype(BF16)
    vst_ref[0, :, 0] = vs3
    for w in range(tm // Q_BLOCK):
        vwt_ref[0, :, w] = vw3[:, :, w * Q_BLOCK:(w + 1) * Q_BLOCK]

    qt = _dot_nt(wqt_ref[...], hq)
    scale = HEAD_DIM ** -0.5 * LOG2E
    q3 = qt[:qd].reshape(N_HEADS, HEAD_DIM, tm)
    qc_ref[0] = (qt[:qd] * scale).astype(BF16)
    frq = lax.broadcasted_iota(I32, (HALF, 1), 0).astype(F32)
    ang_t = jnp.exp(frq * (-jnp.log(ROPE_THETA) / HALF)) * prow_ref[0].astype(F32)
    cos_t = jnp.cos(ang_t)[None]
    sin_t = jnp.sin(ang_t)[None]
    t1 = q3[:, :HALF]
    t2 = q3[:, HALF:]
    qr = jnp.concatenate([t1 * cos_t - t2 * sin_t, t2 * cos_t + t1 * sin_t], axis=1)
    qr_ref[0] = (qr * scale).reshape(qd, tm).astype(BF16)
    gt_ref[0] = jax.nn.sigmoid(qt[qd:])


def _rot_half_cols(w):
    return jnp.concatenate([-w[:, HALF:], w[:, :HALF]], axis=1)


def _projections(x, positions, g_kv, g_q, w_kv, w_q, tm=KEY_TILE):
    b, s, d = x.shape
    gd = N_KV_GROUPS * HEAD_DIM
    qd = N_HEADS * HEAD_DIM
    slot = lambda i: w_kv[:, i * gd:(i + 1) * gd]
    aug = []
    for sl in (2, 4):
        for g in range(N_KV_GROUPS):
            wk = slot(sl)[:, g * HEAD_DIM:(g + 1) * HEAD_DIM]
            aug += [wk, _rot_half_cols(wk)]
    w_nat = jnp.concatenate([slot(0), slot(1)] + aug, axis=1).astype(BF16)
    w_vt = jnp.concatenate([slot(3), slot(5)], axis=1).T.astype(BF16)
    gate_cols = w_q[:, qd:].reshape(d, N_KV_GROUPS, HEADS_PER_GROUP * N_BRANCH)
    gate_cols = jnp.pad(gate_cols, ((0, 0), (0, 0), (0, 16 - HEADS_PER_GROUP * N_BRANCH))).reshape(d, 64)
    w_qt = jnp.concatenate([w_q[:, :qd], gate_cols], axis=1).T.astype(BF16)
    nt = s // tm
    n_nat = w_nat.shape[1]
    const = lambda shape: pl.BlockSpec(shape, lambda i, j: (0,) * len(shape))
    return pl.pallas_call(
        _proj_kernel,
        grid=(b, nt),
        in_specs=[
            pl.BlockSpec((1, tm, d), lambda i, j: (i, j, 0)),
            pl.BlockSpec((1, 1, tm), lambda i, j: (i, 0, j)),
            pl.BlockSpec((1, tm, 1), lambda i, j: (i, j, 0)),
            const((1, d)), const((1, d)), const((d, n_nat)), const((2 * gd, d)), const((qd + 64, d)),
        ],
        out_specs=[
            pl.BlockSpec((1, tm, 2 * gd), lambda i, j: (i, j, 0)),
            pl.BlockSpec((1, N_KV_GROUPS, tm, LANES), lambda i, j: (i, 0, j, 0)),
            pl.BlockSpec((1, N_KV_GROUPS, tm, LANES), lambda i, j: (i, 0, j, 0)),
            pl.BlockSpec((1, N_KV_GROUPS, 1, V_ROWS, tm), lambda i, j: (i, 0, j, 0, 0)),
            pl.BlockSpec((1, N_KV_GROUPS, tm // Q_BLOCK, V_ROWS, Q_BLOCK), lambda i, j: (i, 0, j, 0, 0)),
            pl.BlockSpec((1, qd, tm), lambda i, j: (i, 0, j)),
            pl.BlockSpec((1, qd, tm), lambda i, j: (i, 0, j)),
            pl.BlockSpec((1, 64, tm), lambda i, j: (i, 0, j)),
        ],
        out_shape=[
            jax.ShapeDtypeStruct((b, s, 2 * gd), F32),
            jax.ShapeDtypeStruct((b, N_KV_GROUPS, s, LANES), BF16),
            jax.ShapeDtypeStruct((b, N_KV_GROUPS, s, LANES), BF16),
            jax.ShapeDtypeStruct((b, N_KV_GROUPS, nt, V_ROWS, tm), BF16),
            jax.ShapeDtypeStruct((b, N_KV_GROUPS, s // Q_BLOCK, V_ROWS, Q_BLOCK), BF16),
            jax.ShapeDtypeStruct((b, qd, s), BF16),
            jax.ShapeDtypeStruct((b, qd, s), BF16),
            jax.ShapeDtypeStruct((b, 64, s), F32),
        ],
        compiler_params=_cparams("arbitrary", "arbitrary"),
        name="nsa_projections",
    )(x, positions.reshape(b, 1, s), positions.reshape(b, s, 1), g_kv.reshape(1, d), g_q.reshape(1, d),
      w_nat, w_vt, w_qt)


def _compress_kernel(a_ref, pos_ref, w1_ref, w2_ref, o_ref, *, transposed_out):
    half = w1_ref.shape[0] // 2
    a = a_ref[0, 0]
    top = (a + pos_ref[0:1, :]).astype(BF16)
    bot = (a + pos_ref[1:2, :]).astype(BF16)
    p = _dot(top, w1_ref[:half, :])
    q = _dot(bot, w1_ref[half:, :])
    n = a.shape[0]
    hidden = p + pltpu.roll(q, n - 1, 0)
    act = jax.nn.gelu(hidden, approximate=True).astype(BF16)
    if transposed_out:
        v_t = _dot_nt(w2_ref[...], act)
        o_ref[0, 0] = jnp.concatenate([v_t, _ones_row_block((V_ROWS - HEAD_DIM, n))], axis=0).astype(BF16)
    else:
        o_ref[0, 0] = _dot(act, w2_ref[...]).astype(BF16)


def _compress(a, pos, w1, w2, transposed_out):
    b, g, n, width = a.shape
    hidden = w1.shape[1]
    pos2 = pos.reshape(2, width)
    if transposed_out:
        w2p = w2.T.astype(BF16)
        out_block, out_shape = (1, 1, V_ROWS, n), (b, g, V_ROWS, n)
    else:
        w2p = jnp.pad(w2, ((0, 0), (0, LANES - HEAD_DIM))).astype(BF16)
        out_block, out_shape = (1, 1, n, LANES), (b, g, n, LANES)
    return pl.pallas_call(
        functools.partial(_compress_kernel, transposed_out=transposed_out),
        grid=(b, g),
        in_specs=[
            pl.BlockSpec((1, 1, n, width), lambda i, j: (i, j, 0, 0)),
            pl.BlockSpec((2, width), lambda i, j: (0, 0)),
            pl.BlockSpec((2 * width, hidden), lambda i, j: (0, 0)),
            pl.BlockSpec(w2p.shape, lambda i, j: (0, 0)),
        ],
        out_specs=pl.BlockSpec(out_block, lambda i, j: (i, j, 0, 0)),
        out_shape=jax.ShapeDtypeStruct(out_shape, BF16),
        compiler_params=_cparams("arbitrary", "arbitrary"),
        name="nsa_compress_v" if transposed_out else "nsa_compress_k",
    )(a, pos2, w1.astype(BF16), w2p)


def _heads_on_lanes(ref):
    return jnp.concatenate([ref[0, h * HEAD_DIM:(h + 1) * HEAD_DIM, :] for h in range(HEADS_PER_GROUP)], axis=1)


def _nsa_kernel(qc_ref, qr_ref, g_ref, kc_ref, vct_ref, band_ref, ks_ref, vst_ref, kw_ref, vwt_ref,
                o_ref, bias_ref, score_ref, sa_ref, sb_ref, pa_ref, pb_ref):
    i = pl.program_id(2)
    qb = Q_BLOCK
    width = HEADS_PER_GROUP * qb
    n_cmp = kc_ref.shape[2]
    n_slc = score_ref.shape[0]
    t_q = i * qb + lax.broadcasted_iota(I32, (1, qb), 1)
    t_row = jnp.concatenate([t_q] * HEADS_PER_GROUP, axis=1)
    zeros_pad = jnp.zeros((LANES - HEAD_DIM, width), BF16)

    cc = band_ref.shape[1]
    bpc = cc * CMP_STRIDE // SLC_BLOCK
    qc_t = jnp.concatenate([_heads_on_lanes(qc_ref), zeros_pad], axis=0)
    chunk_scores, maxima = [], []
    for c in range(n_cmp // cc):
        s = _dot(kc_ref[0, 0, c * cc:(c + 1) * cc, :], qc_t)
        cmp_end = (c * cc + lax.broadcasted_iota(I32, (cc, 1), 0)) * CMP_STRIDE + (CMP_BLOCK - 1)
        s = jnp.where(cmp_end <= t_row, s, NEG)
        chunk_scores.append(s)
        maxima.append(jnp.maximum(jnp.max(s, axis=0, keepdims=True), NEG * 1e-10))
    m_fin = functools.reduce(jnp.maximum, maxima)
    shares = []
    for c, (s, mx) in enumerate(zip(chunk_scores, maxima)):
        lhs = jnp.concatenate([vct_ref[0, 0, :, c * cc:(c + 1) * cc], band_ref[...]], axis=0)
        share = _dot(lhs, jnp.exp2(s - mx).astype(BF16))
        shares.append(share * jnp.exp2(mx - m_fin))
    acc_c = shares[0][:V_ROWS]
    for sh in shares[1:]:
        acc_c = acc_c + sh[:V_ROWS]
    rinv = 1.0 / jnp.maximum(acc_c[HEAD_DIM:HEAD_DIM + 1], 1e-30)
    o_cmp = acc_c[:HEAD_DIM] * rinv
    imp_rows = []
    for c, sh in enumerate(shares):
        u = sh[V_ROWS:V_ROWS + bpc]
        if c > 0:
            spill = shares[c - 1][V_ROWS + bpc:V_ROWS + bpc + 8]
            u = jnp.concatenate([u[:8] + spill, u[8:]], axis=0)
        u = u * rinv
        imp_c = u[:, 0:qb]
        for h in range(1, HEADS_PER_GROUP):
            imp_c = imp_c + u[:, h * qb:(h + 1) * qb]
        imp_rows.append(imp_c)
    imp = jnp.concatenate(imp_rows, axis=0)

    j_idx = lax.broadcasted_iota(I32, (n_slc, 1), 0)
    cur = t_q // SLC_BLOCK
    valid = j_idx * SLC_BLOCK <= t_q
    forced = (j_idx == 0) | (j_idx == cur) | (j_idx == cur - 1)
    score_ref[...] = jnp.where(valid & jnp.logical_not(forced), imp, -1.0)

    def pick(_, carry, rows):
        sc = score_ref[0:rows, :]
        best = jnp.max(sc, axis=0, keepdims=True)
        idx = jnp.min(jnp.where(sc == best, j_idx[0:rows], n_slc), axis=0, keepdims=True)
        score_ref[0:rows, :] = jnp.where(j_idx[0:rows] == idx, PICKED, sc)
        return carry

    pick_rows = min(PICK_ROWS, n_slc)
    need = (i * qb + qb - 1) // (SLC_BLOCK * pick_rows)
    for r in range(n_slc // pick_rows):
        @pl.when(need == r)
        def _(rows=(r + 1) * pick_rows):
            lax.fori_loop(0, N_SELECT - N_FORCED, functools.partial(pick, rows=rows), 0)
    bias = jnp.where(forced | (score_ref[...] == PICKED), 0.0, NEG).astype(BF16)
    bias_ref[...] = jnp.concatenate([bias] * HEADS_PER_GROUP, axis=1)

    qr_heads = _heads_on_lanes(qr_ref)
    zeros_sel = jnp.zeros((LANES - HEAD_DIM - BIAS_ROWS, width), BF16)
    tk = KEY_TILE
    blocks_per_tile = tk // SLC_BLOCK

    def scores(kt, s_ref, masked):
        b0 = pl.multiple_of((kt * blocks_per_tile) // BIAS_ROWS * BIAS_ROWS, BIAS_ROWS)
        q_aug = jnp.concatenate([qr_heads, bias_ref[pl.ds(b0, BIAS_ROWS), :], zeros_sel], axis=0)
        k0 = pl.multiple_of(kt * tk, tk)
        s = _dot(ks_ref[0, 0, pl.ds(k0, tk), :], q_aug)
        if masked:
            s = jnp.where(k0 + lax.broadcasted_iota(I32, (tk, 1), 0) <= t_row, s, NEG)
        s_ref[...] = s
        part = jnp.max(s.reshape(tk // MAX_ROWS, MAX_ROWS, width), axis=0)
        return jnp.max(part, axis=0, keepdims=True)

    def values(kt, p_ref):
        return _dot(vst_ref[0, 0, jnp.maximum(kt, 0)], p_ref[...])

    def trip(kk, carry, next_scores):
        m, acc, beta, cmax_a, cmax_b = carry
        kt = 2 * kk
        live = jnp.where(kk > 0, 1.0, 0.0)
        acc = acc + live * (beta * values(kt - 2, pa_ref) + values(kt - 1, pb_ref))
        m1 = jnp.maximum(m, cmax_a)
        pa_ref[...] = jnp.exp2((sa_ref[...] - m1).astype(BF16))
        m2 = jnp.maximum(m1, cmax_b)
        pb_ref[...] = jnp.exp2((sb_ref[...] - m2).astype(BF16))
        beta = jnp.exp2(m1 - m2)
        if next_scores is not None:
            cmax_a = scores(kt + 2, sa_ref, next_scores)
            cmax_b = scores(kt + 3, sb_ref, next_scores)
        return m2, (jnp.exp2(m - m1) * beta) * acc, beta, cmax_a, cmax_b

    @pl.when((pl.program_id(0) == 0) & (pl.program_id(1) == 0) & (i == 0))
    def _():
        pa_ref[...] = jnp.zeros_like(pa_ref)
        pb_ref[...] = jnp.zeros_like(pb_ref)

    n_full = (i * qb) // tk
    n_unmasked = jnp.maximum(n_full // 2 - 1, 0)
    carry = (jnp.full((1, width), NEG, F32), jnp.zeros((V_ROWS, width), F32), jnp.ones((1, width), F32),
             scores(0, sa_ref, True), scores(1, sb_ref, True))
    carry = lax.fori_loop(0, n_unmasked, functools.partial(trip, next_scores=False), carry)
    carry = lax.fori_loop(n_unmasked, n_full // 2, functools.partial(trip, next_scores=True), carry)
    qr_t = jnp.concatenate([qr_heads, zeros_pad], axis=0)
    n_win = WINDOW // qb + 1
    s_w = []
    for r in range(n_win):
        j = i - (n_win - 1) + r
        k0 = pl.multiple_of(jnp.maximum(j, 0) * qb, qb)
        s = _dot(kw_ref[0, 0, pl.ds(k0, qb), :], qr_t)
        if r in (0, n_win - 1):
            kpos = j * qb + lax.broadcasted_iota(I32, (qb, 1), 0)
            diff = t_row - kpos
            s = jnp.where((diff >= 0) & (diff < WINDOW) & (kpos >= 0), s, NEG)
        else:
            s = s + jnp.where(j >= 0, 0.0, NEG)
        s_w.append(s)
    m_w = s_w[0].max(axis=0, keepdims=True)
    for r in range(1, n_win):
        m_w = jnp.maximum(m_w, s_w[r].max(axis=0, keepdims=True))

    last = n_full // 2
    _, acc_s, beta, _, _ = trip(last, carry, None)
    acc_s = acc_s + beta * values(2 * last, pa_ref) + values(2 * last + 1, pb_ref)
    o_slc = acc_s[:HEAD_DIM] / acc_s[HEAD_DIM:HEAD_DIM + 1]

    acc_w = jnp.zeros((V_ROWS, width), F32)
    for r in range(n_win):
        jc = jnp.maximum(i - (n_win - 1) + r, 0)
        acc_w = acc_w + _dot(vwt_ref[0, 0, jc], jnp.exp2((s_w[r] - m_w).astype(BF16)))
    o_win = acc_w[:HEAD_DIM] / acc_w[HEAD_DIM:HEAD_DIM + 1]

    gts = g_ref[0]
    for h in range(HEADS_PER_GROUP):
        sl = slice(h * qb, (h + 1) * qb)
        r0 = h * N_BRANCH
        o = (gts[r0:r0 + 1] * o_cmp[:, sl] + gts[r0 + 1:r0 + 2] * o_slc[:, sl] + gts[r0 + 2:r0 + 3] * o_win[:, sl])
        o_ref[0, h * HEAD_DIM:(h + 1) * HEAD_DIM, :] = o.astype(BF16)


def _nsa_attention(qc_t, qr_t, gates_t, k_cmp, v_cmp_t, band, k_slc, v_slc_t, k_win, v_win_t):
    b, qd, s = qc_t.shape
    g = N_KV_GROUPS
    n_cmp = k_cmp.shape[2]
    n_slc = s // SLC_BLOCK
    nt, tk = v_slc_t.shape[2], v_slc_t.shape[4]
    rows = HEADS_PER_GROUP * HEAD_DIM
    width = HEADS_PER_GROUP * Q_BLOCK
    qspec = pl.BlockSpec((1, rows, Q_BLOCK), lambda bi, gi, i: (bi, gi, i))
    per_bg = lambda shape: pl.BlockSpec((1, 1) + shape, lambda bi, gi, i: (bi, gi) + (0,) * len(shape))
    return pl.pallas_call(
        _nsa_kernel,
        grid=(b, g, s // Q_BLOCK),
        in_specs=[
            qspec, qspec,
            pl.BlockSpec((1, 16, Q_BLOCK), lambda bi, gi, i: (bi, gi, i)),
            per_bg((n_cmp, LANES)), per_bg((V_ROWS, n_cmp)),
            pl.BlockSpec(band.shape, lambda bi, gi, i: (0, 0)),
            per_bg((s, LANES)), per_bg((nt, V_ROWS, tk)),
            per_bg((s, LANES)), per_bg((s // Q_BLOCK, V_ROWS, Q_BLOCK)),
        ],
        out_specs=qspec,
        out_shape=jax.ShapeDtypeStruct((b, qd, s), BF16),
        scratch_shapes=[
            pltpu.VMEM((n_slc, width), BF16),
            pltpu.VMEM((n_slc, Q_BLOCK), F32),
            pltpu.VMEM((tk, width), F32), pltpu.VMEM((tk, width), F32),
            pltpu.VMEM((tk, width), BF16), pltpu.VMEM((tk, width), BF16),
        ],
        compiler_params=_cparams("arbitrary", "arbitrary", "arbitrary"),
        name="nsa_attention",
    )(qc_t, qr_t, gates_t, k_cmp, v_cmp_t, band, k_slc, v_slc_t, k_win, v_win_t)


def _selection_band(n_cmp):
    cc = min(CMP_CHUNK, n_cmp)
    bpc = cc * CMP_STRIDE // SLC_BLOCK
    c0 = jnp.arange(cc) * CMP_STRIDE
    j0 = jnp.arange(bpc + 16) * SLC_BLOCK
    ov = jnp.minimum(c0[None, :] + CMP_BLOCK, j0[:, None] + SLC_BLOCK) - jnp.maximum(c0[None, :], j0[:, None])
    return (jnp.clip(ov, 0).astype(F32) / CMP_BLOCK).astype(BF16)


def _out_proj_kernel(x_ref, ot_ref, w_ref, o_ref):
    o_ref[0] = x_ref[0] + _dot_tn(ot_ref[0], w_ref[...])


def _out_proj(x, o_t, w_out, tm=512):
    b, s, d = x.shape
    qd = o_t.shape[1]
    return pl.pallas_call(
        _out_proj_kernel,
        grid=(b, s // tm),
        in_specs=[
            pl.BlockSpec((1, tm, d), lambda i, j: (i, j, 0)),
            pl.BlockSpec((1, qd, tm), lambda i, j: (i, 0, j)),
            pl.BlockSpec((qd, d), lambda i, j: (0, 0)),
        ],
        out_specs=pl.BlockSpec((1, tm, d), lambda i, j: (i, j, 0)),
        out_shape=jax.ShapeDtypeStruct((b, s, d), F32),
        compiler_params=_cparams("arbitrary", "arbitrary"),
        name="nsa_out_proj",
    )(x, o_t, w_out.astype(BF16))


def _nsa_layer(x, positions, g_kv, g_q, w_kv, cmp_pos_k, cmp_w1_k, cmp_w2_k, cmp_pos_v, cmp_w1_v, cmp_w2_v,
               w_in, w_out):
    b, s, d = x.shape
    g = N_KV_GROUPS
    kcv, k_slc, k_win, v_slc_t, v_win_t, qc_t, qr_t, gates_t = _projections(x, positions, g_kv, g_q, w_kv, w_in)
    n_half = s // CMP_STRIDE
    halves = kcv.reshape(b, n_half, CMP_STRIDE, 2, g, HEAD_DIM).transpose(0, 3, 4, 1, 2, 5)
    halves = halves.reshape(b, 2, g, n_half, CMP_STRIDE * HEAD_DIM)
    k_cmp = _compress(halves[:, 0], cmp_pos_k, cmp_w1_k, cmp_w2_k, transposed_out=False)
    v_cmp_t = _compress(halves[:, 1], cmp_pos_v, cmp_w1_v, cmp_w2_v, transposed_out=True)
    o_t = _nsa_attention(qc_t, qr_t, gates_t, k_cmp, v_cmp_t, _selection_band(n_half), k_slc, v_slc_t, k_win,
                         v_win_t)
    return _out_proj(x, o_t, w_out)


def kernel(x, positions, norm_mix, norm_ffn, conv_w_in, conv_w, conv_w_out, norm_kv, w_kv, cmp_pos_k, cmp_w1_k, cmp_w2_k, cmp_pos_v, cmp_w1_v, cmp_w2_v, attn_w_in, attn_w_out, router_group_w, router_group_b, router_expert_w, router_expert_b, expert_w_gate, expert_w_up, expert_w_down, norm_final):
    b, s, d = x.shape
    assert s % (2 * KEY_TILE) == 0 and (s // SLC_BLOCK) % BIAS_ROWS == 0 and WINDOW % Q_BLOCK == 0

    def moe(xs, l, final_norm):
        return _hier_moe(xs.reshape(b * s, d), norm_ffn[l], router_group_w[l], router_group_b[l],
                         router_expert_w[l], router_expert_b[l], expert_w_gate, expert_w_up,
                         expert_w_down, l, norm_final, final_norm).reshape(b, s, d)

    x = _mixer(x, norm_mix[0], conv_w_in[0], conv_w[0], conv_w_out[0])
    x = moe(x, 0, False)
    x = _nsa_layer(x, positions, norm_kv, norm_mix[1], w_kv, cmp_pos_k, cmp_w1_k, cmp_w2_k,
                   cmp_pos_v, cmp_w1_v, cmp_w2_v, attn_w_in[0], attn_w_out[0])
    return moe(x, 1, True)
```

```python
import functools

import jax
import jax.numpy as jnp
from jax import lax
from jax.experimental import pallas as pl
from jax.experimental.pallas import tpu as pltpu

F32 = jnp.float32
BF16 = jnp.bfloat16
I32 = jnp.int32

RMS_EPS = 1e-6
ROPE_THETA = 10000.0
HEAD_DIM = 64
HALF = HEAD_DIM // 2
N_HEADS = 16
N_KV_GROUPS = 4
HEADS_PER_GROUP = N_HEADS // N_KV_GROUPS
N_BRANCH = 3
CMP_BLOCK = 32
CMP_STRIDE = 16
SLC_BLOCK = 64
N_SELECT = 16
WINDOW = 512
Q_BLOCK = 256
FORCE_BONUS = 1000.0
N_GROUPS = 4
EXPERTS_PER_GROUP = 8
N_EXPERTS = N_GROUPS * EXPERTS_PER_GROUP
CHUNK_ROWS = 256
GATHER_DEPTH = 3
NEG = -1e30
LANES = 128
KEY_TILE = 512
BIAS_ROWS = 16
MAX_ROWS = 64
CMP_CHUNK = 256
N_FORCED = 3
PICK_ROWS = 64
PICKED = -2.0
V_ROWS = 80
LOG2E = 1.4426950408889634
VMEM_LIMIT = 56 * 1024 * 1024


def _cparams(*sem):
    return pltpu.CompilerParams(dimension_semantics=sem, vmem_limit_bytes=VMEM_LIMIT)


def _rms_scale(x):
    return x * lax.rsqrt(jnp.mean(x * x, axis=-1, keepdims=True) + RMS_EPS)


def _dot(a, b):
    return jnp.dot(a, b, preferred_element_type=F32)


def _dot_nt(a, b):
    return lax.dot_general(a, b, (((1,), (1,)), ((), ())), preferred_element_type=F32)


def _dot_tn(a, b):
    return lax.dot_general(a, b, (((0,), (0,)), ((), ())), preferred_element_type=F32)


def _ones_row_block(shape):
    row = lax.broadcasted_iota(I32, shape, len(shape) - 2)
    return jnp.where(row == 0, 1.0, 0.0).astype(F32)


def _mixer_kernel(x_ref, g_ref, win_ref, cw_ref, wout_ref, o_ref, tail_ref):
    d = x_ref.shape[-1]
    tm = x_ref.shape[1]

    @pl.when(pl.program_id(1) == 0)
    def _():
        tail_ref[...] = jnp.zeros_like(tail_ref)

    x = x_ref[0]
    h = (_rms_scale(x) * g_ref[...]).astype(BF16)
    proj = _dot(h, win_ref[...])
    b_gate = proj[:, :d]
    u = proj[:, d:2 * d] * proj[:, 2 * d:]
    row = lax.broadcasted_iota(I32, (tm, 1), 0)
    prev1 = tail_ref[7:8, :]
    prev2 = tail_ref[6:7, :]
    u1 = jnp.where(row >= 1, pltpu.roll(u, 1, 0), prev1)
    u2 = jnp.where(row >= 2, pltpu.roll(u, 2, 0), jnp.where(row == 1, prev1, prev2))
    y = cw_ref[2:3, :] * u + cw_ref[1:2, :] * u1 + cw_ref[0:1, :] * u2
    tail_ref[...] = u[tm - 8:, :]
    z = (b_gate * y).astype(BF16)
    o_ref[0] = x + _dot(z, wout_ref[...])


def _mixer(x, g, w_in, conv_w, w_out, tm=512):
    b, s, d = x.shape
    return pl.pallas_call(
        _mixer_kernel,
        grid=(b, s // tm),
        in_specs=[
            pl.BlockSpec((1, tm, d), lambda i, j: (i, j, 0)),
            pl.BlockSpec((1, d), lambda i, j: (0, 0)),
            pl.BlockSpec((d, 3 * d), lambda i, j: (0, 0)),
            pl.BlockSpec((3, d), lambda i, j: (0, 0)),
            pl.BlockSpec((d, d), lambda i, j: (0, 0)),
        ],
        out_specs=pl.BlockSpec((1, tm, d), lambda i, j: (i, j, 0)),
        out_shape=jax.ShapeDtypeStruct((b, s, d), F32),
        scratch_shapes=[pltpu.VMEM((8, d), F32)],
        compiler_params=_cparams("arbitrary", "arbitrary"),
        name="conv_mixer",
    )(x, g.reshape(1, d), w_in.astype(BF16), conv_w, w_out.astype(BF16))


def _router_kernel(x_ref, g_ref, wt_ref, b_ref, hb_ref, eid_ref, wts_ref, counts_ref, cnt_ref):
    tm = x_ref.shape[0]
    h = _rms_scale(x_ref[...]) * g_ref[...]
    bits = lax.bitcast_convert_type(h.astype(BF16).astype(F32), jnp.uint32)
    half = h.shape[1] // 2
    hb_ref[...] = (bits[:, :half] >> 16) | (bits[:, half:] & jnp.uint32(0xFFFF0000))
    w = wt_ref[...]
    w_hi = w.astype(BF16)
    w_lo = (w - w_hi.astype(F32)).astype(BF16)
    h_hi = h.astype(BF16)
    h_lo = (h - h_hi.astype(F32)).astype(BF16)
    lt = _dot_nt(w_hi, h_hi) + _dot_nt(w_hi, h_lo) + _dot_nt(w_lo, h_hi) + b_ref[...]
    r8 = lax.broadcasted_iota(I32, (8, 1), 0)
    gl = jnp.where(r8 < N_GROUPS, lt[0:8], -jnp.inf)
    gmax = jnp.max(gl, axis=0, keepdims=True)
    gtop = jnp.min(jnp.where(gl == gmax, r8, 8), axis=0, keepdims=True)
    p_g = 1.0 / jnp.sum(jnp.exp(gl - gmax), axis=0, keepdims=True)
    chosen = lt[8:16]
    for k in range(1, N_GROUPS):
        chosen = jnp.where(gtop == k, lt[8 + 8 * k:16 + 8 * k], chosen)
    v0 = jnp.max(chosen, axis=0, keepdims=True)
    i0 = jnp.min(jnp.where(chosen == v0, r8, 8), axis=0, keepdims=True)
    rest = jnp.where(r8 == i0, -jnp.inf, chosen)
    v1 = jnp.max(rest, axis=0, keepdims=True)
    i1 = jnp.min(jnp.where(rest == v1, r8, 8), axis=0, keepdims=True)
    e = jnp.exp(v1 - v0)
    w0 = p_g / (1.0 + e)
    w1 = p_g * e / (1.0 + e)
    e0 = gtop * EXPERTS_PER_GROUP + i0
    e1 = gtop * EXPERTS_PER_GROUP + i1
    wts_ref[...] = jnp.where(r8 == 0, w0, jnp.where(r8 == 1, w1, 0.0))

    @pl.when(pl.program_id(0) == 0)
    def _():
        cnt_ref[...] = jnp.zeros_like(cnt_ref)

    rexp = lax.broadcasted_iota(I32, (N_EXPERTS, 1), 0)
    tri = (lax.broadcasted_iota(I32, (tm, tm), 0) <= lax.broadcasted_iota(I32, (tm, tm), 1)).astype(BF16)
    oh0 = rexp == e0
    oh1 = rexp == e1
    cum0 = _dot(oh0.astype(BF16), tri)
    cum1 = _dot(oh1.astype(BF16), tri)
    before0 = cnt_ref[...]
    before1 = before0 + cum0[:, tm - 1:tm]
    rank0 = jnp.sum(jnp.where(oh0, before0 + cum0, 0.0), axis=0, keepdims=True) - 1.0
    rank1 = jnp.sum(jnp.where(oh1, before1 + cum1, 0.0), axis=0, keepdims=True) - 1.0
    cnt_ref[...] = before1 + cum1[:, tm - 1:tm]
    counts_ref[...] = jnp.broadcast_to(cnt_ref[...], counts_ref.shape).astype(I32)
    eid_ref[...] = jnp.where(r8 == 0, e0, jnp.where(r8 == 1, e1, jnp.where(
        r8 == 2, rank0.astype(I32), jnp.where(r8 == 3, rank1.astype(I32), 0))))


def _router(xf, g, w_rg, b_rg, w_re, b_re, tm=512):
    n, d = xf.shape
    rows = 8 + N_EXPERTS
    wt = jnp.zeros((rows, d), F32).at[:N_GROUPS].set(w_rg.T).at[8:].set(w_re.T)
    bt = jnp.zeros((rows, 1), F32).at[:N_GROUPS, 0].set(b_rg).at[8:, 0].set(b_re)
    return pl.pallas_call(
        _router_kernel,
        grid=(n // tm,),
        in_specs=[
            pl.BlockSpec((tm, d), lambda i: (i, 0)),
            pl.BlockSpec((1, d), lambda i: (0, 0)),
            pl.BlockSpec((rows, d), lambda i: (0, 0)),
            pl.BlockSpec((rows, 1), lambda i: (0, 0)),
        ],
        out_specs=[
            pl.BlockSpec((tm, d // 2), lambda i: (i, 0)),
            pl.BlockSpec((8, tm), lambda i: (0, i)),
            pl.BlockSpec((8, tm), lambda i: (0, i)),
            pl.BlockSpec((N_EXPERTS, LANES), lambda i: (0, 0)),
        ],
        out_shape=[
            jax.ShapeDtypeStruct((n, d // 2), jnp.uint32),
            jax.ShapeDtypeStruct((8, n), I32),
            jax.ShapeDtypeStruct((8, n), F32),
            jax.ShapeDtypeStruct((N_EXPERTS, LANES), I32),
        ],
        scratch_shapes=[pltpu.VMEM((N_EXPERTS, 1), F32)],
        compiler_params=_cparams("arbitrary"),
        name="moe_router",
    )(xf, g.reshape(1, d), wt, bt)


def _take_rows(a, idx):
    return a.at[idx].get(mode="promise_in_bounds")


def _dispatch_plan(eid, counts, n_tok):
    a_exp = eid[0:2].reshape(-1)
    rank = eid[2:4].reshape(-1)
    n_asg = a_exp.shape[0]
    a_tok = jnp.arange(n_asg, dtype=I32) % n_tok
    padded = (counts + CHUNK_ROWS - 1) // CHUNK_ROWS * CHUNK_ROWS
    pend = jnp.cumsum(padded)
    pstart = pend - padded
    dest = (_take_rows(pstart, a_exp) + rank).astype(I32)
    n_chunks = -(-n_asg // CHUNK_ROWS) + N_EXPERTS
    row_tok = jnp.zeros((n_chunks * CHUNK_ROWS,), I32).at[dest].set(
        a_tok, unique_indices=True, mode="promise_in_bounds")
    chunk_row0 = jnp.arange(n_chunks, dtype=I32) * CHUNK_ROWS
    chunk_exp = jnp.minimum(jnp.sum((pend[None, :] <= chunk_row0[:, None]).astype(I32), axis=1), N_EXPERTS - 1)
    return row_tok, dest.reshape(2, n_tok), chunk_exp


def _expert_kernel(ce_ref, rt_ref, hb_ref, wg_ref, wu_ref, wd_ref, o_ref, wg_bf, wu_bf, wd_bf, xbuf, xc_ref, sem):
    c = pl.program_id(0)
    last = pl.num_programs(0) - 1
    depth = xbuf.shape[0]
    slot = c % depth

    def start_gather(chunk, to_slot):
        for r in range(CHUNK_ROWS):
            tok = rt_ref[chunk * CHUNK_ROWS + r]
            pltpu.make_async_copy(hb_ref.at[pl.ds(tok, 1), :], xbuf.at[to_slot, pl.ds(r, 1), :],
                                  sem.at[to_slot]).start()

    def wait_gather(of_slot):
        pltpu.make_async_copy(xbuf.at[of_slot], xbuf.at[of_slot], sem.at[of_slot]).wait()

    @pl.when(c == 0)
    def _():
        for k in range(depth - 1):
            start_gather(jnp.minimum(k, last), k)

    @pl.when((c == 0) | (ce_ref[c] != ce_ref[jnp.maximum(c - 1, 0)]))
    def _():
        wg_bf[...] = wg_ref[0, 0].astype(BF16)
        wu_bf[...] = wu_ref[0, 0].astype(BF16)
        wd_bf[...] = wd_ref[0, 0].astype(BF16)

    wait_gather(slot)
    words = xbuf[slot]
    lo = lax.bitcast_convert_type(words << 16, F32)
    hi = lax.bitcast_convert_type(words & jnp.uint32(0xFFFF0000), F32)
    xc_ref[...] = jnp.concatenate([lo, hi], axis=1).astype(BF16)
    start_gather(jnp.minimum(c + depth - 1, last), (c + depth - 1) % depth)
    xc = xc_ref[...]
    gate = _dot(xc, wg_bf[...])
    up = _dot(xc, wu_bf[...])
    act = (gate * jax.nn.sigmoid(gate) * up).astype(BF16)
    o_ref[...] = _dot(act, wd_bf[...])

    @pl.when(c == last)
    def _():
        for k in range(1, depth):
            wait_gather((c + k) % depth)


def _experts(hb, row_tok, chunk_exp, w_gate, w_up, w_down, layer):
    d = w_gate.shape[-2]
    de = w_gate.shape[-1]
    n_chunks = chunk_exp.shape[0]
    weight = lambda shape: pl.BlockSpec((1, 1) + shape, lambda c, ce, rt: (layer, ce[c], 0, 0))
    grid_spec = pltpu.PrefetchScalarGridSpec(
        num_scalar_prefetch=2,
        grid=(n_chunks,),
        in_specs=[pl.BlockSpec(memory_space=pl.ANY), weight((d, de)), weight((d, de)), weight((de, d))],
        out_specs=pl.BlockSpec((CHUNK_ROWS, d), lambda c, ce, rt: (c, 0)),
        scratch_shapes=[pltpu.VMEM((d, de), BF16), pltpu.VMEM((d, de), BF16), pltpu.VMEM((de, d), BF16),
                        pltpu.VMEM((GATHER_DEPTH, CHUNK_ROWS, d // 2), jnp.uint32),
                        pltpu.VMEM((CHUNK_ROWS, d), BF16),
                        pltpu.SemaphoreType.DMA((GATHER_DEPTH,))],
    )
    return pl.pallas_call(
        _expert_kernel,
        grid_spec=grid_spec,
        out_shape=jax.ShapeDtypeStruct((n_chunks * CHUNK_ROWS, d), F32),
        compiler_params=_cparams("arbitrary"),
        name="moe_experts",
    )(chunk_exp, row_tok, hb, w_gate, w_up, w_down)


def _combine_kernel(x_ref, y0_ref, y1_ref, w_ref, g_ref, o_ref, *, final_norm):
    w = w_ref[...]
    out = x_ref[...] + w[:, 0:1] * y0_ref[...] + w[:, 1:2] * y1_ref[...]
    if final_norm:
        out = _rms_scale(out) * g_ref[...]
    o_ref[...] = out


def _combine(xf, y0, y1, wcol, g, final_norm, tm=512):
    n, d = xf.shape
    row = pl.BlockSpec((tm, d), lambda i: (i, 0))
    return pl.pallas_call(
        functools.partial(_combine_kernel, final_norm=final_norm),
        grid=(n // tm,),
        in_specs=[row, row, row, pl.BlockSpec((tm, 8), lambda i: (i, 0)), pl.BlockSpec((1, d), lambda i: (0, 0))],
        out_specs=row,
        out_shape=jax.ShapeDtypeStruct((n, d), F32),
        compiler_params=_cparams("arbitrary"),
        name="moe_combine",
    )(xf, y0, y1, wcol, g.reshape(1, d))


def _hier_moe(xf, g_ffn, w_rg, b_rg, w_re, b_re, w_gate, w_up, w_down, layer, g_final, final_norm):
    n = xf.shape[0]
    hb, eid, wts, counts = _router(xf, g_ffn, w_rg, b_rg, w_re, b_re)
    row_tok, dest, chunk_exp = _dispatch_plan(eid, counts[:, 0], n)
    ys = _experts(hb, row_tok, chunk_exp, w_gate, w_up, w_down, layer)
    return _combine(xf, _take_rows(ys, dest[0]), _take_rows(ys, dest[1]), wts.T, g_final, final_norm)


def _proj_kernel(x_ref, prow_ref, pcol_ref, gkv_ref, gq_ref, wnat_ref, wvt_ref, wqt_ref,
                 kcv_ref, ks_ref, kw_ref, vst_ref, vwt_ref, qc_ref, qr_ref, gt_ref):
    tm = x_ref.shape[1]
    gd = N_KV_GROUPS * HEAD_DIM
    qd = N_HEADS * HEAD_DIM
    y = _rms_scale(x_ref[0])
    hkv = (y * gkv_ref[...]).astype(BF16)
    hq = (y * gq_ref[...]).astype(BF16)

    nat = _dot(hkv, wnat_ref[...])
    kcv_ref[0] = nat[:, :2 * gd]
    lane = lax.broadcasted_iota(I32, (1, LANES), 1)
    inv = jnp.exp((lane % HALF).astype(F32) * (-jnp.log(ROPE_THETA) / HALF))
    ang = pcol_ref[0].astype(F32) * inv
    ctab = jnp.where(lane < HEAD_DIM, jnp.cos(ang), 0.0)
    stab = jnp.where(lane < HEAD_DIM, jnp.sin(ang), 0.0)
    key = pl.program_id(1) * tm + lax.broadcasted_iota(I32, (tm, 1), 0)
    blk = (key // SLC_BLOCK) % BIAS_ROWS
    onehot = jnp.where(lane == HEAD_DIM + blk, 1.0, 0.0)
    for g in range(N_KV_GROUPS):
        for slot, ref in ((0, ks_ref), (1, kw_ref)):
            lo = 2 * gd + (slot * N_KV_GROUPS + g) * LANES
            t = nat[:, lo:lo + LANES]
            roped = t * ctab + pltpu.roll(t, HEAD_DIM, 1) * stab
            if slot == 0:
                roped = roped + onehot
            ref[0, g] = roped.astype(BF16)

    vt = _dot_nt(wvt_ref[...], hkv)
    ones_rows = _ones_row_block((N_KV_GROUPS, V_ROWS - HEAD_DIM, tm))
    vs3 = jnp.concatenate([vt[:gd].reshape(N_KV_GROUPS, HEAD_DIM, tm), ones_rows], axis=1).astype(BF16)
    vw3 = jnp.concatenate([vt[gd:].reshape(N_KV_GROUPS, HEAD_DIM, tm), ones_rows], axis=1).astype(BF16)
    vst_ref[0, :, 0] = vs3
    for w in range(tm // Q_BLOCK):
        vwt_ref[0, :, w] = vw3[:, :, w * Q_BLOCK:(w + 1) * Q_BLOCK]

    qt = _dot_nt(wqt_ref[...], hq)
    scale = HEAD_DIM ** -0.5 * LOG2E
    q3 = qt[:qd].reshape(N_HEADS, HEAD_DIM, tm)
    qc_ref[0] = (qt[:qd] * scale).astype(BF16)
    frq = lax.broadcasted_iota(I32, (HALF, 1), 0).astype(F32)
    ang_t = jnp.exp(frq * (-jnp.log(ROPE_THETA) / HALF)) * prow_ref[0].astype(F32)
    cos_t = jnp.cos(ang_t)[None]
    sin_t = jnp.sin(ang_t)[None]
    t1 = q3[:, :HALF]
    t2 = q3[:, HALF:]
    qr = jnp.concatenate([t1 * cos_t - t2 * sin_t, t2 * cos_t + t1 * sin_t], axis=1)
    qr_ref[0] = (qr * scale).reshape(qd, tm).astype(BF16)
    gt_ref[0] = jax.nn.sigmoid(qt[qd:])


def _rot_half_cols(w):
    return jnp.concatenate([-w[:, HALF:], w[:, :HALF]], axis=1)


def _projections(x, positions, g_kv, g_q, w_kv, w_q, tm=KEY_TILE):
    b, s, d = x.shape
    gd = N_KV_GROUPS * HEAD_DIM
    qd = N_HEADS * HEAD_DIM
    slot = lambda i: w_kv[:, i * gd:(i + 1) * gd]
    aug = []
    for sl in (2, 4):
        for g in range(N_KV_GROUPS):
            wk = slot(sl)[:, g * HEAD_DIM:(g + 1) * HEAD_DIM]
            aug += [wk, _rot_half_cols(wk)]
    w_nat = jnp.concatenate([slot(0), slot(1)] + aug, axis=1).astype(BF16)
    w_vt = jnp.concatenate([slot(3), slot(5)], axis=1).T.astype(BF16)
    gate_cols = w_q[:, qd:].reshape(d, N_KV_GROUPS, HEADS_PER_GROUP * N_BRANCH)
    gate_cols = jnp.pad(gate_cols, ((0, 0), (0, 0), (0, 16 - HEADS_PER_GROUP * N_BRANCH))).reshape(d, 64)
    w_qt = jnp.concatenate([w_q[:, :qd], gate_cols], axis=1).T.astype(BF16)
    nt = s // tm
    n_nat = w_nat.shape[1]
    const = lambda shape: pl.BlockSpec(shape, lambda i, j: (0,) * len(shape))
    return pl.pallas_call(
        _proj_kernel,
        grid=(b, nt),
        in_specs=[
            pl.BlockSpec((1, tm, d), lambda i, j: (i, j, 0)),
            pl.BlockSpec((1, 1, tm), lambda i, j: (i, 0, j)),
            pl.BlockSpec((1, tm, 1), lambda i, j: (i, j, 0)),
            const((1, d)), const((1, d)), const((d, n_nat)), const((2 * gd, d)), const((qd + 64, d)),
        ],
        out_specs=[
            pl.BlockSpec((1, tm, 2 * gd), lambda i, j: (i, j, 0)),
            pl.BlockSpec((1, N_KV_GROUPS, tm, LANES), lambda i, j: (i, 0, j, 0)),
            pl.BlockSpec((1, N_KV_GROUPS, tm, LANES), lambda i, j: (i, 0, j, 0)),
            pl.BlockSpec((1, N_KV_GROUPS, 1, V_ROWS, tm), lambda i, j: (i, 0, j, 0, 0)),
            pl.BlockSpec((1, N_KV_GROUPS, tm // Q_BLOCK, V_ROWS, Q_BLOCK), lambda i, j: (i, 0, j, 0, 0)),
            pl.BlockSpec((1, qd, tm), lambda i, j: (i, 0, j)),
            pl.BlockSpec((1, qd, tm), lambda i, j: (i, 0, j)),
            pl.BlockSpec((1, 64, tm), lambda i, j: (i, 0, j)),
        ],
        out_shape=[
            jax.ShapeDtypeStruct((b, s, 2 * gd), F32),
            jax.ShapeDtypeStruct((b, N_KV_GROUPS, s, LANES), BF16),
            jax.ShapeDtypeStruct((b, N_KV_GROUPS, s, LANES), BF16),
            jax.ShapeDtypeStruct((b, N_KV_GROUPS, nt, V_ROWS, tm), BF16),
            jax.ShapeDtypeStruct((b, N_KV_GROUPS, s // Q_BLOCK, V_ROWS, Q_BLOCK), BF16),
            jax.ShapeDtypeStruct((b, qd, s), BF16),
            jax.ShapeDtypeStruct((b, qd, s), BF16),
            jax.ShapeDtypeStruct((b, 64, s), F32),
        ],
        compiler_params=_cparams("arbitrary", "arbitrary"),
        name="nsa_projections",
    )(x, positions.reshape(b, 1, s), positions.reshape(b, s, 1), g_kv.reshape(1, d), g_q.reshape(1, d),
      w_nat, w_vt, w_qt)


def _compress_kernel(a_ref, pos_ref, w1_ref, w2_ref, o_ref, *, transposed_out):
    half = w1_ref.shape[0] // 2
    a = a_ref[0, 0]
    top = (a + pos_ref[0:1, :]).astype(BF16)
    bot = (a + pos_ref[1:2, :]).astype(BF16)
    p = _dot(top, w1_ref[:half, :])
    q = _dot(bot, w1_ref[half:, :])
    n = a.shape[0]
    hidden = p + pltpu.roll(q, n - 1, 0)
    act = jax.nn.gelu(hidden, approximate=True).astype(BF16)
    if transposed_out:
        v_t = _dot_nt(w2_ref[...], act)
        o_ref[0, 0] = jnp.concatenate([v_t, _ones_row_block((V_ROWS - HEAD_DIM, n))], axis=0).astype(BF16)
    else:
        o_ref[0, 0] = _dot(act, w2_ref[...]).astype(BF16)


def _compress(a, pos, w1, w2, transposed_out):
    b, g, n, width = a.shape
    hidden = w1.shape[1]
    pos2 = pos.reshape(2, width)
    if transposed_out:
        w2p = w2.T.astype(BF16)
        out_block, out_shape = (1, 1, V_ROWS, n), (b, g, V_ROWS, n)
    else:
        w2p = jnp.pad(w2, ((0, 0), (0, LANES - HEAD_DIM))).astype(BF16)
        out_block, out_shape = (1, 1, n, LANES), (b, g, n, LANES)
    return pl.pallas_call(
        functools.partial(_compress_kernel, transposed_out=transposed_out),
        grid=(b, g),
        in_specs=[
            pl.BlockSpec((1, 1, n, width), lambda i, j: (i, j, 0, 0)),
            pl.BlockSpec((2, width), lambda i, j: (0, 0)),
            pl.BlockSpec((2 * width, hidden), lambda i, j: (0, 0)),
            pl.BlockSpec(w2p.shape, lambda i, j: (0, 0)),
        ],
        out_specs=pl.BlockSpec(out_block, lambda i, j: (i, j, 0, 0)),
        out_shape=jax.ShapeDtypeStruct(out_shape, BF16),
        compiler_params=_cparams("arbitrary", "arbitrary"),
        name="nsa_compress_v" if transposed_out else "nsa_compress_k",
    )(a, pos2, w1.astype(BF16), w2p)


def _heads_on_lanes(ref):
    return jnp.concatenate([ref[0, h * HEAD_DIM:(h + 1) * HEAD_DIM, :] for h in range(HEADS_PER_GROUP)], axis=1)


def _nsa_kernel(qc_ref, qr_ref, g_ref, kc_ref, vct_ref, band_ref, ks_ref, vst_ref, kw_ref, vwt_ref,
                o_ref, bias_ref, score_ref, sa_ref, sb_ref, pa_ref, pb_ref):
    i = pl.program_id(2)
    qb = Q_BLOCK
    width = HEADS_PER_GROUP * qb
    n_cmp = kc_ref.shape[2]
    n_slc = score_ref.shape[0]
    t_q = i * qb + lax.broadcasted_iota(I32, (1, qb), 1)
    t_row = jnp.concatenate([t_q] * HEADS_PER_GROUP, axis=1)
    zeros_pad = jnp.zeros((LANES - HEAD_DIM, width), BF16)

    cc = band_ref.shape[1]
    bpc = cc * CMP_STRIDE // SLC_BLOCK
    qc_t = jnp.concatenate([_heads_on_lanes(qc_ref), zeros_pad], axis=0)
    chunk_scores, maxima = [], []
    for c in range(n_cmp // cc):
        s = _dot(kc_ref[0, 0, c * cc:(c + 1) * cc, :], qc_t)
        cmp_end = (c * cc + lax.broadcasted_iota(I32, (cc, 1), 0)) * CMP_STRIDE + (CMP_BLOCK - 1)
        s = jnp.where(cmp_end <= t_row, s, NEG)
        chunk_scores.append(s)
        maxima.append(jnp.maximum(jnp.max(s, axis=0, keepdims=True), NEG * 1e-10))
    m_fin = functools.reduce(jnp.maximum, maxima)
    shares = []
    for c, (s, mx) in enumerate(zip(chunk_scores, maxima)):
        lhs = jnp.concatenate([vct_ref[0, 0, :, c * cc:(c + 1) * cc], band_ref[...]], axis=0)
        share = _dot(lhs, jnp.exp2(s - mx).astype(BF16))
        shares.append(share * jnp.exp2(mx - m_fin))
    acc_c = shares[0][:V_ROWS]
    for sh in shares[1:]:
        acc_c = acc_c + sh[:V_ROWS]
    rinv = 1.0 / jnp.maximum(acc_c[HEAD_DIM:HEAD_DIM + 1], 1e-30)
    o_cmp = acc_c[:HEAD_DIM] * rinv
    imp_rows = []
    for c, sh in enumerate(shares):
        u = sh[V_ROWS:V_ROWS + bpc]
        if c > 0:
            spill = shares[c - 1][V_ROWS + bpc:V_ROWS + bpc + 8]
            u = jnp.concatenate([u[:8] + spill, u[8:]], axis=0)
        u = u * rinv
        imp_c = u[:, 0:qb]
        for h in range(1, HEADS_PER_GROUP):
            imp_c = imp_c + u[:, h * qb:(h + 1) * qb]
        imp_rows.append(imp_c)
    imp = jnp.concatenate(imp_rows, axis=0)

    j_idx = lax.broadcasted_iota(I32, (n_slc, 1), 0)
    cur = t_q // SLC_BLOCK
    valid = j_idx * SLC_BLOCK <= t_q
    forced = (j_idx == 0) | (j_idx == cur) | (j_idx == cur - 1)
    score_ref[...] = jnp.where(valid & jnp.logical_not(forced), imp, -1.0)

    def pick(_, carry, rows):
        sc = score_ref[0:rows, :]
        best = jnp.max(sc, axis=0, keepdims=True)
        idx = jnp.min(jnp.where(sc == best, j_idx[0:rows], n_slc), axis=0, keepdims=True)
        score_ref[0:rows, :] = jnp.where(j_idx[0:rows] == idx, PICKED, sc)
        return carry

    pick_rows = min(PICK_ROWS, n_slc)
    need = (i * qb + qb - 1) // (SLC_BLOCK * pick_rows)
    for r in range(n_slc // pick_rows):
        @pl.when(need == r)
        def _(rows=(r + 1) * pick_rows):
            lax.fori_loop(0, N_SELECT - N_FORCED, functools.partial(pick, rows=rows), 0)
    bias = jnp.where(forced | (score_ref[...] == PICKED), 0.0, NEG).astype(BF16)
    bias_ref[...] = jnp.concatenate([bias] * HEADS_PER_GROUP, axis=1)

    qr_heads = _heads_on_lanes(qr_ref)
    zeros_sel = jnp.zeros((LANES - HEAD_DIM - BIAS_ROWS, width), BF16)
    tk = KEY_TILE
    blocks_per_tile = tk // SLC_BLOCK

    def scores(kt, s_ref, masked):
        b0 = pl.multiple_of((kt * blocks_per_tile) // BIAS_ROWS * BIAS_ROWS, BIAS_ROWS)
        q_aug = jnp.concatenate([qr_heads, bias_ref[pl.ds(b0, BIAS_ROWS), :], zeros_sel], axis=0)
        k0 = pl.multiple_of(kt * tk, tk)
        s = _dot(ks_ref[0, 0, pl.ds(k0, tk), :], q_aug)
        if masked:
            s = jnp.where(k0 + lax.broadcasted_iota(I32, (tk, 1), 0) <= t_row, s, NEG)
        s_ref[...] = s
        part = jnp.max(s.reshape(tk // MAX_ROWS, MAX_ROWS, width), axis=0)
        return jnp.max(part, axis=0, keepdims=True)

    def values(kt, p_ref):
        return _dot(vst_ref[0, 0, jnp.maximum(kt, 0)], p_ref[...])

    def trip(kk, carry, next_scores):
        m, acc, beta, cmax_a, cmax_b = carry
        kt = 2 * kk
        live = jnp.where(kk > 0, 1.0, 0.0)
        acc = acc + live * (beta * values(kt - 2, pa_ref) + values(kt - 1, pb_ref))
        m1 = jnp.maximum(m, cmax_a)
        pa_ref[...] = jnp.exp2((sa_ref[...] - m1).astype(BF16))
        m2 = jnp.maximum(m1, cmax_b)
        pb_ref[...] = jnp.exp2((sb_ref[...] - m2).astype(BF16))
        beta = jnp.exp2(m1 - m2)
        if next_scores is not None:
            cmax_a = scores(kt + 2, sa_ref, next_scores)
            cmax_b = scores(kt + 3, sb_ref, next_scores)
        return m2, (jnp.exp2(m - m1) * beta) * acc, beta, cmax_a, cmax_b

    @pl.when((pl.program_id(0) == 0) & (pl.program_id(1) == 0) & (i == 0))
    def _():
        pa_ref[...] = jnp.zeros_like(pa_ref)
        pb_ref[...] = jnp.zeros_like(pb_ref)

    n_full = (i * qb) // tk
    n_unmasked = jnp.maximum(n_full // 2 - 1, 0)
    carry = (jnp.full((1, width), NEG, F32), jnp.zeros((V_ROWS, width), F32), jnp.ones((1, width), F32),
             scores(0, sa_ref, True), scores(1, sb_ref, True))
    carry = lax.fori_loop(0, n_unmasked, functools.partial(trip, next_scores=False), carry)
    carry = lax.fori_loop(n_unmasked, n_full // 2, functools.partial(trip, next_scores=True), carry)
    qr_t = jnp.concatenate([qr_heads, zeros_pad], axis=0)
    n_win = WINDOW // qb + 1
    s_w = []
    for r in range(n_win):
        j = i - (n_win - 1) + r
        k0 = pl.multiple_of(jnp.maximum(j, 0) * qb, qb)
        s = _dot(kw_ref[0, 0, pl.ds(k0, qb), :], qr_t)
        if r in (0, n_win - 1):
            kpos = j * qb + lax.broadcasted_iota(I32, (qb, 1), 0)
            diff = t_row - kpos
            s = jnp.where((diff >= 0) & (diff < WINDOW) & (kpos >= 0), s, NEG)
        else:
            s = s + jnp.where(j >= 0, 0.0, NEG)
        s_w.append(s)
    m_w = s_w[0].max(axis=0, keepdims=True)
    for r in range(1, n_win):
        m_w = jnp.maximum(m_w, s_w[r].max(axis=0, keepdims=True))

    last = n_full // 2
    _, acc_s, beta, _, _ = trip(last, carry, None)
    acc_s = acc_s + beta * values(2 * last, pa_ref) + values(2 * last + 1, pb_ref)
    o_slc = acc_s[:HEAD_DIM] / acc_s[HEAD_DIM:HEAD_DIM + 1]

    acc_w = jnp.zeros((V_ROWS, width), F32)
    for r in range(n_win):
        jc = jnp.maximum(i - (n_win - 1) + r, 0)
        acc_w = acc_w + _dot(vwt_ref[0, 0, jc], jnp.exp2((s_w[r] - m_w).astype(BF16)))
    o_win = acc_w[:HEAD_DIM] / acc_w[HEAD_DIM:HEAD_DIM + 1]

    gts = g_ref[0]
    for h in range(HEADS_PER_GROUP):
        sl = slice(h * qb, (h + 1) * qb)
        r0 = h * N_BRANCH
        o = (gts[r0:r0 + 1] * o_cmp[:, sl] + gts[r0 + 1:r0 + 2] * o_slc[:, sl] + gts[r0 + 2:r0 + 3] * o_win[:, sl])
        o_ref[0, h * HEAD_DIM:(h + 1) * HEAD_DIM, :] = o.astype(BF16)


def _nsa_attention(qc_t, qr_t, gates_t, k_cmp, v_cmp_t, band, k_slc, v_slc_t, k_win, v_win_t):
    b, qd, s = qc_t.shape
    g = N_KV_GROUPS
    n_cmp = k_cmp.shape[2]
    n_slc = s // SLC_BLOCK
    nt, tk = v_slc_t.shape[2], v_slc_t.shape[4]
    rows = HEADS_PER_GROUP * HEAD_DIM
    width = HEADS_PER_GROUP * Q_BLOCK
    qspec = pl.BlockSpec((1, rows, Q_BLOCK), lambda bi, gi, i: (bi, gi, i))
    per_bg = lambda shape: pl.BlockSpec((1, 1) + shape, lambda bi, gi, i: (bi, gi) + (0,) * len(shape))
    return pl.pallas_call(
        _nsa_kernel,
        grid=(b, g, s // Q_BLOCK),
        in_specs=[
            qspec, qspec,
            pl.BlockSpec((1, 16, Q_BLOCK), lambda bi, gi, i: (bi, gi, i)),
            per_bg((n_cmp, LANES)), per_bg((V_ROWS, n_cmp)),
            pl.BlockSpec(band.shape, lambda bi, gi, i: (0, 0)),
            per_bg((s, LANES)), per_bg((nt, V_ROWS, tk)),
            per_bg((s, LANES)), per_bg((s // Q_BLOCK, V_ROWS, Q_BLOCK)),
        ],
        out_specs=qspec,
        out_shape=jax.ShapeDtypeStruct((b, qd, s), BF16),
        scratch_shapes=[
            pltpu.VMEM((n_slc, width), BF16),
            pltpu.VMEM((n_slc, Q_BLOCK), F32),
            pltpu.VMEM((tk, width), F32), pltpu.VMEM((tk, width), F32),
            pltpu.VMEM((tk, width), BF16), pltpu.VMEM((tk, width), BF16),
        ],
        compiler_params=_cparams("arbitrary", "arbitrary", "arbitrary"),
        name="nsa_attention",
    )(qc_t, qr_t, gates_t, k_cmp, v_cmp_t, band, k_slc, v_slc_t, k_win, v_win_t)


def _selection_band(n_cmp):
    cc = min(CMP_CHUNK, n_cmp)
    bpc = cc * CMP_STRIDE // SLC_BLOCK
    c0 = jnp.arange(cc) * CMP_STRIDE
    j0 = jnp.arange(bpc + 16) * SLC_BLOCK
    ov = jnp.minimum(c0[None, :] + CMP_BLOCK, j0[:, None] + SLC_BLOCK) - jnp.maximum(c0[None, :], j0[:, None])
    return (jnp.clip(ov, 0).astype(F32) / CMP_BLOCK).astype(BF16)


def _out_proj_kernel(x_ref, ot_ref, w_ref, o_ref):
    o_ref[0] = x_ref[0] + _dot_tn(ot_ref[0], w_ref[...])


def _out_proj(x, o_t, w_out, tm=512):
    b, s, d = x.shape
    qd = o_t.shape[1]
    return pl.pallas_call(
        _out_proj_kernel,
        grid=(b, s // tm),
        in_specs=[
            pl.BlockSpec((1, tm, d), lambda i, j: (i, j, 0)),
            pl.BlockSpec((1, qd, tm), lambda i, j: (i, 0, j)),
            pl.BlockSpec((qd, d), lambda i, j: (0, 0)),
        ],
        out_specs=pl.BlockSpec((1, tm, d), lambda i, j: (i, j, 0)),
        out_shape=jax.ShapeDtypeStruct((b, s, d), F32),
        compiler_params=_cparams("arbitrary", "arbitrary"),
        name="nsa_out_proj",
    )(x, o_t, w_out.astype(BF16))


def _nsa_layer(x, positions, g_kv, g_q, w_kv, cmp_pos_k, cmp_w1_k, cmp_w2_k, cmp_pos_v, cmp_w1_v, cmp_w2_v,
               w_in, w_out):
    b, s, d = x.shape
    g = N_KV_GROUPS
    kcv, k_slc, k_win, v_slc_t, v_win_t, qc_t, qr_t, gates_t = _projections(x, positions, g_kv, g_q, w_kv, w_in)
    n_half = s // CMP_STRIDE
    halves = kcv.reshape(b, n_half, CMP_STRIDE, 2, g, HEAD_DIM).transpose(0, 3, 4, 1, 2, 5)
    halves = halves.reshape(b, 2, g, n_half, CMP_STRIDE * HEAD_DIM)
    k_cmp = _compress(halves[:, 0], cmp_pos_k, cmp_w1_k, cmp_w2_k, transposed_out=False)
    v_cmp_t = _compress(halves[:, 1], cmp_pos_v, cmp_w1_v, cmp_w2_v, transposed_out=True)
    o_t = _nsa_attention(qc_t, qr_t, gates_t, k_cmp, v_cmp_t, _selection_band(n_half), k_slc, v_slc_t, k_win,
                         v_win_t)
    return _out_proj(x, o_t, w_out)


def kernel(x, positions, norm_mix, norm_ffn, conv_w_in, conv_w, conv_w_out, norm_kv, w_kv, cmp_pos_k, cmp_w1_k, cmp_w2_k, cmp_pos_v, cmp_w1_v, cmp_w2_v, attn_w_in, attn_w_out, router_group_w, router_group_b, router_expert_w, router_expert_b, expert_w_gate, expert_w_up, expert_w_down, norm_final):
    b, s, d = x.shape
    assert s % (2 * KEY_TILE) == 0 and (s // SLC_BLOCK) % BIAS_ROWS == 0 and WINDOW % Q_BLOCK == 0

    def moe(xs, l, final_norm):
        return _hier_moe(xs.reshape(b * s, d), norm_ffn[l], router_group_w[l], router_group_b[l],
                         router_expert_w[l], router_expert_b[l], expert_w_gate, expert_w_up,
                         expert_w_down, l, norm_final, final_norm).reshape(b, s, d)

    x = _mixer(x, norm_mix[0], conv_w_in[0], conv_w[0], conv_w_out[0])
    x = moe(x, 0, False)
    x = _nsa_layer(x, positions, norm_kv, norm_mix[1], w_kv, cmp_pos_k, cmp_w1_k, cmp_w2_k,
                   cmp_pos_v, cmp_w1_v, cmp_w2_v, attn_w_in[0], attn_w_out[0])
    return moe(x, 1, True)
```

```python
import functools

import jax
import jax.numpy as jnp
from jax import lax
from jax.experimental import pallas as pl
from jax.experimental.pallas import tpu as pltpu

F32 = jnp.float32
BF16 = jnp.bfloat16
I32 = jnp.int32

RMS_EPS = 1e-6
ROPE_THETA = 10000.0
HEAD_DIM = 64
HALF = HEAD_DIM // 2
N_HEADS = 16
N_KV_GROUPS = 4
HEADS_PER_GROUP = N_HEADS // N_KV_GROUPS
N_BRANCH = 3
CMP_BLOCK = 32
CMP_STRIDE = 16
SLC_BLOCK = 64
N_SELECT = 16
WINDOW = 512
Q_BLOCK = 256
FORCE_BONUS = 1000.0
N_GROUPS = 4
EXPERTS_PER_GROUP = 8
N_EXPERTS = N_GROUPS * EXPERTS_PER_GROUP
CHUNK_ROWS = 256
GATHER_DEPTH = 3
NEG = -1e30
LANES = 128
KEY_TILE = 512
BIAS_ROWS = 16
MAX_ROWS = 64
CMP_CHUNK = 256
N_FORCED = 3
PICK_ROWS = 64
PICKED = -2.0
V_ROWS = 80
LOG2E = 1.4426950408889634
VMEM_LIMIT = 56 * 1024 * 1024


def _cparams(*sem):
    return pltpu.CompilerParams(dimension_semantics=sem, vmem_limit_bytes=VMEM_LIMIT)


def _rms_scale(x):
    return x * lax.rsqrt(jnp.mean(x * x, axis=-1, keepdims=True) + RMS_EPS)


def _dot(a, b):
    return jnp.dot(a, b, preferred_element_type=F32)


def _dot_nt(a, b):
    return lax.dot_general(a, b, (((1,), (1,)), ((), ())), preferred_element_type=F32)


def _dot_tn(a, b):
    return lax.dot_general(a, b, (((0,), (0,)), ((), ())), preferred_element_type=F32)


def _ones_row_block(shape):
    row = lax.broadcasted_iota(I32, shape, len(shape) - 2)
    return jnp.where(row == 0, 1.0, 0.0).astype(F32)


def _mixer_kernel(x_ref, g_ref, win_ref, cw_ref, wout_ref, o_ref, tail_ref):
    d = x_ref.shape[-1]
    tm = x_ref.shape[1]

    @pl.when(pl.program_id(1) == 0)
    def _():
        tail_ref[...] = jnp.zeros_like(tail_ref)

    x = x_ref[0]
    h = (_rms_scale(x) * g_ref[...]).astype(BF16)
    proj = _dot(h, win_ref[...])
    b_gate = proj[:, :d]
    u = proj[:, d:2 * d] * proj[:, 2 * d:]
    row = lax.broadcasted_iota(I32, (tm, 1), 0)
    prev1 = tail_ref[7:8, :]
    prev2 = tail_ref[6:7, :]
    u1 = jnp.where(row >= 1, pltpu.roll(u, 1, 0), prev1)
    u2 = jnp.where(row >= 2, pltpu.roll(u, 2, 0), jnp.where(row == 1, prev1, prev2))
    y = cw_ref[2:3, :] * u + cw_ref[1:2, :] * u1 + cw_ref[0:1, :] * u2
    tail_ref[...] = u[tm - 8:, :]
    z = (b_gate * y).astype(BF16)
    o_ref[0] = x + _dot(z, wout_ref[...])


def _mixer(x, g, w_in, conv_w, w_out, tm=512):
    b, s, d = x.shape
    return pl.pallas_call(
        _mixer_kernel,
        grid=(b, s // tm),
        in_specs=[
            pl.BlockSpec((1, tm, d), lambda i, j: (i, j, 0)),
            pl.BlockSpec((1, d), lambda i, j: (0, 0)),
            pl.BlockSpec((d, 3 * d), lambda i, j: (0, 0)),
            pl.BlockSpec((3, d), lambda i, j: (0, 0)),
            pl.BlockSpec((d, d), lambda i, j: (0, 0)),
        ],
        out_specs=pl.BlockSpec((1, tm, d), lambda i, j: (i, j, 0)),
        out_shape=jax.ShapeDtypeStruct((b, s, d), F32),
        scratch_shapes=[pltpu.VMEM((8, d), F32)],
        compiler_params=_cparams("arbitrary", "arbitrary"),
        name="conv_mixer",
    )(x, g.reshape(1, d), w_in.astype(BF16), conv_w, w_out.astype(BF16))


def _router_kernel(x_ref, g_ref, wt_ref, b_ref, hb_ref, eid_ref, wts_ref, counts_ref, cnt_ref):
    tm = x_ref.shape[0]
    h = _rms_scale(x_ref[...]) * g_ref[...]
    bits = lax.bitcast_convert_type(h.astype(BF16).astype(F32), jnp.uint32)
    half = h.shape[1] // 2
    hb_ref[...] = (bits[:, :half] >> 16) | (bits[:, half:] & jnp.uint32(0xFFFF0000))
    w = wt_ref[...]
    w_hi = w.astype(BF16)
    w_lo = (w - w_hi.astype(F32)).astype(BF16)
    h_hi = h.astype(BF16)
    h_lo = (h - h_hi.astype(F32)).astype(BF16)
    lt = _dot_nt(w_hi, h_hi) + _dot_nt(w_hi, h_lo) + _dot_nt(w_lo, h_hi) + b_ref[...]
    r8 = lax.broadcasted_iota(I32, (8, 1), 0)
    gl = jnp.where(r8 < N_GROUPS, lt[0:8], -jnp.inf)
    gmax = jnp.max(gl, axis=0, keepdims=True)
    gtop = jnp.min(jnp.where(gl == gmax, r8, 8), axis=0, keepdims=True)
    p_g = 1.0 / jnp.sum(jnp.exp(gl - gmax), axis=0, keepdims=True)
    chosen = lt[8:16]
    for k in range(1, N_GROUPS):
        chosen = jnp.where(gtop == k, lt[8 + 8 * k:16 + 8 * k], chosen)
    v0 = jnp.max(chosen, axis=0, keepdims=True)
    i0 = jnp.min(jnp.where(chosen == v0, r8, 8), axis=0, keepdims=True)
    rest = jnp.where(r8 == i0, -jnp.inf, chosen)
    v1 = jnp.max(rest, axis=0, keepdims=True)
    i1 = jnp.min(jnp.where(rest == v1, r8, 8), axis=0, keepdims=True)
    e = jnp.exp(v1 - v0)
    w0 = p_g / (1.0 + e)
    w1 = p_g * e / (1.0 + e)
    e0 = gtop * EXPERTS_PER_GROUP + i0
    e1 = gtop * EXPERTS_PER_GROUP + i1
    wts_ref[...] = jnp.where(r8 == 0, w0, jnp.where(r8 == 1, w1, 0.0))

    @pl.when(pl.program_id(0) == 0)
    def _():
        cnt_ref[...] = jnp.zeros_like(cnt_ref)

    rexp = lax.broadcasted_iota(I32, (N_EXPERTS, 1), 0)
    tri = (lax.broadcasted_iota(I32, (tm, tm), 0) <= lax.broadcasted_iota(I32, (tm, tm), 1)).astype(BF16)
    oh0 = rexp == e0
    oh1 = rexp == e1
    cum0 = _dot(oh0.astype(BF16), tri)
    cum1 = _dot(oh1.astype(BF16), tri)
    before0 = cnt_ref[...]
    before1 = before0 + cum0[:, tm - 1:tm]
    rank0 = jnp.sum(jnp.where(oh0, before0 + cum0, 0.0), axis=0, keepdims=True) - 1.0
    rank1 = jnp.sum(jnp.where(oh1, before1 + cum1, 0.0), axis=0, keepdims=True) - 1.0
    cnt_ref[...] = before1 + cum1[:, tm - 1:tm]
    counts_ref[...] = jnp.broadcast_to(cnt_ref[...], counts_ref.shape).astype(I32)
    eid_ref[...] = jnp.where(r8 == 0, e0, jnp.where(r8 == 1, e1, jnp.where(
        r8 == 2, rank0.astype(I32), jnp.where(r8 == 3, rank1.astype(I32), 0))))


def _router(xf, g, w_rg, b_rg, w_re, b_re, tm=512):
    n, d = xf.shape
    rows = 8 + N_EXPERTS
    wt = jnp.zeros((rows, d), F32).at[:N_GROUPS].set(w_rg.T).at[8:].set(w_re.T)
    bt = jnp.zeros((rows, 1), F32).at[:N_GROUPS, 0].set(b_rg).at[8:, 0].set(b_re)
    return pl.pallas_call(
        _router_kernel,
        grid=(n // tm,),
        in_specs=[
            pl.BlockSpec((tm, d), lambda i: (i, 0)),
            pl.BlockSpec((1, d), lambda i: (0, 0)),
            pl.BlockSpec((rows, d), lambda i: (0, 0)),
            pl.BlockSpec((rows, 1), lambda i: (0, 0)),
        ],
        out_specs=[
            pl.BlockSpec((tm, d // 2), lambda i: (i, 0)),
            pl.BlockSpec((8, tm), lambda i: (0, i)),
            pl.BlockSpec((8, tm), lambda i: (0, i)),
            pl.BlockSpec((N_EXPERTS, LANES), lambda i: (0, 0)),
        ],
        out_shape=[
            jax.ShapeDtypeStruct((n, d // 2), jnp.uint32),
            jax.ShapeDtypeStruct((8, n), I32),
            jax.ShapeDtypeStruct((8, n), F32),
            jax.ShapeDtypeStruct((N_EXPERTS, LANES), I32),
        ],
        scratch_shapes=[pltpu.VMEM((N_EXPERTS, 1), F32)],
        compiler_params=_cparams("arbitrary"),
        name="moe_router",
    )(xf, g.reshape(1, d), wt, bt)


def _take_rows(a, idx):
    return a.at[idx].get(mode="promise_in_bounds")


def _dispatch_plan(eid, counts, n_tok):
    a_exp = eid[0:2].reshape(-1)
    rank = eid[2:4].reshape(-1)
    n_asg = a_exp.shape[0]
    a_tok = jnp.arange(n_asg, dtype=I32) % n_tok
    padded = (counts + CHUNK_ROWS - 1) // CHUNK_ROWS * CHUNK_ROWS
    pend = jnp.cumsum(padded)
    pstart = pend - padded
    dest = (_take_rows(pstart, a_exp) + rank).astype(I32)
    n_chunks = -(-n_asg // CHUNK_ROWS) + N_EXPERTS
    row_tok = jnp.zeros((n_chunks * CHUNK_ROWS,), I32).at[dest].set(
        a_tok, unique_indices=True, mode="promise_in_bounds")
    chunk_row0 = jnp.arange(n_chunks, dtype=I32) * CHUNK_ROWS
    chunk_exp = jnp.minimum(jnp.sum((pend[None, :] <= chunk_row0[:, None]).astype(I32), axis=1), N_EXPERTS - 1)
    return row_tok, dest.reshape(2, n_tok), chunk_exp


def _expert_kernel(ce_ref, rt_ref, hb_ref, wg_ref, wu_ref, wd_ref, o_ref, wg_bf, wu_bf, wd_bf, xbuf, xc_ref, sem):
    c = pl.program_id(0)
    last = pl.num_programs(0) - 1
    depth = xbuf.shape[0]
    slot = c % depth

    def start_gather(chunk, to_slot):
        for r in range(CHUNK_ROWS):
            tok = rt_ref[chunk * CHUNK_ROWS + r]
            pltpu.make_async_copy(hb_ref.at[pl.ds(tok, 1), :], xbuf.at[to_slot, pl.ds(r, 1), :],
                                  sem.at[to_slot]).start()

    def wait_gather(of_slot):
        pltpu.make_async_copy(xbuf.at[of_slot], xbuf.at[of_slot], sem.at[of_slot]).wait()

    @pl.when(c == 0)
    def _():
        for k in range(depth - 1):
            start_gather(jnp.minimum(k, last), k)

    @pl.when((c == 0) | (ce_ref[c] != ce_ref[jnp.maximum(c - 1, 0)]))
    def _():
        wg_bf[...] = wg_ref[0, 0].astype(BF16)
        wu_bf[...] = wu_ref[0, 0].astype(BF16)
        wd_bf[...] = wd_ref[0, 0].astype(BF16)

    wait_gather(slot)
    words = xbuf[slot]
    lo = lax.bitcast_convert_type(words << 16, F32)
    hi = lax.bitcast_convert_type(words & jnp.uint32(0xFFFF0000), F32)
    xc_ref[...] = jnp.concatenate([lo, hi], axis=1).astype(BF16)
    start_gather(jnp.minimum(c + depth - 1, last), (c + depth - 1) % depth)
    xc = xc_ref[...]
    gate = _dot(xc, wg_bf[...])
    up = _dot(xc, wu_bf[...])
    act = (gate * jax.nn.sigmoid(gate) * up).astype(BF16)
    o_ref[...] = _dot(act, wd_bf[...])

    @pl.when(c == last)
    def _():
        for k in range(1, depth):
            wait_gather((c + k) % depth)


def _experts(hb, row_tok, chunk_exp, w_gate, w_up, w_down, layer):
    d = w_gate.shape[-2]
    de = w_gate.shape[-1]
    n_chunks = chunk_exp.shape[0]
    weight = lambda shape: pl.BlockSpec((1, 1) + shape, lambda c, ce, rt: (layer, ce[c], 0, 0))
    grid_spec = pltpu.PrefetchScalarGridSpec(
        num_scalar_prefetch=2,
        grid=(n_chunks,),
        in_specs=[pl.BlockSpec(memory_space=pl.ANY), weight((d, de)), weight((d, de)), weight((de, d))],
        out_specs=pl.BlockSpec((CHUNK_ROWS, d), lambda c, ce, rt: (c, 0)),
        scratch_shapes=[pltpu.VMEM((d, de), BF16), pltpu.VMEM((d, de), BF16), pltpu.VMEM((de, d), BF16),
                        pltpu.VMEM((GATHER_DEPTH, CHUNK_ROWS, d // 2), jnp.uint32),
                        pltpu.VMEM((CHUNK_ROWS, d), BF16),
                        pltpu.SemaphoreType.DMA((GATHER_DEPTH,))],
    )
    return pl.pallas_call(
        _expert_kernel,
        grid_spec=grid_spec,
        out_shape=jax.ShapeDtypeStruct((n_chunks * CHUNK_ROWS, d), F32),
        compiler_params=_cparams("arbitrary"),
        name="moe_experts",
    )(chunk_exp, row_tok, hb, w_gate, w_up, w_down)


def _combine_kernel(x_ref, y0_ref, y1_ref, w_ref, g_ref, o_ref, *, final_norm):
    w = w_ref[...]
    out = x_ref[...] + w[:, 0:1] * y0_ref[...] + w[:, 1:2] * y1_ref[...]
    if final_norm:
        out = _rms_scale(out) * g_ref[...]
    o_ref[...] = out


def _combine(xf, y0, y1, wcol, g, final_norm, tm=512):
    n, d = xf.shape
    row = pl.BlockSpec((tm, d), lambda i: (i, 0))
    return pl.pallas_call(
        functools.partial(_combine_kernel, final_norm=final_norm),
        grid=(n // tm,),
        in_specs=[row, row, row, pl.BlockSpec((tm, 8), lambda i: (i, 0)), pl.BlockSpec((1, d), lambda i: (0, 0))],
        out_specs=row,
        out_shape=jax.ShapeDtypeStruct((n, d), F32),
        compiler_params=_cparams("arbitrary"),
        name="moe_combine",
    )(xf, y0, y1, wcol, g.reshape(1, d))


def _hier_moe(xf, g_ffn, w_rg, b_rg, w_re, b_re, w_gate, w_up, w_down, layer, g_final, final_norm):
    n = xf.shape[0]
    hb, eid, wts, counts = _router(xf, g_ffn, w_rg, b_rg, w_re, b_re)
    row_tok, dest, chunk_exp = _dispatch_plan(eid, counts[:, 0], n)
    ys = _experts(hb, row_tok, chunk_exp, w_gate, w_up, w_down, layer)
    y = _take_rows(ys, dest)
    return _combine(xf, y[0], y[1], wts.T, g_final, final_norm)


def _proj_kernel(x_ref, prow_ref, pcol_ref, gkv_ref, gq_ref, wnat_ref, wvt_ref, wqt_ref,
                 kcv_ref, ks_ref, kw_ref, vst_ref, vwt_ref, qc_ref, qr_ref, gt_ref):
    tm = x_ref.shape[1]
    gd = N_KV_GROUPS * HEAD_DIM
    qd = N_HEADS * HEAD_DIM
    y = _rms_scale(x_ref[0])
    hkv = (y * gkv_ref[...]).astype(BF16)
    hq = (y * gq_ref[...]).astype(BF16)

    nat = _dot(hkv, wnat_ref[...])
    kcv_ref[0] = nat[:, :2 * gd]
    lane = lax.broadcasted_iota(I32, (1, LANES), 1)
    inv = jnp.exp((lane % HALF).astype(F32) * (-jnp.log(ROPE_THETA) / HALF))
    ang = pcol_ref[0].astype(F32) * inv
    ctab = jnp.where(lane < HEAD_DIM, jnp.cos(ang), 0.0)
    stab = jnp.where(lane < HEAD_DIM, jnp.sin(ang), 0.0)
    key = pl.program_id(1) * tm + lax.broadcasted_iota(I32, (tm, 1), 0)
    blk = (key // SLC_BLOCK) % BIAS_ROWS
    onehot = jnp.where(lane == HEAD_DIM + blk, 1.0, 0.0)
    for g in range(N_KV_GROUPS):
        for slot, ref in ((0, ks_ref), (1, kw_ref)):
            lo = 2 * gd + (slot * N_KV_GROUPS + g) * LANES
            t = nat[:, lo:lo + LANES]
            roped = t * ctab + pltpu.roll(t, HEAD_DIM, 1) * stab
            if slot == 0:
                roped = roped + onehot
            ref[0, g] = roped.astype(BF16)

    vt = _dot_nt(wvt_ref[...], hkv)
    ones_rows = _ones_row_block((N_KV_GROUPS, V_ROWS - HEAD_DIM, tm))
    vs3 = jnp.concatenate([vt[:gd].reshape(N_KV_GROUPS, HEAD_DIM, tm), ones_rows], axis=1).astype(BF16)
    vw3 = jnp.concatenate([vt[gd:].reshape(N_KV_GROUPS, HEAD_DIM, tm), ones_rows], axis=1).astype(BF16)
    vst_ref[0, :, 0] = vs3
    for w in range(tm // Q_BLOCK):
        vwt_ref[0, :, w] = vw3[:, :, w * Q_BLOCK:(w + 1) * Q_BLOCK]

    qt = _dot_nt(wqt_ref[...], hq)
    scale = HEAD_DIM ** -0.5 * LOG2E
    q3 = qt[:qd].reshape(N_HEADS, HEAD_DIM, tm)
    qc_ref[0] = (qt[:qd] * scale).astype(BF16)
    frq = lax.broadcasted_iota(I32, (HALF, 1), 0).astype(F32)
    ang_t = jnp.exp(frq * (-jnp.log(ROPE_THETA) / HALF)) * prow_ref[0].astype(F32)
    cos_t = jnp.cos(ang_t)[None]
    sin_t = jnp.sin(ang_t)[None]
    t1 = q3[:, :HALF]
    t2 = q3[:, HALF:]
    qr = jnp.concatenate([t1 * cos_t - t2 * sin_t, t2 * cos_t + t1 * sin_t], axis=1)
    qr_ref[0] = (qr * scale).reshape(qd, tm).astype(BF16)
    gt_ref[0] = jax.nn.sigmoid(qt[qd:])


def _rot_half_cols(w):
    return jnp.concatenate([-w[:, HALF:], w[:, :HALF]], axis=1)


def _projections(x, positions, g_kv, g_q, w_kv, w_q, tm=KEY_TILE):
    b, s, d = x.shape
    gd = N_KV_GROUPS * HEAD_DIM
    qd = N_HEADS * HEAD_DIM
    slot = lambda i: w_kv[:, i * gd:(i + 1) * gd]
    aug = []
    for sl in (2, 4):
        for g in range(N_KV_GROUPS):
            wk = slot(sl)[:, g * HEAD_DIM:(g + 1) * HEAD_DIM]
            aug += [wk, _rot_half_cols(wk)]
    w_nat = jnp.concatenate([slot(0), slot(1)] + aug, axis=1).astype(BF16)
    w_vt = jnp.concatenate([slot(3), slot(5)], axis=1).T.astype(BF16)
    gate_cols = w_q[:, qd:].reshape(d, N_KV_GROUPS, HEADS_PER_GROUP * N_BRANCH)
    gate_cols = jnp.pad(gate_cols, ((0, 0), (0, 0), (0, 16 - HEADS_PER_GROUP * N_BRANCH))).reshape(d, 64)
    w_qt = jnp.concatenate([w_q[:, :qd], gate_cols], axis=1).T.astype(BF16)
    nt = s // tm
    n_nat = w_nat.shape[1]
    const = lambda shape: pl.BlockSpec(shape, lambda i, j: (0,) * len(shape))
    return pl.pallas_call(
        _proj_kernel,
        grid=(b, nt),
        in_specs=[
            pl.BlockSpec((1, tm, d), lambda i, j: (i, j, 0)),
            pl.BlockSpec((1, 1, tm), lambda i, j: (i, 0, j)),
            pl.BlockSpec((1, tm, 1), lambda i, j: (i, j, 0)),
            const((1, d)), const((1, d)), const((d, n_nat)), const((2 * gd, d)), const((qd + 64, d)),
        ],
        out_specs=[
            pl.BlockSpec((1, tm, 2 * gd), lambda i, j: (i, j, 0)),
            pl.BlockSpec((1, N_KV_GROUPS, tm, LANES), lambda i, j: (i, 0, j, 0)),
            pl.BlockSpec((1, N_KV_GROUPS, tm, LANES), lambda i, j: (i, 0, j, 0)),
            pl.BlockSpec((1, N_KV_GROUPS, 1, V_ROWS, tm), lambda i, j: (i, 0, j, 0, 0)),
            pl.BlockSpec((1, N_KV_GROUPS, tm // Q_BLOCK, V_ROWS, Q_BLOCK), lambda i, j: (i, 0, j, 0, 0)),
            pl.BlockSpec((1, qd, tm), lambda i, j: (i, 0, j)),
            pl.BlockSpec((1, qd, tm), lambda i, j: (i, 0, j)),
            pl.BlockSpec((1, 64, tm), lambda i, j: (i, 0, j)),
        ],
        out_shape=[
            jax.ShapeDtypeStruct((b, s, 2 * gd), F32),
            jax.ShapeDtypeStruct((b, N_KV_GROUPS, s, LANES), BF16),
            jax.ShapeDtypeStruct((b, N_KV_GROUPS, s, LANES), BF16),
            jax.ShapeDtypeStruct((b, N_KV_GROUPS, nt, V_ROWS, tm), BF16),
            jax.ShapeDtypeStruct((b, N_KV_GROUPS, s // Q_BLOCK, V_ROWS, Q_BLOCK), BF16),
            jax.ShapeDtypeStruct((b, qd, s), BF16),
            jax.ShapeDtypeStruct((b, qd, s), BF16),
            jax.ShapeDtypeStruct((b, 64, s), F32),
        ],
        compiler_params=_cparams("arbitrary", "arbitrary"),
        name="nsa_projections",
    )(x, positions.reshape(b, 1, s), positions.reshape(b, s, 1), g_kv.reshape(1, d), g_q.reshape(1, d),
      w_nat, w_vt, w_qt)


def _compress_kernel(a_ref, pos_ref, w1_ref, w2_ref, o_ref, *, transposed_out):
    half = w1_ref.shape[0] // 2
    a = a_ref[0, 0]
    top = (a + pos_ref[0:1, :]).astype(BF16)
    bot = (a + pos_ref[1:2, :]).astype(BF16)
    p = _dot(top, w1_ref[:half, :])
    q = _dot(bot, w1_ref[half:, :])
    n = a.shape[0]
    hidden = p + pltpu.roll(q, n - 1, 0)
    act = jax.nn.gelu(hidden, approximate=True).astype(BF16)
    if transposed_out:
        v_t = _dot_nt(w2_ref[...], act)
        o_ref[0, 0] = jnp.concatenate([v_t, _ones_row_block((V_ROWS - HEAD_DIM, n))], axis=0).astype(BF16)
    else:
        o_ref[0, 0] = _dot(act, w2_ref[...]).astype(BF16)


def _compress(a, pos, w1, w2, transposed_out):
    b, g, n, width = a.shape
    hidden = w1.shape[1]
    pos2 = pos.reshape(2, width)
    if transposed_out:
        w2p = w2.T.astype(BF16)
        out_block, out_shape = (1, 1, V_ROWS, n), (b, g, V_ROWS, n)
    else:
        w2p = jnp.pad(w2, ((0, 0), (0, LANES - HEAD_DIM))).astype(BF16)
        out_block, out_shape = (1, 1, n, LANES), (b, g, n, LANES)
    return pl.pallas_call(
        functools.partial(_compress_kernel, transposed_out=transposed_out),
        grid=(b, g),
        in_specs=[
            pl.BlockSpec((1, 1, n, width), lambda i, j: (i, j, 0, 0)),
            pl.BlockSpec((2, width), lambda i, j: (0, 0)),
            pl.BlockSpec((2 * width, hidden), lambda i, j: (0, 0)),
            pl.BlockSpec(w2p.shape, lambda i, j: (0, 0)),
        ],
        out_specs=pl.BlockSpec(out_block, lambda i, j: (i, j, 0, 0)),
        out_shape=jax.ShapeDtypeStruct(out_shape, BF16),
        compiler_params=_cparams("arbitrary", "arbitrary"),
        name="nsa_compress_v" if transposed_out else "nsa_compress_k",
    )(a, pos2, w1.astype(BF16), w2p)


def _heads_on_lanes(ref):
    return jnp.concatenate([ref[0, h * HEAD_DIM:(h + 1) * HEAD_DIM, :] for h in range(HEADS_PER_GROUP)], axis=1)


def _nsa_kernel(qc_ref, qr_ref, g_ref, kc_ref, vct_ref, band_ref, ks_ref, vst_ref, kw_ref, vwt_ref,
                o_ref, bias_ref, score_ref, sa_ref, sb_ref, pa_ref, pb_ref):
    i = pl.program_id(2)
    qb = Q_BLOCK
    width = HEADS_PER_GROUP * qb
    n_cmp = kc_ref.shape[2]
    n_slc = score_ref.shape[0]
    t_q = i * qb + lax.broadcasted_iota(I32, (1, qb), 1)
    t_row = jnp.concatenate([t_q] * HEADS_PER_GROUP, axis=1)
    zeros_pad = jnp.zeros((LANES - HEAD_DIM, width), BF16)

    cc = band_ref.shape[1]
    bpc = cc * CMP_STRIDE // SLC_BLOCK
    qc_t = jnp.concatenate([_heads_on_lanes(qc_ref), zeros_pad], axis=0)
    chunk_scores, maxima = [], []
    for c in range(n_cmp // cc):
        s = _dot(kc_ref[0, 0, c * cc:(c + 1) * cc, :], qc_t)
        cmp_end = (c * cc + lax.broadcasted_iota(I32, (cc, 1), 0)) * CMP_STRIDE + (CMP_BLOCK - 1)
        s = jnp.where(cmp_end <= t_row, s, NEG)
        chunk_scores.append(s)
        maxima.append(jnp.maximum(jnp.max(s, axis=0, keepdims=True), NEG * 1e-10))
    m_fin = functools.reduce(jnp.maximum, maxima)
    shares = []
    for c, (s, mx) in enumerate(zip(chunk_scores, maxima)):
        lhs = jnp.concatenate([vct_ref[0, 0, :, c * cc:(c + 1) * cc], band_ref[...]], axis=0)
        share = _dot(lhs, jnp.exp2(s - mx).astype(BF16))
        shares.append(share * jnp.exp2(mx - m_fin))
    acc_c = shares[0][:V_ROWS]
    for sh in shares[1:]:
        acc_c = acc_c + sh[:V_ROWS]
    rinv = 1.0 / jnp.maximum(acc_c[HEAD_DIM:HEAD_DIM + 1], 1e-30)
    o_cmp = acc_c[:HEAD_DIM] * rinv
    imp_rows = []
    for c, sh in enumerate(shares):
        u = sh[V_ROWS:V_ROWS + bpc]
        if c > 0:
            spill = shares[c - 1][V_ROWS + bpc:V_ROWS + bpc + 8]
            u = jnp.concatenate([u[:8] + spill, u[8:]], axis=0)
        u = u * rinv
        imp_c = u[:, 0:qb]
        for h in range(1, HEADS_PER_GROUP):
            imp_c = imp_c + u[:, h * qb:(h + 1) * qb]
        imp_rows.append(imp_c)
    imp = jnp.concatenate(imp_rows, axis=0)

    j_idx = lax.broadcasted_iota(I32, (n_slc, 1), 0)
    cur = t_q // SLC_BLOCK
    valid = j_idx * SLC_BLOCK <= t_q
    forced = (j_idx == 0) | (j_idx == cur) | (j_idx == cur - 1)
    score_ref[...] = jnp.where(valid & jnp.logical_not(forced), imp, -1.0)

    def pick(_, carry, rows):
        sc = score_ref[0:rows, :]
        best = jnp.max(sc, axis=0, keepdims=True)
        idx = jnp.min(jnp.where(sc == best, j_idx[0:rows], n_slc), axis=0, keepdims=True)
        score_ref[0:rows, :] = jnp.where(j_idx[0:rows] == idx, PICKED, sc)
        return carry

    pick_rows = min(PICK_ROWS, n_slc)
    need = (i * qb + qb - 1) // (SLC_BLOCK * pick_rows)
    for r in range(n_slc // pick_rows):
        @pl.when(need == r)
        def _(rows=(r + 1) * pick_rows):
            lax.fori_loop(0, N_SELECT - N_FORCED, functools.partial(pick, rows=rows), 0)
    bias = jnp.where(forced | (score_ref[...] == PICKED), 0.0, NEG).astype(BF16)
    bias_ref[...] = jnp.concatenate([bias] * HEADS_PER_GROUP, axis=1)

    qr_heads = _heads_on_lanes(qr_ref)
    zeros_sel = jnp.zeros((LANES - HEAD_DIM - BIAS_ROWS, width), BF16)
    tk = KEY_TILE
    blocks_per_tile = tk // SLC_BLOCK

    def scores(kt, s_ref, masked):
        b0 = pl.multiple_of((kt * blocks_per_tile) // BIAS_ROWS * BIAS_ROWS, BIAS_ROWS)
        q_aug = jnp.concatenate([qr_heads, bias_ref[pl.ds(b0, BIAS_ROWS), :], zeros_sel], axis=0)
        k0 = pl.multiple_of(kt * tk, tk)
        s = _dot(ks_ref[0, 0, pl.ds(k0, tk), :], q_aug)
        if masked:
            s = jnp.where(k0 + lax.broadcasted_iota(I32, (tk, 1), 0) <= t_row, s, NEG)
        s_ref[...] = s
        part = jnp.max(s.reshape(tk // MAX_ROWS, MAX_ROWS, width), axis=0)
        return jnp.max(part, axis=0, keepdims=True)

    def values(kt, p_ref):
        return _dot(vst_ref[0, 0, jnp.maximum(kt, 0)], p_ref[...])

    def trip(kk, carry, next_scores):
        m, acc, beta, cmax_a, cmax_b = carry
        kt = 2 * kk
        live = jnp.where(kk > 0, 1.0, 0.0)
        acc = acc + live * (beta * values(kt - 2, pa_ref) + values(kt - 1, pb_ref))
        m1 = jnp.maximum(m, cmax_a)
        pa_ref[...] = jnp.exp2((sa_ref[...] - m1).astype(BF16))
        m2 = jnp.maximum(m1, cmax_b)
        pb_ref[...] = jnp.exp2((sb_ref[...] - m2).astype(BF16))
        beta = jnp.exp2(m1 - m2)
        if next_scores is not None:
            cmax_a = scores(kt + 2, sa_ref, next_scores)
            cmax_b = scores(kt + 3, sb_ref, next_scores)
        return m2, (jnp.exp2(m - m1) * beta) * acc, beta, cmax_a, cmax_b

    @pl.when((pl.program_id(0) == 0) & (pl.program_id(1) == 0) & (i == 0))
    def _():
        pa_ref[...] = jnp.zeros_like(pa_ref)
        pb_ref[...] = jnp.zeros_like(pb_ref)

    n_full = (i * qb) // tk
    n_unmasked = jnp.maximum(n_full // 2 - 1, 0)
    carry = (jnp.full((1, width), NEG, F32), jnp.zeros((V_ROWS, width), F32), jnp.ones((1, width), F32),
             scores(0, sa_ref, True), scores(1, sb_ref, True))
    carry = lax.fori_loop(0, n_unmasked, functools.partial(trip, next_scores=False), carry)
    carry = lax.fori_loop(n_unmasked, n_full // 2, functools.partial(trip, next_scores=True), carry)
    qr_t = jnp.concatenate([qr_heads, zeros_pad], axis=0)
    n_win = WINDOW // qb + 1
    s_w = []
    for r in range(n_win):
        j = i - (n_win - 1) + r
        k0 = pl.multiple_of(jnp.maximum(j, 0) * qb, qb)
        s = _dot(kw_ref[0, 0, pl.ds(k0, qb), :], qr_t)
        if r in (0, n_win - 1):
            kpos = j * qb + lax.broadcasted_iota(I32, (qb, 1), 0)
            diff = t_row - kpos
            s = jnp.where((diff >= 0) & (diff < WINDOW) & (kpos >= 0), s, NEG)
        else:
            s = s + jnp.where(j >= 0, 0.0, NEG)
        s_w.append(s)
    m_w = s_w[0].max(axis=0, keepdims=True)
    for r in range(1, n_win):
        m_w = jnp.maximum(m_w, s_w[r].max(axis=0, keepdims=True))

    last = n_full // 2
    _, acc_s, beta, _, _ = trip(last, carry, None)
    acc_s = acc_s + beta * values(2 * last, pa_ref) + values(2 * last + 1, pb_ref)
    o_slc = acc_s[:HEAD_DIM] / acc_s[HEAD_DIM:HEAD_DIM + 1]

    acc_w = jnp.zeros((V_ROWS, width), F32)
    for r in range(n_win):
        jc = jnp.maximum(i - (n_win - 1) + r, 0)
        acc_w = acc_w + _dot(vwt_ref[0, 0, jc], jnp.exp2((s_w[r] - m_w).astype(BF16)))
    o_win = acc_w[:HEAD_DIM] / acc_w[HEAD_DIM:HEAD_DIM + 1]

    gts = g_ref[0]
    for h in range(HEADS_PER_GROUP):
        sl = slice(h * qb, (h + 1) * qb)
        r0 = h * N_BRANCH
        o = (gts[r0:r0 + 1] * o_cmp[:, sl] + gts[r0 + 1:r0 + 2] * o_slc[:, sl] + gts[r0 + 2:r0 + 3] * o_win[:, sl])
        o_ref[0, h * HEAD_DIM:(h + 1) * HEAD_DIM, :] = o.astype(BF16)


def _nsa_attention(qc_t, qr_t, gates_t, k_cmp, v_cmp_t, band, k_slc, v_slc_t, k_win, v_win_t):
    b, qd, s = qc_t.shape
    g = N_KV_GROUPS
    n_cmp = k_cmp.shape[2]
    n_slc = s // SLC_BLOCK
    nt, tk = v_slc_t.shape[2], v_slc_t.shape[4]
    rows = HEADS_PER_GROUP * HEAD_DIM
    width = HEADS_PER_GROUP * Q_BLOCK
    qspec = pl.BlockSpec((1, rows, Q_BLOCK), lambda bi, gi, i: (bi, gi, i))
    per_bg = lambda shape: pl.BlockSpec((1, 1) + shape, lambda bi, gi, i: (bi, gi) + (0,) * len(shape))
    return pl.pallas_call(
        _nsa_kernel,
        grid=(b, g, s // Q_BLOCK),
        in_specs=[
            qspec, qspec,
            pl.BlockSpec((1, 16, Q_BLOCK), lambda bi, gi, i: (bi, gi, i)),
            per_bg((n_cmp, LANES)), per_bg((V_ROWS, n_cmp)),
            pl.BlockSpec(band.shape, lambda bi, gi, i: (0, 0)),
            per_bg((s, LANES)), per_bg((nt, V_ROWS, tk)),
            per_bg((s, LANES)), per_bg((s // Q_BLOCK, V_ROWS, Q_BLOCK)),
        ],
        out_specs=qspec,
        out_shape=jax.ShapeDtypeStruct((b, qd, s), BF16),
        scratch_shapes=[
            pltpu.VMEM((n_slc, width), BF16),
            pltpu.VMEM((n_slc, Q_BLOCK), F32),
            pltpu.VMEM((tk, width), F32), pltpu.VMEM((tk, width), F32),
            pltpu.VMEM((tk, width), BF16), pltpu.VMEM((tk, width), BF16),
        ],
        compiler_params=_cparams("arbitrary", "arbitrary", "arbitrary"),
        name="nsa_attention",
    )(qc_t, qr_t, gates_t, k_cmp, v_cmp_t, band, k_slc, v_slc_t, k_win, v_win_t)


def _selection_band(n_cmp):
    cc = min(CMP_CHUNK, n_cmp)
    bpc = cc * CMP_STRIDE // SLC_BLOCK
    c0 = jnp.arange(cc) * CMP_STRIDE
    j0 = jnp.arange(bpc + 16) * SLC_BLOCK
    ov = jnp.minimum(c0[None, :] + CMP_BLOCK, j0[:, None] + SLC_BLOCK) - jnp.maximum(c0[None, :], j0[:, None])
    return (jnp.clip(ov, 0).astype(F32) / CMP_BLOCK).astype(BF16)


def _out_proj_kernel(x_ref, ot_ref, w_ref, o_ref):
    o_ref[0] = x_ref[0] + _dot_tn(ot_ref[0], w_ref[...])


def _out_proj(x, o_t, w_out, tm=512):
    b, s, d = x.shape
    qd = o_t.shape[1]
    return pl.pallas_call(
        _out_proj_kernel,
        grid=(b, s // tm),
        in_specs=[
            pl.BlockSpec((1, tm, d), lambda i, j: (i, j, 0)),
            pl.BlockSpec((1, qd, tm), lambda i, j: (i, 0, j)),
            pl.BlockSpec((qd, d), lambda i, j: (0, 0)),
        ],
        out_specs=pl.BlockSpec((1, tm, d), lambda i, j: (i, j, 0)),
        out_shape=jax.ShapeDtypeStruct((b, s, d), F32),
        compiler_params=_cparams("arbitrary", "arbitrary"),
        name="nsa_out_proj",
    )(x, o_t, w_out.astype(BF16))


def _nsa_layer(x, positions, g_kv, g_q, w_kv, cmp_pos_k, cmp_w1_k, cmp_w2_k, cmp_pos_v, cmp_w1_v, cmp_w2_v,
               w_in, w_out):
    b, s, d = x.shape
    g = N_KV_GROUPS
    kcv, k_slc, k_win, v_slc_t, v_win_t, qc_t, qr_t, gates_t = _projections(x, positions, g_kv, g_q, w_kv, w_in)
    n_half = s // CMP_STRIDE
    halves = kcv.reshape(b, n_half, CMP_STRIDE, 2, g, HEAD_DIM).transpose(0, 3, 4, 1, 2, 5)
    halves = halves.reshape(b, 2, g, n_half, CMP_STRIDE * HEAD_DIM)
    k_cmp = _compress(halves[:, 0], cmp_pos_k, cmp_w1_k, cmp_w2_k, transposed_out=False)
    v_cmp_t = _compress(halves[:, 1], cmp_pos_v, cmp_w1_v, cmp_w2_v, transposed_out=True)
    o_t = _nsa_attention(qc_t, qr_t, gates_t, k_cmp, v_cmp_t, _selection_band(n_half), k_slc, v_slc_t, k_win,
                         v_win_t)
    return _out_proj(x, o_t, w_out)


def kernel(x, positions, norm_mix, norm_ffn, conv_w_in, conv_w, conv_w_out, norm_kv, w_kv, cmp_pos_k, cmp_w1_k, cmp_w2_k, cmp_pos_v, cmp_w1_v, cmp_w2_v, attn_w_in, attn_w_out, router_group_w, router_group_b, router_expert_w, router_expert_b, expert_w_gate, expert_w_up, expert_w_down, norm_final):
    b, s, d = x.shape
    assert s % (2 * KEY_TILE) == 0 and (s // SLC_BLOCK) % BIAS_ROWS == 0 and WINDOW % Q_BLOCK == 0

    def moe(xs, l, final_norm):
        return _hier_moe(xs.reshape(b * s, d), norm_ffn[l], router_group_w[l], router_group_b[l],
                         router_expert_w[l], router_expert_b[l], expert_w_gate, expert_w_up,
                         expert_w_down, l, norm_final, final_norm).reshape(b, s, d)

    x = _mixer(x, norm_mix[0], conv_w_in[0], conv_w[0], conv_w_out[0])
    x = moe(x, 0, False)
    x = _nsa_layer(x, positions, norm_kv, norm_mix[1], w_kv, cmp_pos_k, cmp_w1_k, cmp_w2_k,
                   cmp_pos_v, cmp_w1_v, cmp_w2_v, attn_w_in[0], attn_w_out[0])
    return moe(x, 1, True)
```

```python
import functools

import jax
import jax.numpy as jnp
from jax import lax
from jax.experimental import pallas as pl
from jax.experimental.pallas import tpu as pltpu

F32 = jnp.float32
BF16 = jnp.bfloat16
I32 = jnp.int32

RMS_EPS = 1e-6
ROPE_THETA = 10000.0
HEAD_DIM = 64
HALF = HEAD_DIM // 2
N_HEADS = 16
N_KV_GROUPS = 4
HEADS_PER_GROUP = N_HEADS // N_KV_GROUPS
N_BRANCH = 3
CMP_BLOCK = 32
CMP_STRIDE = 16
SLC_BLOCK = 64
N_SELECT = 16
WINDOW = 512
Q_BLOCK = 256
FORCE_BONUS = 1000.0
N_GROUPS = 4
EXPERTS_PER_GROUP = 8
N_EXPERTS = N_GROUPS * EXPERTS_PER_GROUP
CHUNK_ROWS = 256
GATHER_DEPTH = 3
NEG = -1e30
LANES = 128
KEY_TILE = 512
BIAS_ROWS = 16
MAX_ROWS = 64
CMP_CHUNK = 256
N_FORCED = 3
PICK_ROWS = 64
PICKED = -2.0
V_ROWS = 80
LOG2E = 1.4426950408889634
VMEM_LIMIT = 56 * 1024 * 1024


def _cparams(*sem):
    return pltpu.CompilerParams(dimension_semantics=sem, vmem_limit_bytes=VMEM_LIMIT)


def _rms_scale(x):
    return x * lax.rsqrt(jnp.mean(x * x, axis=-1, keepdims=True) + RMS_EPS)


def _dot(a, b):
    return jnp.dot(a, b, preferred_element_type=F32)


def _dot_nt(a, b):
    return lax.dot_general(a, b, (((1,), (1,)), ((), ())), preferred_element_type=F32)


def _dot_tn(a, b):
    return lax.dot_general(a, b, (((0,), (0,)), ((), ())), preferred_element_type=F32)


def _ones_row_block(shape):
    row = lax.broadcasted_iota(I32, shape, len(shape) - 2)
    return jnp.where(row == 0, 1.0, 0.0).astype(F32)


def _mixer_kernel(x_ref, g_ref, win_ref, cw_ref, wout_ref, o_ref, tail_ref):
    d = x_ref.shape[-1]
    tm = x_ref.shape[1]

    @pl.when(pl.program_id(1) == 0)
    def _():
        tail_ref[...] = jnp.zeros_like(tail_ref)

    x = x_ref[0]
    h = (_rms_scale(x) * g_ref[...]).astype(BF16)
    proj = _dot(h, win_ref[...])
    b_gate = proj[:, :d]
    u = proj[:, d:2 * d] * proj[:, 2 * d:]
    row = lax.broadcasted_iota(I32, (tm, 1), 0)
    prev1 = tail_ref[7:8, :]
    prev2 = tail_ref[6:7, :]
    u1 = jnp.where(row >= 1, pltpu.roll(u, 1, 0), prev1)
    u2 = jnp.where(row >= 2, pltpu.roll(u, 2, 0), jnp.where(row == 1, prev1, prev2))
    y = cw_ref[2:3, :] * u + cw_ref[1:2, :] * u1 + cw_ref[0:1, :] * u2
    tail_ref[...] = u[tm - 8:, :]
    z = (b_gate * y).astype(BF16)
    o_ref[0] = x + _dot(z, wout_ref[...])


def _mixer(x, g, w_in, conv_w, w_out, tm=512):
    b, s, d = x.shape
    return pl.pallas_call(
        _mixer_kernel,
        grid=(b, s // tm),
        in_specs=[
            pl.BlockSpec((1, tm, d), lambda i, j: (i, j, 0)),
            pl.BlockSpec((1, d), lambda i, j: (0, 0)),
            pl.BlockSpec((d, 3 * d), lambda i, j: (0, 0)),
            pl.BlockSpec((3, d), lambda i, j: (0, 0)),
            pl.BlockSpec((d, d), lambda i, j: (0, 0)),
        ],
        out_specs=pl.BlockSpec((1, tm, d), lambda i, j: (i, j, 0)),
        out_shape=jax.ShapeDtypeStruct((b, s, d), F32),
        scratch_shapes=[pltpu.VMEM((8, d), F32)],
        compiler_params=_cparams("arbitrary", "arbitrary"),
        name="conv_mixer",
    )(x, g.reshape(1, d), w_in.astype(BF16), conv_w, w_out.astype(BF16))


def _router_kernel(x_ref, g_ref, wt_ref, b_ref, hb_ref, eid_ref, wts_ref, counts_ref, cnt_ref):
    tm = x_ref.shape[0]
    h = _rms_scale(x_ref[...]) * g_ref[...]
    bits = lax.bitcast_convert_type(h.astype(BF16).astype(F32), jnp.uint32)
    half = h.shape[1] // 2
    hb_ref[...] = (bits[:, :half] >> 16) | (bits[:, half:] & jnp.uint32(0xFFFF0000))
    w = wt_ref[...]
    w_hi = w.astype(BF16)
    w_lo = (w - w_hi.astype(F32)).astype(BF16)
    h_hi = h.astype(BF16)
    h_lo = (h - h_hi.astype(F32)).astype(BF16)
    lt = _dot_nt(w_hi, h_hi) + _dot_nt(w_hi, h_lo) + _dot_nt(w_lo, h_hi) + b_ref[...]
    r8 = lax.broadcasted_iota(I32, (8, 1), 0)
    gl = jnp.where(r8 < N_GROUPS, lt[0:8], -jnp.inf)
    gmax = jnp.max(gl, axis=0, keepdims=True)
    gtop = jnp.min(jnp.where(gl == gmax, r8, 8), axis=0, keepdims=True)
    p_g = 1.0 / jnp.sum(jnp.exp(gl - gmax), axis=0, keepdims=True)
    chosen = lt[8:16]
    for k in range(1, N_GROUPS):
        chosen = jnp.where(gtop == k, lt[8 + 8 * k:16 + 8 * k], chosen)
    v0 = jnp.max(chosen, axis=0, keepdims=True)
    i0 = jnp.min(jnp.where(chosen == v0, r8, 8), axis=0, keepdims=True)
    rest = jnp.where(r8 == i0, -jnp.inf, chosen)
    v1 = jnp.max(rest, axis=0, keepdims=True)
    i1 = jnp.min(jnp.where(rest == v1, r8, 8), axis=0, keepdims=True)
    e = jnp.exp(v1 - v0)
    w0 = p_g / (1.0 + e)
    w1 = p_g * e / (1.0 + e)
    e0 = gtop * EXPERTS_PER_GROUP + i0
    e1 = gtop * EXPERTS_PER_GROUP + i1
    wts_ref[...] = jnp.where(r8 == 0, w0, jnp.where(r8 == 1, w1, 0.0))

    @pl.when(pl.program_id(0) == 0)
    def _():
        cnt_ref[...] = jnp.zeros_like(cnt_ref)

    rexp = lax.broadcasted_iota(I32, (N_EXPERTS, 1), 0)
    tri = (lax.broadcasted_iota(I32, (tm, tm), 0) <= lax.broadcasted_iota(I32, (tm, tm), 1)).astype(BF16)
    oh0 = rexp == e0
    oh1 = rexp == e1
    cum0 = _dot(oh0.astype(BF16), tri)
    cum1 = _dot(oh1.astype(BF16), tri)
    before0 = cnt_ref[...]
    before1 = before0 + cum0[:, tm - 1:tm]
    rank0 = jnp.sum(jnp.where(oh0, before0 + cum0, 0.0), axis=0, keepdims=True) - 1.0
    rank1 = jnp.sum(jnp.where(oh1, before1 + cum1, 0.0), axis=0, keepdims=True) - 1.0
    cnt_ref[...] = before1 + cum1[:, tm - 1:tm]
    counts_ref[...] = jnp.broadcast_to(cnt_ref[...], counts_ref.shape).astype(I32)
    eid_ref[...] = jnp.where(r8 == 0, e0, jnp.where(r8 == 1, e1, jnp.where(
        r8 == 2, rank0.astype(I32), jnp.where(r8 == 3, rank1.astype(I32), 0))))


def _router(xf, g, w_rg, b_rg, w_re, b_re, tm=512):
    n, d = xf.shape
    rows = 8 + N_EXPERTS
    wt = jnp.zeros((rows, d), F32).at[:N_GROUPS].set(w_rg.T).at[8:].set(w_re.T)
    bt = jnp.zeros((rows, 1), F32).at[:N_GROUPS, 0].set(b_rg).at[8:, 0].set(b_re)
    return pl.pallas_call(
        _router_kernel,
        grid=(n // tm,),
        in_specs=[
            pl.BlockSpec((tm, d), lambda i: (i, 0)),
            pl.BlockSpec((1, d), lambda i: (0, 0)),
            pl.BlockSpec((rows, d), lambda i: (0, 0)),
            pl.BlockSpec((rows, 1), lambda i: (0, 0)),
        ],
        out_specs=[
            pl.BlockSpec((tm, d // 2), lambda i: (i, 0)),
            pl.BlockSpec((8, tm), lambda i: (0, i)),
            pl.BlockSpec((8, tm), lambda i: (0, i)),
            pl.BlockSpec((N_EXPERTS, LANES), lambda i: (0, 0)),
        ],
        out_shape=[
            jax.ShapeDtypeStruct((n, d // 2), jnp.uint32),
            jax.ShapeDtypeStruct((8, n), I32),
            jax.ShapeDtypeStruct((8, n), F32),
            jax.ShapeDtypeStruct((N_EXPERTS, LANES), I32),
        ],
        scratch_shapes=[pltpu.VMEM((N_EXPERTS, 1), F32)],
        compiler_params=_cparams("arbitrary"),
        name="moe_router",
    )(xf, g.reshape(1, d), wt, bt)


def _take_rows(a, idx):
    return a.at[idx].get(mode="promise_in_bounds")


def _dispatch_plan(eid, counts, n_tok):
    a_exp = eid[0:2].reshape(-1)
    rank = eid[2:4].reshape(-1)
    n_asg = a_exp.shape[0]
    a_tok = jnp.arange(n_asg, dtype=I32) % n_tok
    padded = (counts + CHUNK_ROWS - 1) // CHUNK_ROWS * CHUNK_ROWS
    pend = jnp.cumsum(padded)
    pstart = pend - padded
    dest = (_take_rows(pstart, a_exp) + rank).astype(I32)
    n_chunks = -(-n_asg // CHUNK_ROWS) + N_EXPERTS
    row_tok = jnp.zeros((n_chunks * CHUNK_ROWS,), I32).at[dest].set(
        a_tok, unique_indices=True, mode="promise_in_bounds")
    chunk_row0 = jnp.arange(n_chunks, dtype=I32) * CHUNK_ROWS
    chunk_exp = jnp.minimum(jnp.sum((pend[None, :] <= chunk_row0[:, None]).astype(I32), axis=1), N_EXPERTS - 1)
    return row_tok, dest.reshape(2, n_tok), chunk_exp


def _expert_kernel(ce_ref, rt_ref, hb_ref, wg_ref, wu_ref, wd_ref, o_ref, wg_bf, wu_bf, wd_bf, xbuf, xc_ref, sem):
    c = pl.program_id(0)
    last = pl.num_programs(0) - 1
    depth = xbuf.shape[0]
    slot = c % depth

    def start_gather(chunk, to_slot):
        for r in range(CHUNK_ROWS):
            tok = rt_ref[chunk * CHUNK_ROWS + r]
            pltpu.make_async_copy(hb_ref.at[pl.ds(tok, 1), :], xbuf.at[to_slot, pl.ds(r, 1), :],
                                  sem.at[to_slot]).start(priority=r % 2)

    def wait_gather(of_slot):
        pltpu.make_async_copy(xbuf.at[of_slot], xbuf.at[of_slot], sem.at[of_slot]).wait()

    @pl.when(c == 0)
    def _():
        for k in range(depth - 1):
            start_gather(jnp.minimum(k, last), k)

    @pl.when((c == 0) | (ce_ref[c] != ce_ref[jnp.maximum(c - 1, 0)]))
    def _():
        wg_bf[...] = wg_ref[0, 0].astype(BF16)
        wu_bf[...] = wu_ref[0, 0].astype(BF16)
        wd_bf[...] = wd_ref[0, 0].astype(BF16)

    wait_gather(slot)
    words = xbuf[slot]
    lo = lax.bitcast_convert_type(words << 16, F32)
    hi = lax.bitcast_convert_type(words & jnp.uint32(0xFFFF0000), F32)
    xc_ref[...] = jnp.concatenate([lo, hi], axis=1).astype(BF16)
    start_gather(jnp.minimum(c + depth - 1, last), (c + depth - 1) % depth)
    xc = xc_ref[...]
    gate = _dot(xc, wg_bf[...])
    up = _dot(xc, wu_bf[...])
    act = (gate * jax.nn.sigmoid(gate) * up).astype(BF16)
    o_ref[...] = _dot(act, wd_bf[...])

    @pl.when(c == last)
    def _():
        for k in range(1, depth):
            wait_gather((c + k) % depth)


def _experts(hb, row_tok, chunk_exp, w_gate, w_up, w_down, layer):
    d = w_gate.shape[-2]
    de = w_gate.shape[-1]
    n_chunks = chunk_exp.shape[0]
    weight = lambda shape: pl.BlockSpec((1, 1) + shape, lambda c, ce, rt: (layer, ce[c], 0, 0))
    grid_spec = pltpu.PrefetchScalarGridSpec(
        num_scalar_prefetch=2,
        grid=(n_chunks,),
        in_specs=[pl.BlockSpec(memory_space=pl.ANY), weight((d, de)), weight((d, de)), weight((de, d))],
        out_specs=pl.BlockSpec((CHUNK_ROWS, d), lambda c, ce, rt: (c, 0)),
        scratch_shapes=[pltpu.VMEM((d, de), BF16), pltpu.VMEM((d, de), BF16), pltpu.VMEM((de, d), BF16),
                        pltpu.VMEM((GATHER_DEPTH, CHUNK_ROWS, d // 2), jnp.uint32),
                        pltpu.VMEM((CHUNK_ROWS, d), BF16),
                        pltpu.SemaphoreType.DMA((GATHER_DEPTH,))],
    )
    return pl.pallas_call(
        _expert_kernel,
        grid_spec=grid_spec,
        out_shape=jax.ShapeDtypeStruct((n_chunks * CHUNK_ROWS, d), F32),
        compiler_params=_cparams("arbitrary"),
        name="moe_experts",
    )(chunk_exp, row_tok, hb, w_gate, w_up, w_down)


def _combine_kernel(x_ref, y0_ref, y1_ref, w_ref, g_ref, o_ref, *, final_norm):
    w = w_ref[...]
    out = x_ref[...] + w[:, 0:1] * y0_ref[...] + w[:, 1:2] * y1_ref[...]
    if final_norm:
        out = _rms_scale(out) * g_ref[...]
    o_ref[...] = out


def _combine(xf, y0, y1, wcol, g, final_norm, tm=512):
    n, d = xf.shape
    row = pl.BlockSpec((tm, d), lambda i: (i, 0))
    return pl.pallas_call(
        functools.partial(_combine_kernel, final_norm=final_norm),
        grid=(n // tm,),
        in_specs=[row, row, row, pl.BlockSpec((tm, 8), lambda i: (i, 0)), pl.BlockSpec((1, d), lambda i: (0, 0))],
        out_specs=row,
        out_shape=jax.ShapeDtypeStruct((n, d), F32),
        compiler_params=_cparams("arbitrary"),
        name="moe_combine",
    )(xf, y0, y1, wcol, g.reshape(1, d))


def _hier_moe(xf, g_ffn, w_rg, b_rg, w_re, b_re, w_gate, w_up, w_down, layer, g_final, final_norm):
    n = xf.shape[0]
    hb, eid, wts, counts = _router(xf, g_ffn, w_rg, b_rg, w_re, b_re)
    row_tok, dest, chunk_exp = _dispatch_plan(eid, counts[:, 0], n)
    ys = _experts(hb, row_tok, chunk_exp, w_gate, w_up, w_down, layer)
    return _combine(xf, _take_rows(ys, dest[0]), _take_rows(ys, dest[1]), wts.T, g_final, final_norm)


def _proj_kernel(x_ref, prow_ref, pcol_ref, gkv_ref, gq_ref, wnat_ref, wvt_ref, wqt_ref,
                 kcv_ref, ks_ref, kw_ref, vst_ref, vwt_ref, qc_ref, qr_ref, gt_ref):
    tm = x_ref.shape[1]
    gd = N_KV_GROUPS * HEAD_DIM
    qd = N_HEADS * HEAD_DIM
    y = _rms_scale(x_ref[0])
    hkv = (y * gkv_ref[...]).astype(BF16)
    hq = (y * gq_ref[...]).astype(BF16)

    nat = _dot(hkv, wnat_ref[...])
    kcv_ref[0] = nat[:, :2 * gd]
    lane = lax.broadcasted_iota(I32, (1, LANES), 1)
    inv = jnp.exp((lane % HALF).astype(F32) * (-jnp.log(ROPE_THETA) / HALF))
    ang = pcol_ref[0].astype(F32) * inv
    ctab = jnp.where(lane < HEAD_DIM, jnp.cos(ang), 0.0)
    stab = jnp.where(lane < HEAD_DIM, jnp.sin(ang), 0.0)
    key = pl.program_id(1) * tm + lax.broadcasted_iota(I32, (tm, 1), 0)
    blk = (key // SLC_BLOCK) % BIAS_ROWS
    onehot = jnp.where(lane == HEAD_DIM + blk, 1.0, 0.0)
    for g in range(N_KV_GROUPS):
        for slot, ref in ((0, ks_ref), (1, kw_ref)):
            lo = 2 * gd + (slot * N_KV_GROUPS + g) * LANES
            t = nat[:, lo:lo + LANES]
            roped = t * ctab + pltpu.roll(t, HEAD_DIM, 1) * stab
            if slot == 0:
                roped = roped + onehot
            ref[0, g] = roped.astype(BF16)

    vt = _dot_nt(wvt_ref[...], hkv)
    ones_rows = _ones_row_block((N_KV_GROUPS, V_ROWS - HEAD_DIM, tm))
    vs3 = jnp.concatenate([vt[:gd].reshape(N_KV_GROUPS, HEAD_DIM, tm), ones_rows], axis=1).astype(BF16)
    vw3 = jnp.concatenate([vt[gd:].reshape(N_KV_GROUPS, HEAD_DIM, tm), ones_rows], axis=1).astype(BF16)
    vst_ref[0, :, 0] = vs3
    for w in range(tm // Q_BLOCK):
        vwt_ref[0, :, w] = vw3[:, :, w * Q_BLOCK:(w + 1) * Q_BLOCK]

    qt = _dot_nt(wqt_ref[...], hq)
    scale = HEAD_DIM ** -0.5 * LOG2E
    q3 = qt[:qd].reshape(N_HEADS, HEAD_DIM, tm)
    qc_ref[0] = (qt[:qd] * scale).astype(BF16)
    frq = lax.broadcasted_iota(I32, (HALF, 1), 0).astype(F32)
    ang_t = jnp.exp(frq * (-jnp.log(ROPE_THETA) / HALF)) * prow_ref[0].astype(F32)
    cos_t = jnp.cos(ang_t)[None]
    sin_t = jnp.sin(ang_t)[None]
    t1 = q3[:, :HALF]
    t2 = q3[:, HALF:]
    qr = jnp.concatenate([t1 * cos_t - t2 * sin_t, t2 * cos_t + t1 * sin_t], axis=1)
    qr_ref[0] = (qr * scale).reshape(qd, tm).astype(BF16)
    gt_ref[0] = jax.nn.sigmoid(qt[qd:])


def _rot_half_cols(w):
    return jnp.concatenate([-w[:, HALF:], w[:, :HALF]], axis=1)


def _projections(x, positions, g_kv, g_q, w_kv, w_q, tm=KEY_TILE):
    b, s, d = x.shape
    gd = N_KV_GROUPS * HEAD_DIM
    qd = N_HEADS * HEAD_DIM
    slot = lambda i: w_kv[:, i * gd:(i + 1) * gd]
    aug = []
    for sl in (2, 4):
        for g in range(N_KV_GROUPS):
            wk = slot(sl)[:, g * HEAD_DIM:(g + 1) * HEAD_DIM]
            aug += [wk, _rot_half_cols(wk)]
    w_nat = jnp.concatenate([slot(0), slot(1)] + aug, axis=1).astype(BF16)
    w_vt = jnp.concatenate([slot(3), slot(5)], axis=1).T.astype(BF16)
    gate_cols = w_q[:, qd:].reshape(d, N_KV_GROUPS, HEADS_PER_GROUP * N_BRANCH)
    gate_cols = jnp.pad(gate_cols, ((0, 0), (0, 0), (0, 16 - HEADS_PER_GROUP * N_BRANCH))).reshape(d, 64)
    w_qt = jnp.concatenate([w_q[:, :qd], gate_cols], axis=1).T.astype(BF16)
    nt = s // tm
    n_nat = w_nat.shape[1]
    const = lambda shape: pl.BlockSpec(shape, lambda i, j: (0,) * len(shape))
    return pl.pallas_call(
        _proj_kernel,
        grid=(b, nt),
        in_specs=[
            pl.BlockSpec((1, tm, d), lambda i, j: (i, j, 0)),
            pl.BlockSpec((1, 1, tm), lambda i, j: (i, 0, j)),
            pl.BlockSpec((1, tm, 1), lambda i, j: (i, j, 0)),
            const((1, d)), const((1, d)), const((d, n_nat)), const((2 * gd, d)), const((qd + 64, d)),
        ],
        out_specs=[
            pl.BlockSpec((1, tm, 2 * gd), lambda i, j: (i, j, 0)),
            pl.BlockSpec((1, N_KV_GROUPS, tm, LANES), lambda i, j: (i, 0, j, 0)),
            pl.BlockSpec((1, N_KV_GROUPS, tm, LANES), lambda i, j: (i, 0, j, 0)),
            pl.BlockSpec((1, N_KV_GROUPS, 1, V_ROWS, tm), lambda i, j: (i, 0, j, 0, 0)),
            pl.BlockSpec((1, N_KV_GROUPS, tm // Q_BLOCK, V_ROWS, Q_BLOCK), lambda i, j: (i, 0, j, 0, 0)),
            pl.BlockSpec((1, qd, tm), lambda i, j: (i, 0, j)),
            pl.BlockSpec((1, qd, tm), lambda i, j: (i, 0, j)),
            pl.BlockSpec((1, 64, tm), lambda i, j: (i, 0, j)),
        ],
        out_shape=[
            jax.ShapeDtypeStruct((b, s, 2 * gd), F32),
            jax.ShapeDtypeStruct((b, N_KV_GROUPS, s, LANES), BF16),
            jax.ShapeDtypeStruct((b, N_KV_GROUPS, s, LANES), BF16),
            jax.ShapeDtypeStruct((b, N_KV_GROUPS, nt, V_ROWS, tm), BF16),
            jax.ShapeDtypeStruct((b, N_KV_GROUPS, s // Q_BLOCK, V_ROWS, Q_BLOCK), BF16),
            jax.ShapeDtypeStruct((b, qd, s), BF16),
            jax.ShapeDtypeStruct((b, qd, s), BF16),
            jax.ShapeDtypeStruct((b, 64, s), F32),
        ],
        compiler_params=_cparams("arbitrary", "arbitrary"),
        name="nsa_projections",
    )(x, positions.reshape(b, 1, s), positions.reshape(b, s, 1), g_kv.reshape(1, d), g_q.reshape(1, d),
      w_nat, w_vt, w_qt)


def _compress_kernel(a_ref, pos_ref, w1_ref, w2_ref, o_ref, *, transposed_out):
    half = w1_ref.shape[0] // 2
    a = a_ref[0, 0]
    top = (a + pos_ref[0:1, :]).astype(BF16)
    bot = (a + pos_ref[1:2, :]).astype(BF16)
    p = _dot(top, w1_ref[:half, :])
    q = _dot(bot, w1_ref[half:, :])
    n = a.shape[0]
    hidden = p + pltpu.roll(q, n - 1, 0)
    act = jax.nn.gelu(hidden, approximate=True).astype(BF16)
    if transposed_out:
        v_t = _dot_nt(w2_ref[...], act)
        o_ref[0, 0] = jnp.concatenate([v_t, _ones_row_block((V_ROWS - HEAD_DIM, n))], axis=0).astype(BF16)
    else:
        o_ref[0, 0] = _dot(act, w2_ref[...]).astype(BF16)


def _compress(a, pos, w1, w2, transposed_out):
    b, g, n, width = a.shape
    hidden = w1.shape[1]
    pos2 = pos.reshape(2, width)
    if transposed_out:
        w2p = w2.T.astype(BF16)
        out_block, out_shape = (1, 1, V_ROWS, n), (b, g, V_ROWS, n)
    else:
        w2p = jnp.pad(w2, ((0, 0), (0, LANES - HEAD_DIM))).astype(BF16)
        out_block, out_shape = (1, 1, n, LANES), (b, g, n, LANES)
    return pl.pallas_call(
        functools.partial(_compress_kernel, transposed_out=transposed_out),
        grid=(b, g),
        in_specs=[
            pl.BlockSpec((1, 1, n, width), lambda i, j: (i, j, 0, 0)),
            pl.BlockSpec((2, width), lambda i, j: (0, 0)),
            pl.BlockSpec((2 * width, hidden), lambda i, j: (0, 0)),
            pl.BlockSpec(w2p.shape, lambda i, j: (0, 0)),
        ],
        out_specs=pl.BlockSpec(out_block, lambda i, j: (i, j, 0, 0)),
        out_shape=jax.ShapeDtypeStruct(out_shape, BF16),
        compiler_params=_cparams("arbitrary", "arbitrary"),
        name="nsa_compress_v" if transposed_out else "nsa_compress_k",
    )(a, pos2, w1.astype(BF16), w2p)


def _heads_on_lanes(ref):
    return jnp.concatenate([ref[0, h * HEAD_DIM:(h + 1) * HEAD_DIM, :] for h in range(HEADS_PER_GROUP)], axis=1)


def _nsa_kernel(qc_ref, qr_ref, g_ref, kc_ref, vct_ref, band_ref, ks_ref, vst_ref, kw_ref, vwt_ref,
                o_ref, bias_ref, score_ref, sa_ref, sb_ref, pa_ref, pb_ref):
    i = pl.program_id(2)
    qb = Q_BLOCK
    width = HEADS_PER_GROUP * qb
    n_cmp = kc_ref.shape[2]
    n_slc = score_ref.shape[0]
    t_q = i * qb + lax.broadcasted_iota(I32, (1, qb), 1)
    t_row = jnp.concatenate([t_q] * HEADS_PER_GROUP, axis=1)
    zeros_pad = jnp.zeros((LANES - HEAD_DIM, width), BF16)

    cc = band_ref.shape[1]
    bpc = cc * CMP_STRIDE // SLC_BLOCK
    qc_t = jnp.concatenate([_heads_on_lanes(qc_ref), zeros_pad], axis=0)
    chunk_scores, maxima = [], []
    for c in range(n_cmp // cc):
        s = _dot(kc_ref[0, 0, c * cc:(c + 1) * cc, :], qc_t)
        cmp_end = (c * cc + lax.broadcasted_iota(I32, (cc, 1), 0)) * CMP_STRIDE + (CMP_BLOCK - 1)
        s = jnp.where(cmp_end <= t_row, s, NEG)
        chunk_scores.append(s)
        maxima.append(jnp.maximum(jnp.max(s, axis=0, keepdims=True), NEG * 1e-10))
    m_fin = functools.reduce(jnp.maximum, maxima)
    shares = []
    for c, (s, mx) in enumerate(zip(chunk_scores, maxima)):
        lhs = jnp.concatenate([vct_ref[0, 0, :, c * cc:(c + 1) * cc], band_ref[...]], axis=0)
        share = _dot(lhs, jnp.exp2(s - mx).astype(BF16))
        shares.append(share * jnp.exp2(mx - m_fin))
    acc_c = shares[0][:V_ROWS]
    for sh in shares[1:]:
        acc_c = acc_c + sh[:V_ROWS]
    rinv = 1.0 / jnp.maximum(acc_c[HEAD_DIM:HEAD_DIM + 1], 1e-30)
    o_cmp = acc_c[:HEAD_DIM] * rinv
    imp_rows = []
    for c, sh in enumerate(shares):
        u = sh[V_ROWS:V_ROWS + bpc]
        if c > 0:
            spill = shares[c - 1][V_ROWS + bpc:V_ROWS + bpc + 8]
            u = jnp.concatenate([u[:8] + spill, u[8:]], axis=0)
        u = u * rinv
        imp_c = u[:, 0:qb]
        for h in range(1, HEADS_PER_GROUP):
            imp_c = imp_c + u[:, h * qb:(h + 1) * qb]
        imp_rows.append(imp_c)
    imp = jnp.concatenate(imp_rows, axis=0)

    j_idx = lax.broadcasted_iota(I32, (n_slc, 1), 0)
    cur = t_q // SLC_BLOCK
    valid = j_idx * SLC_BLOCK <= t_q
    forced = (j_idx == 0) | (j_idx == cur) | (j_idx == cur - 1)
    score_ref[...] = jnp.where(valid & jnp.logical_not(forced), imp, -1.0)

    def pick(_, carry, rows):
        sc = score_ref[0:rows, :]
        best = jnp.max(sc, axis=0, keepdims=True)
        idx = jnp.min(jnp.where(sc == best, j_idx[0:rows], n_slc), axis=0, keepdims=True)
        score_ref[0:rows, :] = jnp.where(j_idx[0:rows] == idx, PICKED, sc)
        return carry

    pick_rows = min(PICK_ROWS, n_slc)
    need = (i * qb + qb - 1) // (SLC_BLOCK * pick_rows)
    for r in range(n_slc // pick_rows):
        @pl.when(need == r)
        def _(rows=(r + 1) * pick_rows):
            lax.fori_loop(0, N_SELECT - N_FORCED, functools.partial(pick, rows=rows), 0)
    bias = jnp.where(forced | (score_ref[...] == PICKED), 0.0, NEG).astype(BF16)
    bias_ref[...] = jnp.concatenate([bias] * HEADS_PER_GROUP, axis=1)

    qr_heads = _heads_on_lanes(qr_ref)
    zeros_sel = jnp.zeros((LANES - HEAD_DIM - BIAS_ROWS, width), BF16)
    tk = KEY_TILE
    blocks_per_tile = tk // SLC_BLOCK

    def scores(kt, s_ref, masked):
        b0 = pl.multiple_of((kt * blocks_per_tile) // BIAS_ROWS * BIAS_ROWS, BIAS_ROWS)
        q_aug = jnp.concatenate([qr_heads, bias_ref[pl.ds(b0, BIAS_ROWS), :], zeros_sel], axis=0)
        k0 = pl.multiple_of(kt * tk, tk)
        s = _dot(ks_ref[0, 0, pl.ds(k0, tk), :], q_aug)
        if masked:
            s = jnp.where(k0 + lax.broadcasted_iota(I32, (tk, 1), 0) <= t_row, s, NEG)
        s_ref[...] = s
        part = jnp.max(s.reshape(tk // MAX_ROWS, MAX_ROWS, width), axis=0)
        return jnp.max(part, axis=0, keepdims=True)

    def values(kt, p_ref):
        return _dot(vst_ref[0, 0, jnp.maximum(kt, 0)], p_ref[...])

    def trip(kk, carry, next_scores):
        m, acc, beta, cmax_a, cmax_b = carry
        kt = 2 * kk
        live = jnp.where(kk > 0, 1.0, 0.0)
        acc = acc + live * (beta * values(kt - 2, pa_ref) + values(kt - 1, pb_ref))
        m1 = jnp.maximum(m, cmax_a)
        pa_ref[...] = jnp.exp2((sa_ref[...] - m1).astype(BF16))
        m2 = jnp.maximum(m1, cmax_b)
        pb_ref[...] = jnp.exp2((sb_ref[...] - m2).astype(BF16))
        beta = jnp.exp2(m1 - m2)
        if next_scores is not None:
            cmax_a = scores(kt + 2, sa_ref, next_scores)
            cmax_b = scores(kt + 3, sb_ref, next_scores)
        return m2, (jnp.exp2(m - m1) * beta) * acc, beta, cmax_a, cmax_b

    @pl.when((pl.program_id(0) == 0) & (pl.program_id(1) == 0) & (i == 0))
    def _():
        pa_ref[...] = jnp.zeros_like(pa_ref)
        pb_ref[...] = jnp.zeros_like(pb_ref)

    n_full = (i * qb) // tk
    n_unmasked = jnp.maximum(n_full // 2 - 1, 0)
    carry = (jnp.full((1, width), NEG, F32), jnp.zeros((V_ROWS, width), F32), jnp.ones((1, width), F32),
             scores(0, sa_ref, True), scores(1, sb_ref, True))
    carry = lax.fori_loop(0, n_unmasked, functools.partial(trip, next_scores=False), carry)
    carry = lax.fori_loop(n_unmasked, n_full // 2, functools.partial(trip, next_scores=True), carry)
    qr_t = jnp.concatenate([qr_heads, zeros_pad], axis=0)
    n_win = WINDOW // qb + 1
    s_w = []
    for r in range(n_win):
        j = i - (n_win - 1) + r
        k0 = pl.multiple_of(jnp.maximum(j, 0) * qb, qb)
        s = _dot(kw_ref[0, 0, pl.ds(k0, qb), :], qr_t)
        if r in (0, n_win - 1):
            kpos = j * qb + lax.broadcasted_iota(I32, (qb, 1), 0)
            diff = t_row - kpos
            s = jnp.where((diff >= 0) & (diff < WINDOW) & (kpos >= 0), s, NEG)
        else:
            s = s + jnp.where(j >= 0, 0.0, NEG)
        s_w.append(s)
    m_w = s_w[0].max(axis=0, keepdims=True)
    for r in range(1, n_win):
        m_w = jnp.maximum(m_w, s_w[r].max(axis=0, keepdims=True))

    last = n_full // 2
    _, acc_s, beta, _, _ = trip(last, carry, None)
    acc_s = acc_s + beta * values(2 * last, pa_ref) + values(2 * last + 1, pb_ref)
    o_slc = acc_s[:HEAD_DIM] / acc_s[HEAD_DIM:HEAD_DIM + 1]

    acc_w = jnp.zeros((V_ROWS, width), F32)
    for r in range(n_win):
        jc = jnp.maximum(i - (n_win - 1) + r, 0)
        acc_w = acc_w + _dot(vwt_ref[0, 0, jc], jnp.exp2((s_w[r] - m_w).astype(BF16)))
    o_win = acc_w[:HEAD_DIM] / acc_w[HEAD_DIM:HEAD_DIM + 1]

    gts = g_ref[0]
    for h in range(HEADS_PER_GROUP):
        sl = slice(h * qb, (h + 1) * qb)
        r0 = h * N_BRANCH
        o = (gts[r0:r0 + 1] * o_cmp[:, sl] + gts[r0 + 1:r0 + 2] * o_slc[:, sl] + gts[r0 + 2:r0 + 3] * o_win[:, sl])
        o_ref[0, h * HEAD_DIM:(h + 1) * HEAD_DIM, :] = o.astype(BF16)


def _nsa_attention(qc_t, qr_t, gates_t, k_cmp, v_cmp_t, band, k_slc, v_slc_t, k_win, v_win_t):
    b, qd, s = qc_t.shape
    g = N_KV_GROUPS
    n_cmp = k_cmp.shape[2]
    n_slc = s // SLC_BLOCK
    nt, tk = v_slc_t.shape[2], v_slc_t.shape[4]
    rows = HEADS_PER_GROUP * HEAD_DIM
    width = HEADS_PER_GROUP * Q_BLOCK
    qspec = pl.BlockSpec((1, rows, Q_BLOCK), lambda bi, gi, i: (bi, gi, i))
    per_bg = lambda shape: pl.BlockSpec((1, 1) + shape, lambda bi, gi, i: (bi, gi) + (0,) * len(shape))
    return pl.pallas_call(
        _nsa_kernel,
        grid=(b, g, s // Q_BLOCK),
        in_specs=[
            qspec, qspec,
            pl.BlockSpec((1, 16, Q_BLOCK), lambda bi, gi, i: (bi, gi, i)),
            per_bg((n_cmp, LANES)), per_bg((V_ROWS, n_cmp)),
            pl.BlockSpec(band.shape, lambda bi, gi, i: (0, 0)),
            per_bg((s, LANES)), per_bg((nt, V_ROWS, tk)),
            per_bg((s, LANES)), per_bg((s // Q_BLOCK, V_ROWS, Q_BLOCK)),
        ],
        out_specs=qspec,
        out_shape=jax.ShapeDtypeStruct((b, qd, s), BF16),
        scratch_shapes=[
            pltpu.VMEM((n_slc, width), BF16),
            pltpu.VMEM((n_slc, Q_BLOCK), F32),
            pltpu.VMEM((tk, width), F32), pltpu.VMEM((tk, width), F32),
            pltpu.VMEM((tk, width), BF16), pltpu.VMEM((tk, width), BF16),
        ],
        compiler_params=_cparams("arbitrary", "arbitrary", "arbitrary"),
        name="nsa_attention",
    )(qc_t, qr_t, gates_t, k_cmp, v_cmp_t, band, k_slc, v_slc_t, k_win, v_win_t)


def _selection_band(n_cmp):
    cc = min(CMP_CHUNK, n_cmp)
    bpc = cc * CMP_STRIDE // SLC_BLOCK
    c0 = jnp.arange(cc) * CMP_STRIDE
    j0 = jnp.arange(bpc + 16) * SLC_BLOCK
    ov = jnp.minimum(c0[None, :] + CMP_BLOCK, j0[:, None] + SLC_BLOCK) - jnp.maximum(c0[None, :], j0[:, None])
    return (jnp.clip(ov, 0).astype(F32) / CMP_BLOCK).astype(BF16)


def _out_proj_kernel(x_ref, ot_ref, w_ref, o_ref):
    o_ref[0] = x_ref[0] + _dot_tn(ot_ref[0], w_ref[...])


def _out_proj(x, o_t, w_out, tm=512):
    b, s, d = x.shape
    qd = o_t.shape[1]
    return pl.pallas_call(
        _out_proj_kernel,
        grid=(b, s // tm),
        in_specs=[
            pl.BlockSpec((1, tm, d), lambda i, j: (i, j, 0)),
            pl.BlockSpec((1, qd, tm), lambda i, j: (i, 0, j)),
            pl.BlockSpec((qd, d), lambda i, j: (0, 0)),
        ],
        out_specs=pl.BlockSpec((1, tm, d), lambda i, j: (i, j, 0)),
        out_shape=jax.ShapeDtypeStruct((b, s, d), F32),
        compiler_params=_cparams("arbitrary", "arbitrary"),
        name="nsa_out_proj",
    )(x, o_t, w_out.astype(BF16))


def _nsa_layer(x, positions, g_kv, g_q, w_kv, cmp_pos_k, cmp_w1_k, cmp_w2_k, cmp_pos_v, cmp_w1_v, cmp_w2_v,
               w_in, w_out):
    b, s, d = x.shape
    g = N_KV_GROUPS
    kcv, k_slc, k_win, v_slc_t, v_win_t, qc_t, qr_t, gates_t = _projections(x, positions, g_kv, g_q, w_kv, w_in)
    n_half = s // CMP_STRIDE
    halves = kcv.reshape(b, n_half, CMP_STRIDE, 2, g, HEAD_DIM).transpose(0, 3, 4, 1, 2, 5)
    halves = halves.reshape(b, 2, g, n_half, CMP_STRIDE * HEAD_DIM)
    k_cmp = _compress(halves[:, 0], cmp_pos_k, cmp_w1_k, cmp_w2_k, transposed_out=False)
    v_cmp_t = _compress(halves[:, 1], cmp_pos_v, cmp_w1_v, cmp_w2_v, transposed_out=True)
    o_t = _nsa_attention(qc_t, qr_t, gates_t, k_cmp, v_cmp_t, _selection_band(n_half), k_slc, v_slc_t, k_win,
                         v_win_t)
    return _out_proj(x, o_t, w_out)


def kernel(x, positions, norm_mix, norm_ffn, conv_w_in, conv_w, conv_w_out, norm_kv, w_kv, cmp_pos_k, cmp_w1_k, cmp_w2_k, cmp_pos_v, cmp_w1_v, cmp_w2_v, attn_w_in, attn_w_out, router_group_w, router_group_b, router_expert_w, router_expert_b, expert_w_gate, expert_w_up, expert_w_down, norm_final):
    b, s, d = x.shape
    assert s % (2 * KEY_TILE) == 0 and (s // SLC_BLOCK) % BIAS_ROWS == 0 and WINDOW % Q_BLOCK == 0

    def moe(xs, l, final_norm):
        return _hier_moe(xs.reshape(b * s, d), norm_ffn[l], router_group_w[l], router_group_b[l],
                         router_expert_w[l], router_expert_b[l], expert_w_gate, expert_w_up,
                         expert_w_down, l, norm_final, final_norm).reshape(b, s, d)

    x = _mixer(x, norm_mix[0], conv_w_in[0], conv_w[0], conv_w_out[0])
    x = moe(x, 0, False)
    x = _nsa_layer(x, positions, norm_kv, norm_mix[1], w_kv, cmp_pos_k, cmp_w1_k, cmp_w2_k,
                   cmp_pos_v, cmp_w1_v, cmp_w2_v, attn_w_in[0], attn_w_out[0])
    return moe(x, 1, True)
```
